```python
import functools
import jax, jax.numpy as jnp
from jax import lax
import numpy as np

D_MODEL = 1024
BATCH = 2
SEQ = 8192
DEPTH = 2
DEC_BATCH = 32
DEC_SEQ = 4
PAST_LEN = 16384
PAGE_SIZE = 128

D_MIX = D_MODEL
D_A = D_MIX // 2
D_B = D_MIX - D_A
HEAD_DIM = 64
N_HEADS_A = D_A // HEAD_DIM
N_HEADS_B = D_B // HEAD_DIM
CHUNK = 128
DILATIONS = ((128, 1), (512, 4), (2048, 16))
MAX_WINDOW = 2048
BAND_BLOCK = 128
ROT_DIM = HEAD_DIM // 4
ROPE_THETA = 500000.0
D_FF = 4 * D_MODEL
D_IN = 2 * D_A + 3 * D_B
EPS = 1e-6
ATTN_SCALE = HEAD_DIM ** -0.5

kernel_name = "hymba_gmlp_dilated_swa_decoder_step"


def rms_norm(x, g):
    xf = x.astype(jnp.float32)
    y = xf * lax.rsqrt(jnp.mean(xf * xf, axis=-1, keepdims=True) + EPS)
    return (y * g.astype(jnp.float32)).astype(x.dtype)


def rope_partial(x, pos):
    half = ROT_DIM // 2
    inv = jnp.power(ROPE_THETA, -jnp.arange(half, dtype=jnp.float32) * (2.0 / ROT_DIM))
    ang = pos.astype(jnp.float32)[:, None] * inv[None, :]
    cos = jnp.cos(ang)[:, None, :]
    sin = jnp.sin(ang)[:, None, :]
    xf = x.astype(jnp.float32)
    x1 = xf[..., :half]
    x2 = xf[..., half:ROT_DIM]
    out = jnp.concatenate([x1 * cos - x2 * sin, x2 * cos + x1 * sin, xf[..., ROT_DIM:]], axis=-1)
    return out.astype(x.dtype)


def mixer_inputs(h, pos, w_in, g_v, g_q, g_k):
    B, T, _ = h.shape
    z = jnp.einsum('btd,de->bte', h, w_in)
    u_a, v_a, q, k, v = jnp.split(z, [D_A, 2 * D_A, 2 * D_A + D_B, 2 * D_A + 2 * D_B], axis=-1)
    u_a = jax.nn.gelu(u_a)
    v_a = rms_norm(jax.nn.gelu(v_a), g_v)
    q = rope_partial(rms_norm(q.reshape(B, T, N_HEADS_B, HEAD_DIM), g_q), pos)
    k = rope_partial(rms_norm(k.reshape(B, T, N_HEADS_B, HEAD_DIM), g_k), pos)
    v = v.reshape(B, T, N_HEADS_B, HEAD_DIM)
    return u_a, v_a, q, k, v


def chunk_gating(u_a, v_a, w_s, b_s):
    B, T, _ = v_a.shape
    c = min(CHUNK, T)
    tril = jnp.tril(jnp.ones((c, c), dtype=bool))
    w = jnp.where(tril, w_s[:, :c, :c], 0).astype(v_a.dtype)
    vc = v_a.reshape(B, T // c, c, N_HEADS_A, HEAD_DIM)
    bias = b_s[:, :c].T[None, None, :, :, None].astype(v_a.dtype)
    s = jnp.einsum('hij,bnjhe->bnihe', w, vc) + bias
    return u_a * s.reshape(B, T, D_A)


def dilated_branch_prompt(q, k, v, window, dil):
    B, S, H, E = q.shape
    L = S // dil
    nk = window // dil
    blk = BAND_BLOCK
    Lp = -(-L // blk) * blk
    nb = Lp // blk

    def to_sub(x):
        x = x.reshape(B, L, dil, H, E).transpose(0, 2, 1, 3, 4)
        return jnp.pad(x, ((0, 0), (0, 0), (0, Lp - L), (0, 0), (0, 0)))

    def band(x):
        xp = jnp.pad(x, ((0, 0), (0, 0), (blk, 0), (0, 0), (0, 0)))
        prev = xp[:, :, :Lp].reshape(B, dil, nb, blk, H, E)
        cur = x.reshape(B, dil, nb, blk, H, E)
        return jnp.concatenate([prev, cur], axis=3)

    def from_sub(x):
        x = x.reshape((B, dil, Lp) + x.shape[4:])[:, :, :L]
        x = jnp.swapaxes(x, 1, 2)
        return x.reshape((B, S) + x.shape[3:])

    qb = to_sub(q).reshape(B, dil, nb, blk, H, E)
    kb = band(to_sub(k))
    vb = band(to_sub(v))
    s = jnp.einsum('bdnqhe,bdnkhe->bdnhqk', qb.astype(jnp.float32), kb.astype(jnp.float32)) * ATTN_SCALE
    qi = jnp.arange(blk)[:, None]
    kj = jnp.arange(2 * blk)[None, :]
    rel = qi + blk - kj
    kidx = jnp.arange(nb)[:, None, None] * blk + kj[None] - blk
    mask = (rel >= 0) & (rel <= nk) & (kidx >= 0)
    s = jnp.where(mask[None, None, :, None], s, -jnp.inf)
    m = jnp.max(s, axis=-1)
    p = jnp.exp(s - m[..., None])
    l = jnp.sum(p, axis=-1)
    acc = jnp.einsum('bdnhqk,bdnkhe->bdnqhe', p, vb.astype(jnp.float32))
    return from_sub(acc), from_sub(jnp.swapaxes(m, 3, 4)), from_sub(jnp.swapaxes(l, 3, 4))


def dilated_branch_sample(q, kc, vc, window, dil, W):
    T = q.shape[1]
    nk = window // dil
    idx = W + jnp.arange(T)[:, None] - dil * jnp.arange(nk + 1)[None, :]
    valid = idx >= 0
    idx = jnp.maximum(idx, 0)
    kg = kc[:, idx].astype(jnp.float32)
    vg = vc[:, idx].astype(jnp.float32)
    s = jnp.einsum('bthe,btkhe->bhtk', q.astype(jnp.float32), kg) * ATTN_SCALE
    s = jnp.where(valid[None, None], s, -jnp.inf)
    m = jnp.max(s, axis=-1)
    p = jnp.exp(s - m[..., None])
    l = jnp.sum(p, axis=-1)
    acc = jnp.einsum('bhtk,btkhe->bthe', p, vg)
    return acc, jnp.transpose(m, (0, 2, 1)), jnp.transpose(l, (0, 2, 1))


def merge_branches(stats):
    m_all = jnp.max(jnp.stack([st[1] for st in stats]), axis=0)
    num = None
    den = None
    for acc, m, l in stats:
        c = jnp.exp(m - m_all)
        num = acc * c[..., None] if num is None else num + acc * c[..., None]
        den = l * c if den is None else den + l * c
    return num / den[..., None]


def attend_prompt(q, k, v):
    stats = [dilated_branch_prompt(q, k, v, win, dil) for win, dil in DILATIONS]
    w = min(MAX_WINDOW, k.shape[1])
    return merge_branches(stats), k[:, -w:], v[:, -w:]


def attend_sample(q, k, v, buf_k, buf_v):
    W = buf_k.shape[1]
    T = q.shape[1]
    kc = jnp.concatenate([buf_k.astype(k.dtype), k], axis=1)
    vc = jnp.concatenate([buf_v.astype(v.dtype), v], axis=1)
    stats = [dilated_branch_sample(q, kc, vc, win, dil, W) for win, dil in DILATIONS]
    return merge_branches(stats), kc[:, T:], vc[:, T:]


def decoder_layer(x, pos, attend, g_attn, w_in, g_v, w_s, b_s, g_q, g_k, g_oa, g_ob, w_out,
                  g_ffn, w_up, w_down):
    B, T, _ = x.shape
    h = rms_norm(x, g_attn)
    u_a, v_a, q, k, v = mixer_inputs(h, pos, w_in, g_v, g_q, g_k)
    y_a = chunk_gating(u_a, v_a, w_s, b_s)
    y_b, k_state, v_state = attend(q, k, v)
    y_b = y_b.astype(x.dtype).reshape(B, T, D_B)
    mix = jnp.concatenate([rms_norm(y_a, g_oa), rms_norm(y_b, g_ob)], axis=-1)
    x = x + jnp.einsum('bte,ed->btd', mix, w_out)
    h = rms_norm(x, g_ffn)
    f = jnp.square(jax.nn.relu(jnp.einsum('btd,df->btf', h, w_up)))
    x = x + jnp.einsum('btf,fd->btd', f, w_down)
    return x, v_a, k_state, v_state


def setup_inputs(seed: int = 0) -> dict:
    key = jax.random.key(seed)
    ks = jax.random.split(key, 18)
    W = min(MAX_WINDOW, PAST_LEN)
    f32 = jnp.float32
    nrm = lambda k, shape: jax.random.normal(k, shape, f32)
    gain = lambda k, shape: 1.0 + 0.01 * nrm(k, shape)
    return {
        "x_prompt": nrm(ks[0], (BATCH, SEQ, D_MODEL)),
        "x_sample": nrm(ks[1], (DEC_BATCH, DEC_SEQ, D_MODEL)),
        "cache_win_k": nrm(ks[2], (DEPTH, DEC_BATCH, W, N_HEADS_B, HEAD_DIM)),
        "cache_win_v": nrm(ks[3], (DEPTH, DEC_BATCH, W, N_HEADS_B, HEAD_DIM)),
        "attn_norm_g": gain(ks[4], (DEPTH, D_MODEL)),
        "w_in": nrm(ks[5], (DEPTH, D_MODEL, D_IN)) * D_MODEL ** -0.5,
        "v_norm_g": gain(ks[6], (DEPTH, D_A)),
        "w_spatial": nrm(ks[7], (DEPTH, N_HEADS_A, CHUNK, CHUNK)) * (0.5 * CHUNK ** -0.5),
        "b_spatial": gain(ks[8], (DEPTH, N_HEADS_A, CHUNK)),
        "q_norm_g": gain(ks[9], (DEPTH, HEAD_DIM)),
        "k_norm_g": gain(ks[10], (DEPTH, HEAD_DIM)),
        "out_norm_a_g": gain(ks[11], (DEPTH, D_A)),
        "out_norm_b_g": gain(ks[12], (DEPTH, D_B)),
        "w_out": nrm(ks[13], (DEPTH, D_MIX, D_MODEL)) * D_MIX ** -0.5,
        "ffn_norm_g": gain(ks[14], (DEPTH, D_MODEL)),
        "w_up": nrm(ks[15], (DEPTH, D_MODEL, D_FF)) * D_MODEL ** -0.5,
        "w_down": nrm(ks[16], (DEPTH, D_FF, D_MODEL)) * D_FF ** -0.5,
    }


def reference(x_prompt, x_sample, cache_win_k, cache_win_v, attn_norm_g, w_in, v_norm_g,
              w_spatial, b_spatial, q_norm_g, k_norm_g, out_norm_a_g, out_norm_b_g, w_out,
              ffn_norm_g, w_up, w_down):
    pos_p = jnp.arange(x_prompt.shape[1])
    pos_s = PAST_LEN + jnp.arange(x_sample.shape[1])
    yp, ys = x_prompt, x_sample
    kp_list, vp_list, ks_list, vs_list, cv_list = [], [], [], [], []
    for l in range(DEPTH):
        params = (attn_norm_g[l], w_in[l], v_norm_g[l], w_spatial[l], b_spatial[l], q_norm_g[l],
                  k_norm_g[l], out_norm_a_g[l], out_norm_b_g[l], w_out[l], ffn_norm_g[l],
                  w_up[l], w_down[l])
        yp, _, kp, vp = decoder_layer(yp, pos_p, attend_prompt, *params)
        attend_s = functools.partial(attend_sample, buf_k=cache_win_k[l], buf_v=cache_win_v[l])
        ys, va_s, k_s, v_s = decoder_layer(ys, pos_s, attend_s, *params)
        kp_list.append(kp)
        vp_list.append(vp)
        ks_list.append(k_s)
        vs_list.append(v_s)
        cv_list.append(va_s)
    new_win_k_prompt = jnp.stack(kp_list)
    new_win_v_prompt = jnp.stack(vp_list)
    new_win_k_sample = jnp.stack(ks_list)
    new_win_v_sample = jnp.stack(vs_list)
    new_chunk_v_sample = jnp.stack(cv_list)
    return (yp, ys, new_win_k_prompt, new_win_v_prompt, new_win_k_sample, new_win_v_sample, new_chunk_v_sample)
```

```python
import functools

import numpy as np
import jax
import jax.numpy as jnp
from jax import lax
from jax.experimental import pallas as pl
from jax.experimental.pallas import tpu as pltpu

D_MODEL = 1024
D_A = 512
D_B = 512
HEAD_DIM = 64
N_HEADS = 8
CHUNK = 128
D_FF = 4 * D_MODEL
ROT_DIM = HEAD_DIM // 4
ROPE_THETA = 500000.0
EPS = 1e-6
ATTN_SCALE = HEAD_DIM ** -0.5
DILATIONS = (1, 4, 16)
BAND = 128
TILE = 16 * BAND
MAX_WINDOW = 2048
PAST_LEN = 16384
T_NEW = 4

LANES = 128
N_SLABS = D_B // LANES
VMEM_LIMIT = 56 * 1024 * 1024

F32 = jnp.float32
BF16 = jnp.bfloat16


def _rms(x, g):
    return x * lax.rsqrt(jnp.mean(x * x, axis=-1, keepdims=True) + EPS) * g


def _gelu(x):
    c = np.float32(np.sqrt(2.0 / np.pi))
    return x * (0.5 * (1.0 + jnp.tanh(c * (x + 0.044715 * (x * x * x)))))


def _dot(a, b):
    return jnp.dot(a, b, preferred_element_type=F32)


def _dot_nt(a, b):
    return lax.dot_general(a, b, (((1,), (1,)), ((), ())), preferred_element_type=F32)


def _params(n_axes):
    return pltpu.CompilerParams(dimension_semantics=("arbitrary",) * n_axes,
                                vmem_limit_bytes=VMEM_LIMIT)


def _const_spec(shape):
    return pl.BlockSpec(shape, lambda *_: (0,) * len(shape), pipeline_mode=pl.Buffered(1))


def _slab(p):
    return slice(p * LANES, (p + 1) * LANES)


def _in_proj_body(x_ref, gattn_ref, win_ref, gv_ref, gq_ref, gk_ref, hsum_ref, cos_ref, sa_ref,
                  sb_ref, wg_ref, bg_ref, goa_ref, *outs, rows, tiles_per_seq):
    dilated = tiles_per_seq is not None
    hb = _rms(x_ref[...], gattn_ref[...]).astype(BF16)

    def proj(i):
        return _dot(hb, win_ref[:, i * 512:(i + 1) * 512])

    def head_norm_rope(z, g_ref):
        ms = _dot((z * z).astype(BF16), hsum_ref[...])
        zn = z * lax.rsqrt(ms + EPS) * g_ref[...]
        cos, sa, sb = cos_ref[...], sa_ref[...], sb_ref[...]
        slabs = []
        for p in range(N_SLABS):
            s = zn[:, _slab(p)]
            slabs.append(s * cos + pltpu.roll(s, LANES - ROT_DIM // 2, 1) * sa
                         + pltpu.roll(s, ROT_DIM // 2, 1) * sb)
        return jnp.concatenate(slabs, axis=1)

    u = _gelu(proj(0))
    va = _rms(_gelu(proj(1)), gv_ref[...])
    q = head_norm_rope(proj(2), gq_ref) * ATTN_SCALE
    k = head_norm_rope(proj(3), gk_ref)
    v = proj(4)

    if dilated:
        ya_ref, q1, q4, q16, k1, k4, k16, v1, v4, v16, kwin_ref, vwin_ref, scr = outs

        def emit(val, flat_ref, r4_ref, r16_ref):
            flat_ref[...] = val.astype(BF16)
            for p in range(N_SLABS):
                scr[p] = val[:, _slab(p)]
            for p in range(N_SLABS):
                for r in range(4):
                    r4_ref[r, :, _slab(p)] = scr[p, pl.ds(r, rows // 4, stride=4), :].astype(BF16)
                for r in range(16):
                    r16_ref[r, :, _slab(p)] = scr[p, pl.ds(r, rows // 16, stride=16), :].astype(BF16)

        emit(q, q1, q4, q16)
        emit(k, k1, k4, k16)
        emit(v, v1, v4, v16)

        @pl.when(pl.program_id(0) % tiles_per_seq >= tiles_per_seq - MAX_WINDOW // rows)
        def _():
            kwin_ref[...] = k.T
            vwin_ref[...] = v.T
    else:
        ya_ref, q1, k32_ref, v32_ref, va_ref = outs
        q1[...] = q.astype(BF16)
        va_ref[...] = va
        k32_ref[...] = k
        v32_ref[...] = v

    lane_lo = lax.broadcasted_iota(jnp.int32, (CHUNK, LANES), 1) < HEAD_DIM
    for c in range(rows // CHUNK):
        rs = slice(c * CHUNK, (c + 1) * CHUNK)
        ys = []
        for p in range(N_SLABS):
            rhs = va[rs, _slab(p)]
            lo = jnp.where(lane_lo, rhs, 0.0).astype(BF16)
            hi = jnp.where(lane_lo, 0.0, rhs).astype(BF16)
            s = _dot(wg_ref[2 * p], lo) + _dot(wg_ref[2 * p + 1], hi) + bg_ref[:, _slab(p)]
            ys.append(u[rs, _slab(p)] * s)
        ya_ref[rs, :] = _rms(jnp.concatenate(ys, axis=1), goa_ref[...]).astype(BF16)


def _in_proj(x, lw, tables, gate_w, gate_b, *, rows, batch=None, seq=None):
    n = x.shape[0]
    dilated = batch is not None
    cos, sa, sb = tables
    n_tab = cos.shape[0] // rows
    row_spec = lambda w: pl.BlockSpec((rows, w), lambda i: (i, 0))
    tab_spec = pl.BlockSpec((rows, LANES), lambda i: (i % n_tab, 0))
    flat = lambda dt: jax.ShapeDtypeStruct((n, 512), dt)
    scratch = []
    per_b = None
    if dilated:
        per_b = seq // rows
        split_shape = lambda d: jax.ShapeDtypeStruct((batch, d, seq // d, D_B), BF16)
        split_spec = lambda d: pl.BlockSpec((None, d, rows // d, D_B),
                                            lambda i: (i // per_b, 0, i % per_b, 0))
        qkv_shape = [flat(BF16), split_shape(4), split_shape(16)]
        qkv_spec = [row_spec(512), split_spec(4), split_spec(16)]
        first_win = per_b - MAX_WINDOW // rows
        win_shape = jax.ShapeDtypeStruct((batch, D_B, MAX_WINDOW), F32)
        win_spec = pl.BlockSpec((None, D_B, rows),
                                lambda i: (i // per_b, 0, jnp.maximum(i % per_b - first_win, 0)))
        out_shape = [flat(BF16)] + qkv_shape * 3 + [win_shape] * 2
        out_specs = [row_spec(512)] + qkv_spec * 3 + [win_spec] * 2
        scratch = [pltpu.VMEM((N_SLABS, rows, LANES), F32)]
    else:
        out_shape = [flat(BF16), flat(BF16), flat(F32), flat(F32), flat(F32)]
        out_specs = [row_spec(512)] * 5
    return pl.pallas_call(
        functools.partial(_in_proj_body, rows=rows, tiles_per_seq=per_b),
        grid=(n // rows,),
        in_specs=[row_spec(D_MODEL), _const_spec((1, D_MODEL)), _const_spec((D_MODEL, 5 * 512)),
                  _const_spec((1, D_A)), _const_spec((1, D_B)), _const_spec((1, D_B)),
                  _const_spec((D_B, D_B)), tab_spec, tab_spec, tab_spec,
                  _const_spec((N_HEADS, CHUNK, CHUNK)), _const_spec((CHUNK, D_A)),
                  _const_spec((1, D_A))],
        out_specs=out_specs,
        out_shape=out_shape,
        scratch_shapes=scratch,
        compiler_params=_params(1),
        name="in_proj",
    )(x, lw["g_attn"], lw["w_in"], lw["g_v"], lw["g_q"], lw["g_k"], lw["hsum"], cos, sa, sb,
      gate_w, gate_b, lw["g_oa"])


def _band_block(q, kp, kc, vp, vc, bias):
    lane_lo = lax.broadcasted_iota(jnp.int32, (BAND, LANES), 1) < HEAD_DIM
    head_mask = [(lax.broadcasted_iota(jnp.int32, (1, LANES), 1) // HEAD_DIM == hh)
                 .astype(F32).astype(BF16) for hh in range(2)]
    out = []
    for p in range(N_SLABS):
        qp = q[:, _slab(p)]
        kk = jnp.concatenate([kp[:, _slab(p)], kc[:, _slab(p)]], axis=0)
        vv = jnp.concatenate([vp[:, _slab(p)], vc[:, _slab(p)]], axis=0)
        acc, ms, ls = [], [], []
        for hh in range(2):
            s = _dot_nt(qp * head_mask[hh], kk) + bias
            m = jnp.max(s, axis=-1, keepdims=True)
            pr = jnp.exp(s - m)
            ms.append(m)
            ls.append(jnp.sum(pr, axis=-1, keepdims=True))
            acc.append(_dot(pr.astype(BF16), vv))
        out.append(tuple(jnp.where(lane_lo, a, b) for a, b in (acc, ms, ls)))
    return out


def _attn_prompt_body(q1, k1c, k1p, v1c, v1p, q4, k4c, k4p, v4c, v4p, q16, k16c, k16p, v16c, v16p,
                      bias_ref, gob_ref, out_ref,
                      a1, m1, l1, a4, m4, l4, a16, m16, l16):
    j = pl.program_id(1)
    c = pl.program_id(2)
    n_sub = pl.num_programs(2)

    def run(q, kc, kp, vc, vp, first, acc_ref, m_ref, l_ref, row0):
        bias = bias_ref[first.astype(jnp.int32)]
        res = _band_block(q[...], kp[...], kc[...], vp[...], vc[...], bias)
        rows = pl.ds(pl.multiple_of(row0, BAND), BAND)
        for p in range(N_SLABS):
            acc_ref[p, rows, :], m_ref[p, rows, :], l_ref[p, rows, :] = res[p]

    run(q1, k1c, k1p, v1c, v1p, (16 * j + c) == 0, a1, m1, l1, c * BAND)
    run(q4, k4c, k4p, v4c, v4p, (4 * j + c // 4) == 0, a4, m4, l4,
        (c % 4) * (4 * BAND) + (c // 4) * BAND)
    run(q16, k16c, k16p, v16c, v16p, j == 0, a16, m16, l16, c * BAND)

    @pl.when(c == n_sub - 1)
    def _finalize():
        for r in range(16):
            rows16 = pl.ds(r * BAND, BAND)
            rows4 = pl.ds((r % 4) * (4 * BAND) + r // 4, BAND, stride=4)
            rows1 = pl.ds(r, BAND, stride=16)
            parts = [(a1, m1, l1, rows1), (a4, m4, l4, rows4), (a16, m16, l16, rows16)]
            slabs = []
            for p in range(N_SLABS):
                mm = [m_ref[p, rw, :] for (_, m_ref, _, rw) in parts]
                m_all = jnp.maximum(jnp.maximum(mm[0], mm[1]), mm[2])
                num = den = None
                for m, (a_ref, _, l_ref, rw) in zip(mm, parts):
                    cf = jnp.exp(m - m_all)
                    tn, td = a_ref[p, rw, :] * cf, l_ref[p, rw, :] * cf
                    num, den = (tn, td) if num is None else (num + tn, den + td)
                slabs.append(num / den)
            y = _rms(jnp.concatenate(slabs, axis=1), gob_ref[...])
            for p in range(N_SLABS):
                a1[p, rows1, :] = y[:, _slab(p)]
        for p in range(N_SLABS):
            out_ref[:, _slab(p)] = a1[p].astype(BF16)


def _attn_prompt(qkv, bias, g_ob, batch, seq):
    n_tiles = seq // TILE
    blk1 = (None, BAND, D_B)
    blkd = (None, None, BAND, D_B)
    prev = lambda i: jnp.maximum(i - 1, 0)
    row1 = lambda j, c: 16 * j + c
    row4 = lambda j, c: 4 * j + c // 4
    maps = {
        1: (blk1, lambda b, j, c: (b, row1(j, c), 0), lambda b, j, c: (b, prev(row1(j, c)), 0)),
        4: (blkd, lambda b, j, c: (b, c % 4, row4(j, c), 0),
            lambda b, j, c: (b, c % 4, prev(row4(j, c)), 0)),
        16: (blkd, lambda b, j, c: (b, c, j, 0), lambda b, j, c: (b, c, prev(j), 0)),
    }
    in_specs, args = [], []
    for di, d in enumerate(DILATIONS):
        blk, cur, prv = maps[d]
        cs, ps = pl.BlockSpec(blk, cur), pl.BlockSpec(blk, prv)
        in_specs += [cs, cs, ps, cs, ps]
        qd, kd, vd = (t[di] if d > 1 else t[di].reshape(batch, seq, D_B) for t in qkv)
        args += [qd, kd, kd, vd, vd]
    in_specs += [_const_spec((2, BAND, 2 * BAND)), _const_spec((1, D_B))]
    args += [bias, g_ob]
    out = pl.pallas_call(
        _attn_prompt_body,
        grid=(batch, n_tiles, 16),
        in_specs=in_specs,
        out_specs=pl.BlockSpec((None, TILE, D_B), lambda b, j, c: (b, j, 0)),
        out_shape=jax.ShapeDtypeStruct((batch, seq, D_B), BF16),
        scratch_shapes=[pltpu.VMEM((N_SLABS, TILE, LANES), F32)] * 9,
        compiler_params=_params(3),
        name="attn_prompt",
    )(*args)
    return out.reshape(batch * seq, D_B)


def _attn_sample_body(kt_ref, vt_ref, kn_ref, vn_ref, q_ref, mult_ref, *rest):
    okt_ref, ovt_ref, o_ref = rest[-3:]
    body = MAX_WINDOW - LANES
    is_new = lax.broadcasted_iota(jnp.int32, (HEAD_DIM, LANES), 1) >= LANES - T_NEW
    mult = mult_ref[...]

    def shift(src_ref, new_ref, dst_ref, h):
        x = src_ref[h]
        y = pltpu.roll(x, MAX_WINDOW - T_NEW, 1)
        tail = jnp.where(is_new, new_ref[h], y[:, body:])
        dst_ref[h, :, 0:body] = y[:, 0:body]
        dst_ref[h, :, body:MAX_WINDOW] = tail
        return jnp.concatenate([y[:, 0:body], tail, x[:, 0:LANES]], axis=1).astype(BF16)

    for h in range(N_HEADS):
        kcat = shift(kt_ref, kn_ref, okt_ref, h)
        vcat = shift(vt_ref, vn_ref, ovt_ref, h)
        s = jnp.where(mult > 0, _dot(q_ref[h], kcat), -jnp.inf)
        p = mult * jnp.exp(s - jnp.max(s, axis=-1, keepdims=True))
        o_ref[h] = _dot_nt(p.astype(BF16), vcat) / jnp.sum(p, axis=-1, keepdims=True)


def _attn_sample(layer, cache_kt, cache_vt, k_new, v_new, q, mult, prev_out):
    depth, db = cache_kt.shape[:2]
    win_blk = (None, None, N_HEADS, HEAD_DIM, MAX_WINDOW)
    cache_spec = pl.BlockSpec(win_blk, lambda b: (layer, b, 0, 0, 0))
    new_spec = pl.BlockSpec((None, N_HEADS, HEAD_DIM, LANES), lambda b: (b, 0, 0, 0))
    q_spec = pl.BlockSpec((None, N_HEADS, 8, HEAD_DIM), lambda b: (b, 0, 0, 0))
    in_specs = [cache_spec, cache_spec, new_spec, new_spec, q_spec,
                _const_spec((8, MAX_WINDOW + LANES))]
    args = [cache_kt, cache_vt, k_new, v_new, q, mult]
    aliases = {}
    if prev_out is not None:
        in_specs += [pl.BlockSpec(memory_space=pl.ANY)] * 2
        aliases = {len(args): 0, len(args) + 1: 1}
        args += list(prev_out)
    win_shape = jax.ShapeDtypeStruct(cache_kt.shape, F32)
    return pl.pallas_call(
        _attn_sample_body,
        grid=(db,),
        in_specs=in_specs,
        out_specs=[cache_spec, cache_spec, q_spec],
        out_shape=[win_shape, win_shape, jax.ShapeDtypeStruct((db, N_HEADS, 8, HEAD_DIM), F32)],
        input_output_aliases=aliases,
        compiler_params=_params(1),
        name="attn_sample",
    )(*args)


def _sample_multiplicity():
    def mult(d):
        ok = d >= 0
        return (ok & (d <= 128)).astype(np.int32) + (ok & (d % 4 == 0) & (d <= 512)) \
            + (ok & (d % 16 == 0) & (d <= 2048))
    t = (np.arange(8) % T_NEW)[:, None]
    kept = mult(MAX_WINDOW - T_NEW + t - np.arange(MAX_WINDOW)[None, :])
    old = np.arange(LANES)[None, :]
    dropped = mult(MAX_WINDOW + t - old) * (old < T_NEW)
    return np.concatenate([kept, dropped], axis=1).astype(np.float32)


def _out_ffn_body(x_ref, ya_ref, yb_ref, gob_ref, wout_ref, gffn_ref, wup_ref, wdown_ref, o_ref,
                  *, norm_b):
    yb = yb_ref[...]
    if norm_b:
        yb = _rms(yb, gob_ref[...])
    x1 = x_ref[...] + _dot(ya_ref[...], wout_ref[0:D_A, :]) \
        + _dot(yb.astype(BF16), wout_ref[D_A:D_A + D_B, :])
    hb = _rms(x1, gffn_ref[...]).astype(BF16)
    step = D_FF // 4
    mlp = None
    for c in range(4):
        f = _dot(hb, wup_ref[:, c * step:(c + 1) * step])
        f = jnp.square(jnp.maximum(f, 0.0)).astype(BF16)
        d = _dot(f, wdown_ref[c * step:(c + 1) * step, :])
        mlp = d if mlp is None else mlp + d
    o_ref[...] = x1 + mlp


def _out_ffn(x, ya, yb, lw, *, rows, norm_b):
    n = x.shape[0]
    row_spec = lambda w: pl.BlockSpec((rows, w), lambda i: (i, 0))
    return pl.pallas_call(
        functools.partial(_out_ffn_body, norm_b=norm_b),
        grid=(n // rows,),
        in_specs=[row_spec(D_MODEL), row_spec(D_A), row_spec(D_B), _const_spec((1, D_B)),
                  _const_spec((D_A + D_B, D_MODEL)), _const_spec((1, D_MODEL)),
                  _const_spec((D_MODEL, D_FF)), _const_spec((D_FF, D_MODEL))],
        out_specs=row_spec(D_MODEL),
        out_shape=jax.ShapeDtypeStruct((n, D_MODEL), F32),
        compiler_params=_params(1),
        name="out_ffn",
    )(x, ya, yb, lw["g_ob"], lw["w_out"], lw["g_ffn"], lw["w_up"], lw["w_down"])


def _rope_tables(pos):
    half = ROT_DIM // 2
    inv = jnp.power(ROPE_THETA, -jnp.arange(half, dtype=F32) * (2.0 / ROT_DIM))
    ang = pos.astype(F32)[:, None] * inv[None, :]
    cos, sin = jnp.cos(ang), jnp.sin(ang)
    n = pos.shape[0]
    pad = HEAD_DIM - ROT_DIM
    head = lambda a, b, fill: jnp.concatenate([a, b, jnp.full((n, pad), fill, F32)], axis=1)
    zero = jnp.zeros((n, half), F32)
    two = lambda t: jnp.concatenate([t, t], axis=1)
    return two(head(cos, cos, 1.0)), two(head(-sin, zero, 0.0)), two(head(zero, sin, 0.0))


def kernel(x_prompt, x_sample, cache_win_k, cache_win_v, attn_norm_g, w_in, v_norm_g, w_spatial, b_spatial, q_norm_g, k_norm_g, out_norm_a_g, out_norm_b_g, w_out, ffn_norm_g, w_up, w_down):
    depth = w_in.shape[0]
    batch, seq, _ = x_prompt.shape
    db, t_new, _ = x_sample.shape
    n_s = db * t_new
    assert n_s == CHUNK and t_new == T_NEW and cache_win_k.shape[2] == MAX_WINDOW
    assert seq % TILE == 0

    tab_p = _rope_tables(jnp.arange(seq))
    tab_s = _rope_tables(jnp.tile(PAST_LEN + jnp.arange(t_new), db))

    tril = jnp.tril(jnp.ones((CHUNK, CHUNK), bool))
    tril_s = jnp.tril(jnp.ones((t_new, t_new), bool))
    hsum = jnp.kron(jnp.eye(N_HEADS, dtype=F32), jnp.full((HEAD_DIM, HEAD_DIM), 1.0 / HEAD_DIM, F32)).astype(BF16)

    qi = np.arange(BAND)[:, None]
    kj = np.arange(2 * BAND)[None, :]
    ok = (kj >= qi) & (kj <= qi + BAND)
    bias = jnp.asarray(np.stack([np.where(ok, 0.0, -np.inf), np.where(ok & (kj >= BAND), 0.0, -np.inf)]), F32)
    mult = jnp.asarray(_sample_multiplicity())

    cache_kt = cache_win_k.transpose(0, 1, 3, 4, 2)
    cache_vt = cache_win_v.transpose(0, 1, 3, 4, 2)

    xp = x_prompt.reshape(batch * seq, D_MODEL)
    xs = x_sample.reshape(n_s, D_MODEL)
    kp_list, vp_list, cv_list = [], [], []
    win_out = None
    row = lambda g: g.reshape(1, -1)
    for l in range(depth):
        lw = dict(g_attn=row(attn_norm_g[l]), w_in=w_in[l].astype(BF16), g_v=row(v_norm_g[l]),
                  g_q=row(jnp.tile(q_norm_g[l], N_HEADS)), g_k=row(jnp.tile(k_norm_g[l], N_HEADS)),
                  hsum=hsum, g_oa=row(out_norm_a_g[l]), g_ob=row(out_norm_b_g[l]),
                  w_out=w_out[l].astype(BF16), g_ffn=row(ffn_norm_g[l]),
                  w_up=w_up[l].astype(BF16), w_down=w_down[l].astype(BF16))
        gate_w_p = jnp.where(tril, w_spatial[l], 0).astype(BF16)
        gate_b_p = jnp.repeat(b_spatial[l].T, HEAD_DIM, axis=1)
        w4 = jnp.where(tril_s, w_spatial[l][:, :t_new, :t_new], 0)
        gate_w_s = (jnp.eye(db, dtype=F32)[None, :, None, :, None] * w4[:, None, :, None, :]) \
            .reshape(N_HEADS, CHUNK, CHUNK).astype(BF16)
        gate_b_s = jnp.tile(jnp.repeat(b_spatial[l][:, :t_new].T, HEAD_DIM, axis=1), (db, 1))

        outs = _in_proj(xp, lw, tab_p, gate_w_p, gate_b_p, rows=256, batch=batch, seq=seq)
        ya, qkv, k32, v32 = outs[0], (outs[1:4], outs[4:7], outs[7:10]), outs[10], outs[11]
        yb = _attn_prompt(qkv, bias, lw["g_ob"], batch, seq)
        xp = _out_ffn(xp, ya, yb, lw, rows=512, norm_b=False)
        kp_list.append(k32.reshape(batch, N_HEADS, HEAD_DIM, MAX_WINDOW))
        vp_list.append(v32.reshape(batch, N_HEADS, HEAD_DIM, MAX_WINDOW))

        ya, q, k32, v32, va = _in_proj(xs, lw, tab_s, gate_w_s, gate_b_s, rows=CHUNK)
        cv_list.append(va.reshape(db, t_new, D_A))
        heads = lambda t: t.reshape(db, t_new, N_HEADS, HEAD_DIM)
        new_t = lambda t: jnp.pad(heads(t).transpose(0, 2, 3, 1), ((0, 0), (0, 0), (0, 0), (LANES - t_new, 0)))
        q8 = jnp.tile(heads(q).transpose(0, 2, 1, 3), (1, 1, 2, 1))
        okt, ovt, o = _attn_sample(l, cache_kt, cache_vt, new_t(k32), new_t(v32), q8, mult, win_out)
        win_out = (okt, ovt)
        yb = o[:, :, :t_new].transpose(0, 2, 1, 3).reshape(n_s, D_B)
        xs = _out_ffn(xs, ya, yb, lw, rows=CHUNK, norm_b=True)

    unt = lambda w: w.transpose(0, 1, 4, 2, 3)
    return (xp.reshape(batch, seq, D_MODEL), xs.reshape(db, t_new, D_MODEL),
            unt(jnp.stack(kp_list)), unt(jnp.stack(vp_list)), unt(win_out[0]), unt(win_out[1]),
            jnp.stack(cv_list))
```

```python
import functools

import numpy as np
import jax
import jax.numpy as jnp
from jax import lax
from jax.experimental import pallas as pl
from jax.experimental.pallas import tpu as pltpu

D_MODEL = 1024
D_A = 512
D_B = 512
HEAD_DIM = 64
N_HEADS = 8
CHUNK = 128
D_FF = 4 * D_MODEL
ROT_DIM = HEAD_DIM // 4
ROPE_THETA = 500000.0
EPS = 1e-6
ATTN_SCALE = HEAD_DIM ** -0.5
DILATIONS = (1, 4, 16)
BAND = 128
TILE = 16 * BAND
MAX_WINDOW = 2048
PAST_LEN = 16384
T_NEW = 4

LANES = 128
N_SLABS = D_B // LANES
VMEM_LIMIT = 56 * 1024 * 1024

F32 = jnp.float32
BF16 = jnp.bfloat16


def _rms(x, g):
    return x * lax.rsqrt(jnp.mean(x * x, axis=-1, keepdims=True) + EPS) * g


def _gelu(x):
    c = np.float32(np.sqrt(2.0 / np.pi))
    return x * (0.5 * (1.0 + jnp.tanh(c * (x + 0.044715 * (x * x * x)))))


def _dot(a, b):
    return jnp.dot(a, b, preferred_element_type=F32)


def _dot_nt(a, b):
    return lax.dot_general(a, b, (((1,), (1,)), ((), ())), preferred_element_type=F32)


def _params(n_axes):
    return pltpu.CompilerParams(dimension_semantics=("arbitrary",) * n_axes,
                                vmem_limit_bytes=VMEM_LIMIT)


def _const_spec(shape):
    return pl.BlockSpec(shape, lambda *_: (0,) * len(shape), pipeline_mode=pl.Buffered(1))


def _slab(p):
    return slice(p * LANES, (p + 1) * LANES)


def _in_proj_body(x_ref, gattn_ref, win_ref, gv_ref, gq_ref, gk_ref, hsum_ref, tbase_ref, toff_ref,
                  tsign_ref, wg_ref, bg_ref, goa_ref, *outs, rows, tiles_per_seq):
    dilated = tiles_per_seq is not None
    hb = _rms(x_ref[...], gattn_ref[...]).astype(BF16)

    cb, sb = tbase_ref[0:1, :], tbase_ref[1:2, :]
    co, so = toff_ref[0], toff_ref[1]
    cos = cb * co - sb * so
    sin = sb * co + cb * so
    sin_up, sin_dn = sin * tsign_ref[0:1, :], sin * tsign_ref[1:2, :]

    def proj(i):
        return _dot(hb, win_ref[:, i * 512:(i + 1) * 512])

    def head_norm_rope(z, g_ref):
        ms = _dot((z * z).astype(BF16), hsum_ref[...])
        zn = z * lax.rsqrt(ms + EPS) * g_ref[...]
        slabs = []
        for p in range(N_SLABS):
            s = zn[:, _slab(p)]
            slabs.append(s * cos + pltpu.roll(s, LANES - ROT_DIM // 2, 1) * sin_up
                         + pltpu.roll(s, ROT_DIM // 2, 1) * sin_dn)
        return jnp.concatenate(slabs, axis=1)

    u = _gelu(proj(0))
    va = _rms(_gelu(proj(1)), gv_ref[...])
    q = head_norm_rope(proj(2), gq_ref) * ATTN_SCALE
    k = head_norm_rope(proj(3), gk_ref)
    v = proj(4)

    if dilated:
        ya_ref, q16, k1, k4, k16, v1, v4, v16, kwin_ref, vwin_ref, scr1, scr4 = outs

        def split(val, r4_ref, r16_ref):
            for p in range(N_SLABS):
                scr1[p] = val[:, _slab(p)]
            n4, n16 = rows // 4, rows // 16
            for p in range(N_SLABS):
                for r in range(4):
                    part = scr1[p, pl.ds(r, n4, stride=4), :]
                    scr4[p, r * n4:(r + 1) * n4, :] = part
                    if r4_ref is not None:
                        r4_ref[r, :, _slab(p)] = part.astype(BF16)
                for r in range(4):
                    for c in range(4):
                        part = scr4[p, pl.ds(r * n4 + c, n16, stride=4), :]
                        r16_ref[r + 4 * c, :, _slab(p)] = part.astype(r16_ref.dtype)

        split(q, None, q16)
        k1[...] = k.astype(BF16)
        split(k, k4, k16)
        v1[...] = v.astype(BF16)
        split(v, v4, v16)

        @pl.when(pl.program_id(0) % tiles_per_seq >= tiles_per_seq - MAX_WINDOW // rows)
        def _():
            kwin_ref[...] = k.T
            vwin_ref[...] = v.T
    else:
        ya_ref, q1, k32_ref, v32_ref, va_ref = outs
        q1[...] = q.astype(BF16)
        va_ref[...] = va
        k32_ref[...] = k
        v32_ref[...] = v

    lane_lo = lax.broadcasted_iota(jnp.int32, (CHUNK, LANES), 1) < HEAD_DIM
    for c in range(rows // CHUNK):
        rs = slice(c * CHUNK, (c + 1) * CHUNK)
        ys = []
        for p in range(N_SLABS):
            rhs = va[rs, _slab(p)]
            lo = jnp.where(lane_lo, rhs, 0.0).astype(BF16)
            hi = jnp.where(lane_lo, 0.0, rhs).astype(BF16)
            s = _dot(wg_ref[2 * p], lo) + _dot(wg_ref[2 * p + 1], hi) + bg_ref[:, _slab(p)]
            ys.append(u[rs, _slab(p)] * s)
        ya_ref[rs, :] = _rms(jnp.concatenate(ys, axis=1), goa_ref[...]).astype(BF16)


def _in_proj(x, lw, tables, gate_w, gate_b, *, rows, batch=None, seq=None):
    n = x.shape[0]
    dilated = batch is not None
    tbase, toff, tsign = tables
    n_base = tbase.shape[0]
    row_spec = lambda w: pl.BlockSpec((rows, w), lambda i: (i, 0))
    flat = lambda dt: jax.ShapeDtypeStruct((n, 512), dt)
    scratch = []
    per_b = None
    if dilated:
        per_b = seq // rows
        split_shape = lambda d, dt: jax.ShapeDtypeStruct((batch, d, seq // d, D_B), dt)
        split_spec = lambda d: pl.BlockSpec((None, d, rows // d, D_B),
                                            lambda i: (i // per_b, 0, i % per_b, 0))
        kv_shape = [flat(BF16), split_shape(4, BF16), split_shape(16, BF16)]
        kv_spec = [row_spec(512), split_spec(4), split_spec(16)]
        first_win = per_b - MAX_WINDOW // rows
        win_shape = jax.ShapeDtypeStruct((batch, D_B, MAX_WINDOW), F32)
        win_spec = pl.BlockSpec((None, D_B, rows),
                                lambda i: (i // per_b, 0, jnp.maximum(i % per_b - first_win, 0)))
        out_shape = [flat(BF16), split_shape(16, F32)] + kv_shape * 2 + [win_shape] * 2
        out_specs = [row_spec(512), split_spec(16)] + kv_spec * 2 + [win_spec] * 2
        scratch = [pltpu.VMEM((N_SLABS, rows, LANES), F32)] * 2
    else:
        out_shape = [flat(BF16), flat(BF16), flat(F32), flat(F32), flat(F32)]
        out_specs = [row_spec(512)] * 5
    return pl.pallas_call(
        functools.partial(_in_proj_body, rows=rows, tiles_per_seq=per_b),
        grid=(n // rows,),
        in_specs=[row_spec(D_MODEL), _const_spec((1, D_MODEL)), _const_spec((D_MODEL, 5 * 512)),
                  _const_spec((1, D_A)), _const_spec((1, D_B)), _const_spec((1, D_B)),
                  _const_spec((D_B, D_B)),
                  pl.BlockSpec((None, 2, LANES), lambda i: (i % n_base, 0, 0)),
                  _const_spec((2, rows, LANES)), _const_spec((2, LANES)),
                  _const_spec((N_HEADS, CHUNK, CHUNK)), _const_spec((CHUNK, D_A)),
                  _const_spec((1, D_A))],
        out_specs=out_specs,
        out_shape=out_shape,
        scratch_shapes=scratch,
        compiler_params=_params(1),
        name="in_proj",
    )(x, lw["g_attn"], lw["w_in"], lw["g_v"], lw["g_q"], lw["g_k"], lw["hsum"], tbase, toff, tsign,
      gate_w, gate_b, lw["g_oa"])


def _band_block(q, kp, kc, vp, vc, bias):
    lane_lo = lax.broadcasted_iota(jnp.int32, (BAND, LANES), 1) < HEAD_DIM
    head_mask = [(lax.broadcasted_iota(jnp.int32, (1, LANES), 1) // HEAD_DIM == hh)
                 .astype(F32).astype(BF16) for hh in range(2)]
    out = []
    for p in range(N_SLABS):
        qp = q[:, _slab(p)]
        kk = jnp.concatenate([kp[:, _slab(p)], kc[:, _slab(p)]], axis=0)
        vv = jnp.concatenate([vp[:, _slab(p)], vc[:, _slab(p)]], axis=0)
        acc, ms, ls = [], [], []
        for hh in range(2):
            s = _dot_nt(qp * head_mask[hh], kk) + bias
            m = jnp.max(s, axis=-1, keepdims=True)
            pr = jnp.exp(s - m)
            ms.append(m)
            ls.append(jnp.sum(pr, axis=-1, keepdims=True))
            acc.append(_dot(pr.astype(BF16), vv))
        out.append(tuple(jnp.where(lane_lo, a, b) for a, b in (acc, ms, ls)))
    return out


def _attn_prompt_body(q1, k1c, k1p, v1c, v1p, q4, k4c, k4p, v4c, v4p, q16, k16c, k16p, v16c, v16p,
                      bias_ref, gob_ref, out_ref,
                      a1, m1, l1, a4, m4, l4, a16, m16, l16):
    j = pl.program_id(1)
    c = pl.program_id(2)
    n_sub = pl.num_programs(2)

    def run(branch, q, kc, kp, vc, vp, first, refs, runs):
        bias = bias_ref[branch, first.astype(jnp.int32)]
        res = _band_block(q[...].reshape(BAND, D_B).astype(BF16), kp[...], kc[...], vp[...],
                          vc[...], bias)
        n = BAND // len(runs)
        for p in range(N_SLABS):
            for ref, val in zip(refs, res[p]):
                for i, row0 in enumerate(runs):
                    ref[p, pl.ds(pl.multiple_of(row0, n), n), :] = val[i * n:(i + 1) * n]

    run(0, q1, k1c, k1p, v1c, v1p, (16 * j + c) == 0, (a1, m1, l1),
        [BAND * r + 8 * c for r in range(16)])
    run(1, q4, k4c, k4p, v4c, v4p, (4 * j + c // 4) == 0, (a4, m4, l4),
        [BAND * (4 * cc + c % 4) + 32 * (c // 4) for cc in range(4)])
    run(2, q16, k16c, k16p, v16c, v16p, j == 0, (a16, m16, l16), [BAND * c])

    @pl.when(c == n_sub - 1)
    def _finalize():
        parts = [(a1, m1, l1), (a4, m4, l4), (a16, m16, l16)]
        for r in range(16):
            rows = pl.ds(r * BAND, BAND)
            slabs = []
            for p in range(N_SLABS):
                mm = [m_ref[p, rows, :] for (_, m_ref, _) in parts]
                m_all = jnp.maximum(jnp.maximum(mm[0], mm[1]), mm[2])
                num = den = None
                for m, (a_ref, _, l_ref) in zip(mm, parts):
                    cf = jnp.exp(m - m_all)
                    tn, td = a_ref[p, rows, :] * cf, l_ref[p, rows, :] * cf
                    num, den = (tn, td) if num is None else (num + tn, den + td)
                slabs.append(num / den)
            y = _rms(jnp.concatenate(slabs, axis=1), gob_ref[...])
            for p in range(N_SLABS):
                a16[p, rows, :] = y[:, _slab(p)]
        for p in range(N_SLABS):
            for r in range(16):
                a1[p, pl.ds(r, BAND, stride=16), :] = a16[p, r * BAND:(r + 1) * BAND, :]
            out_ref[:, _slab(p)] = a1[p].astype(BF16)


def _attn_prompt(q16, kv, bias, g_ob, batch, seq):
    n_tiles = seq // TILE
    prev = lambda i: jnp.maximum(i - 1, 0)
    row1 = lambda j, c: 16 * j + c
    row4 = lambda j, c: 4 * j + c // 4
    blkd = (None, None, BAND, D_B)
    branches = [
        (pl.BlockSpec((None, 16, 8, D_B), lambda b, j, c: (b, 0, row1(j, c), 0)), q16,
         (None, BAND, D_B), lambda b, j, c: (b, row1(j, c), 0), lambda b, j, c: (b, prev(row1(j, c)), 0)),
        (pl.BlockSpec((None, 4, None, 32, D_B), lambda b, j, c: (b, 0, c % 4, row4(j, c), 0)),
         q16.reshape(batch, 4, 4, seq // 16, D_B),
         blkd, lambda b, j, c: (b, c % 4, row4(j, c), 0), lambda b, j, c: (b, c % 4, prev(row4(j, c)), 0)),
        (pl.BlockSpec(blkd, lambda b, j, c: (b, c, j, 0)), q16,
         blkd, lambda b, j, c: (b, c, j, 0), lambda b, j, c: (b, c, prev(j), 0)),
    ]
    in_specs, args = [], []
    for di, (q_spec, q_arr, blk, cur, prv) in enumerate(branches):
        cs, ps = pl.BlockSpec(blk, cur), pl.BlockSpec(blk, prv)
        in_specs += [q_spec, cs, ps, cs, ps]
        kd, vd = (t[di] if di else t[di].reshape(batch, seq, D_B) for t in kv)
        args += [q_arr, kd, kd, vd, vd]
    in_specs += [_const_spec((3, 2, BAND, 2 * BAND)), _const_spec((1, D_B))]
    args += [bias, g_ob]
    out = pl.pallas_call(
        _attn_prompt_body,
        grid=(batch, n_tiles, 16),
        in_specs=in_specs,
        out_specs=pl.BlockSpec((None, TILE, D_B), lambda b, j, c: (b, j, 0)),
        out_shape=jax.ShapeDtypeStruct((batch, seq, D_B), BF16),
        scratch_shapes=[pltpu.VMEM((N_SLABS, TILE, LANES), F32)] * 9,
        compiler_params=_params(3),
        name="attn_prompt",
    )(*args)
    return out.reshape(batch * seq, D_B)


def _band_bias():
    kj = np.arange(2 * BAND)[None, :]
    rho = np.arange(BAND)
    out = []
    for runs in (16, 4, 1):
        per = BAND // runs
        qi = (runs * (rho % per) + rho // per)[:, None]
        ok = (kj >= qi) & (kj <= qi + BAND)
        out.append(np.stack([np.where(ok, 0.0, -np.inf), np.where(ok & (kj >= BAND), 0.0, -np.inf)]))
    return np.stack(out).astype(np.float32)


def _attn_sample_body(kt_ref, vt_ref, kn_ref, vn_ref, q_ref, mult_ref, *rest):
    okt_ref, ovt_ref, o_ref = rest[-3:]
    body = MAX_WINDOW - LANES
    is_new = lax.broadcasted_iota(jnp.int32, (HEAD_DIM, LANES), 1) >= LANES - T_NEW
    mult = mult_ref[...]

    def shift(src_ref, new_ref, dst_ref, h):
        x = src_ref[h]
        y = pltpu.roll(x, MAX_WINDOW - T_NEW, 1)
        tail = jnp.where(is_new, new_ref[h], y[:, body:])
        dst_ref[h, :, 0:body] = y[:, 0:body]
        dst_ref[h, :, body:MAX_WINDOW] = tail
        return jnp.concatenate([y[:, 0:body], tail, x[:, 0:LANES]], axis=1).astype(BF16)

    for h in range(N_HEADS):
        kcat = shift(kt_ref, kn_ref, okt_ref, h)
        vcat = shift(vt_ref, vn_ref, ovt_ref, h)
        s = jnp.where(mult > 0, _dot(q_ref[h], kcat), -jnp.inf)
        p = mult * jnp.exp(s - jnp.max(s, axis=-1, keepdims=True))
        o_ref[h] = _dot_nt(p.astype(BF16), vcat) / jnp.sum(p, axis=-1, keepdims=True)


def _attn_sample(layer, cache_kt, cache_vt, k_new, v_new, q, mult, prev_out):
    depth, db = cache_kt.shape[:2]
    win_blk = (None, None, N_HEADS, HEAD_DIM, MAX_WINDOW)
    cache_spec = pl.BlockSpec(win_blk, lambda b: (layer, b, 0, 0, 0))
    new_spec = pl.BlockSpec((None, N_HEADS, HEAD_DIM, LANES), lambda b: (b, 0, 0, 0))
    q_spec = pl.BlockSpec((None, N_HEADS, 8, HEAD_DIM), lambda b: (b, 0, 0, 0))
    in_specs = [cache_spec, cache_spec, new_spec, new_spec, q_spec,
                _const_spec((8, MAX_WINDOW + LANES))]
    args = [cache_kt, cache_vt, k_new, v_new, q, mult]
    aliases = {}
    if prev_out is not None:
        in_specs += [pl.BlockSpec(memory_space=pl.ANY)] * 2
        aliases = {len(args): 0, len(args) + 1: 1}
        args += list(prev_out)
    win_shape = jax.ShapeDtypeStruct(cache_kt.shape, F32)
    return pl.pallas_call(
        _attn_sample_body,
        grid=(db,),
        in_specs=in_specs,
        out_specs=[cache_spec, cache_spec, q_spec],
        out_shape=[win_shape, win_shape, jax.ShapeDtypeStruct((db, N_HEADS, 8, HEAD_DIM), F32)],
        input_output_aliases=aliases,
        compiler_params=_params(1),
        name="attn_sample",
    )(*args)


def _sample_multiplicity():
    def mult(d):
        ok = d >= 0
        return (ok & (d <= 128)).astype(np.int32) + (ok & (d % 4 == 0) & (d <= 512)) \
            + (ok & (d % 16 == 0) & (d <= 2048))
    t = (np.arange(8) % T_NEW)[:, None]
    kept = mult(MAX_WINDOW - T_NEW + t - np.arange(MAX_WINDOW)[None, :])
    old = np.arange(LANES)[None, :]
    dropped = mult(MAX_WINDOW + t - old) * (old < T_NEW)
    return np.concatenate([kept, dropped], axis=1).astype(np.float32)


def _out_ffn_body(x_ref, ya_ref, yb_ref, gob_ref, wout_ref, gffn_ref, wup_ref, wdown_ref, o_ref,
                  *, norm_b):
    yb = yb_ref[...]
    if norm_b:
        yb = _rms(yb, gob_ref[...])
    x1 = x_ref[...] + _dot(ya_ref[...], wout_ref[0:D_A, :]) \
        + _dot(yb.astype(BF16), wout_ref[D_A:D_A + D_B, :])
    hb = _rms(x1, gffn_ref[...]).astype(BF16)
    step = D_FF // 4
    mlp = None
    for c in range(4):
        f = _dot(hb, wup_ref[:, c * step:(c + 1) * step])
        f = jnp.square(jnp.maximum(f, 0.0)).astype(BF16)
        d = _dot(f, wdown_ref[c * step:(c + 1) * step, :])
        mlp = d if mlp is None else mlp + d
    o_ref[...] = x1 + mlp


def _out_ffn(x, ya, yb, lw, *, rows, norm_b):
    n = x.shape[0]
    row_spec = lambda w: pl.BlockSpec((rows, w), lambda i: (i, 0))
    return pl.pallas_call(
        functools.partial(_out_ffn_body, norm_b=norm_b),
        grid=(n // rows,),
        in_specs=[row_spec(D_MODEL), row_spec(D_A), row_spec(D_B), _const_spec((1, D_B)),
                  _const_spec((D_A + D_B, D_MODEL)), _const_spec((1, D_MODEL)),
                  _const_spec((D_MODEL, D_FF)), _const_spec((D_FF, D_MODEL))],
        out_specs=row_spec(D_MODEL),
        out_shape=jax.ShapeDtypeStruct((n, D_MODEL), F32),
        compiler_params=_params(1),
        name="out_ffn",
    )(x, ya, yb, lw["g_ob"], lw["w_out"], lw["g_ffn"], lw["w_up"], lw["w_down"])


def _rope_tables(base_pos, offsets):
    half = ROT_DIM // 2
    inv = jnp.power(ROPE_THETA, -jnp.arange(half, dtype=F32) * (2.0 / ROT_DIM))
    lane = np.arange(LANES) % HEAD_DIM
    freq = jnp.where(lane < ROT_DIM, inv[lane % half], 0.0)
    ang = lambda pos: pos.astype(F32)[:, None] * freq[None, :]
    ab, ao = ang(base_pos), ang(offsets)
    tbase = jnp.stack([jnp.cos(ab), jnp.sin(ab)], axis=1)
    toff = jnp.stack([jnp.cos(ao), jnp.sin(ao)], axis=0)
    tsign = jnp.asarray(np.stack([-1.0 * (lane < half), 1.0 * ((lane >= half) & (lane < ROT_DIM))]), F32)
    return tbase, toff, tsign


def kernel(x_prompt, x_sample, cache_win_k, cache_win_v, attn_norm_g, w_in, v_norm_g, w_spatial, b_spatial, q_norm_g, k_norm_g, out_norm_a_g, out_norm_b_g, w_out, ffn_norm_g, w_up, w_down):
    depth = w_in.shape[0]
    batch, seq, _ = x_prompt.shape
    db, t_new, _ = x_sample.shape
    n_s = db * t_new
    assert n_s == CHUNK and t_new == T_NEW and cache_win_k.shape[2] == MAX_WINDOW
    assert seq % TILE == 0

    rows_p = 512
    tab_p = _rope_tables(jnp.arange(seq // rows_p) * rows_p, jnp.arange(rows_p))
    tab_s = _rope_tables(jnp.full((1,), PAST_LEN), jnp.arange(n_s) % t_new)

    tril = jnp.tril(jnp.ones((CHUNK, CHUNK), bool))
    tril_s = jnp.tril(jnp.ones((t_new, t_new), bool))
    hsum = jnp.kron(jnp.eye(N_HEADS, dtype=F32), jnp.full((HEAD_DIM, HEAD_DIM), 1.0 / HEAD_DIM, F32)).astype(BF16)

    bias = jnp.asarray(_band_bias())
    mult = jnp.asarray(_sample_multiplicity())

    cache_kt = cache_win_k.transpose(0, 1, 3, 4, 2)
    cache_vt = cache_win_v.transpose(0, 1, 3, 4, 2)

    xp = x_prompt.reshape(batch * seq, D_MODEL)
    xs = x_sample.reshape(n_s, D_MODEL)
    kp_list, vp_list, cv_list = [], [], []
    win_out = None
    row = lambda g: g.reshape(1, -1)
    for l in range(depth):
        lw = dict(g_attn=row(attn_norm_g[l]), w_in=w_in[l].astype(BF16), g_v=row(v_norm_g[l]),
                  g_q=row(jnp.tile(q_norm_g[l], N_HEADS)), g_k=row(jnp.tile(k_norm_g[l], N_HEADS)),
                  hsum=hsum, g_oa=row(out_norm_a_g[l]), g_ob=row(out_norm_b_g[l]),
                  w_out=w_out[l].astype(BF16), g_ffn=row(ffn_norm_g[l]),
                  w_up=w_up[l].astype(BF16), w_down=w_down[l].astype(BF16))
        gate_w_p = jnp.where(tril, w_spatial[l], 0).astype(BF16)
        gate_b_p = jnp.repeat(b_spatial[l].T, HEAD_DIM, axis=1)
        w4 = jnp.where(tril_s, w_spatial[l][:, :t_new, :t_new], 0)
        gate_w_s = (jnp.eye(db, dtype=F32)[None, :, None, :, None] * w4[:, None, :, None, :]) \
            .reshape(N_HEADS, CHUNK, CHUNK).astype(BF16)
        gate_b_s = jnp.tile(jnp.repeat(b_spatial[l][:, :t_new].T, HEAD_DIM, axis=1), (db, 1))

        ya, q16, *kv, k32, v32 = _in_proj(xp, lw, tab_p, gate_w_p, gate_b_p, rows=rows_p,
                                          batch=batch, seq=seq)
        yb = _attn_prompt(q16, (kv[0:3], kv[3:6]), bias, lw["g_ob"], batch, seq)
        xp = _out_ffn(xp, ya, yb, lw, rows=512, norm_b=False)
        kp_list.append(k32.reshape(batch, N_HEADS, HEAD_DIM, MAX_WINDOW))
        vp_list.append(v32.reshape(batch, N_HEADS, HEAD_DIM, MAX_WINDOW))

        ya, q, k32, v32, va = _in_proj(xs, lw, tab_s, gate_w_s, gate_b_s, rows=CHUNK)
        cv_list.append(va.reshape(db, t_new, D_A))
        heads = lambda t: t.reshape(db, t_new, N_HEADS, HEAD_DIM)
        new_t = lambda t: jnp.pad(heads(t).transpose(0, 2, 3, 1), ((0, 0), (0, 0), (0, 0), (LANES - t_new, 0)))
        q8 = jnp.tile(heads(q).transpose(0, 2, 1, 3), (1, 1, 2, 1))
        okt, ovt, o = _attn_sample(l, cache_kt, cache_vt, new_t(k32), new_t(v32), q8, mult, win_out)
        win_out = (okt, ovt)
        yb = o[:, :, :t_new].transpose(0, 2, 1, 3).reshape(n_s, D_B)
        xs = _out_ffn(xs, ya, yb, lw, rows=CHUNK, norm_b=True)

    unt = lambda w: w.transpose(0, 1, 4, 2, 3)
    return (xp.reshape(batch, seq, D_MODEL), xs.reshape(db, t_new, D_MODEL),
            unt(jnp.stack(kp_list)), unt(jnp.stack(vp_list)), unt(win_out[0]), unt(win_out[1]),
            jnp.stack(cv_list))
```

```python
import functools

import numpy as np
import jax
import jax.numpy as jnp
from jax import lax
from jax.experimental import pallas as pl
from jax.experimental.pallas import tpu as pltpu

D_MODEL = 1024
D_A = 512
D_B = 512
HEAD_DIM = 64
N_HEADS = 8
CHUNK = 128
D_FF = 4 * D_MODEL
ROT_DIM = HEAD_DIM // 4
ROPE_THETA = 500000.0
EPS = 1e-6
ATTN_SCALE = HEAD_DIM ** -0.5
DILATIONS = (1, 4, 16)
BAND = 128
TILE = 16 * BAND
SUB = 4
MAX_WINDOW = 2048
PAST_LEN = 16384
T_NEW = 4

LANES = 128
N_SLABS = D_B // LANES
VMEM_LIMIT = 56 * 1024 * 1024

F32 = jnp.float32
BF16 = jnp.bfloat16


def _rms(x, g):
    return x * lax.rsqrt(jnp.mean(x * x, axis=-1, keepdims=True) + EPS) * g


def _gelu(x):
    c = np.float32(np.sqrt(2.0 / np.pi))
    return x * (0.5 * (1.0 + jnp.tanh(c * (x + 0.044715 * (x * x * x)))))


def _dot(a, b):
    return jnp.dot(a, b, preferred_element_type=F32)


def _dot_nt(a, b):
    return lax.dot_general(a, b, (((1,), (1,)), ((), ())), preferred_element_type=F32)


def _params(n_axes):
    return pltpu.CompilerParams(dimension_semantics=("arbitrary",) * n_axes,
                                vmem_limit_bytes=VMEM_LIMIT)


def _const_spec(shape):
    return pl.BlockSpec(shape, lambda *_: (0,) * len(shape), pipeline_mode=pl.Buffered(1))


def _slab(p):
    return slice(p * LANES, (p + 1) * LANES)


def _in_proj_body(x_ref, gattn_ref, win_ref, gv_ref, gq_ref, gk_ref, hsum_ref, tbase_ref, toff_ref,
                  tsign_ref, wg_ref, bg_ref, goa_ref, *outs, rows, tiles_per_seq):
    dilated = tiles_per_seq is not None
    hb = _rms(x_ref[...], gattn_ref[...]).astype(BF16)

    cb, sb = tbase_ref[0:1, :], tbase_ref[1:2, :]
    co, so = toff_ref[0], toff_ref[1]
    cos = cb * co - sb * so
    sin = sb * co + cb * so
    sin_up, sin_dn = sin * tsign_ref[0:1, :], sin * tsign_ref[1:2, :]

    def proj(i):
        return _dot(hb, win_ref[:, i * 512:(i + 1) * 512])

    def head_norm_rope(z, g_ref):
        ms = _dot((z * z).astype(BF16), hsum_ref[...])
        zn = z * lax.rsqrt(ms + EPS) * g_ref[...]
        slabs = []
        for p in range(N_SLABS):
            s = zn[:, _slab(p)]
            slabs.append(s * cos + pltpu.roll(s, LANES - ROT_DIM // 2, 1) * sin_up
                         + pltpu.roll(s, ROT_DIM // 2, 1) * sin_dn)
        return jnp.concatenate(slabs, axis=1)

    u = _gelu(proj(0))
    va = _rms(_gelu(proj(1)), gv_ref[...])
    q = head_norm_rope(proj(2), gq_ref) * ATTN_SCALE
    k = head_norm_rope(proj(3), gk_ref)
    v = proj(4)

    if dilated:
        ya_ref, q16, k1, k4, k16, v1, v4, v16, kwin_ref, vwin_ref, scr1, scr4 = outs

        def split(val, r4_ref, r16_ref):
            for p in range(N_SLABS):
                scr1[p] = val[:, _slab(p)]
            n4, n16 = rows // 4, rows // 16
            for p in range(N_SLABS):
                for r in range(4):
                    part = scr1[p, pl.ds(r, n4, stride=4), :]
                    scr4[p, r * n4:(r + 1) * n4, :] = part
                    if r4_ref is not None:
                        r4_ref[r, :, _slab(p)] = part.astype(BF16)
                for r in range(4):
                    for c in range(4):
                        part = scr4[p, pl.ds(r * n4 + c, n16, stride=4), :]
                        r16_ref[r + 4 * c, :, _slab(p)] = part.astype(r16_ref.dtype)

        split(q, None, q16)
        k1[...] = k.astype(BF16)
        split(k, k4, k16)
        v1[...] = v.astype(BF16)
        split(v, v4, v16)

        @pl.when(pl.program_id(0) % tiles_per_seq >= tiles_per_seq - MAX_WINDOW // rows)
        def _():
            kwin_ref[...] = k.T
            vwin_ref[...] = v.T
    else:
        ya_ref, q1, k32_ref, v32_ref, va_ref = outs
        q1[...] = q.astype(BF16)
        va_ref[...] = va
        k32_ref[...] = k
        v32_ref[...] = v

    lane_lo = lax.broadcasted_iota(jnp.int32, (CHUNK, LANES), 1) < HEAD_DIM
    for c in range(rows // CHUNK):
        rs = slice(c * CHUNK, (c + 1) * CHUNK)
        ys = []
        for p in range(N_SLABS):
            rhs = va[rs, _slab(p)]
            lo = jnp.where(lane_lo, rhs, 0.0).astype(BF16)
            hi = jnp.where(lane_lo, 0.0, rhs).astype(BF16)
            s = _dot(wg_ref[2 * p], lo) + _dot(wg_ref[2 * p + 1], hi) + bg_ref[:, _slab(p)]
            ys.append(u[rs, _slab(p)] * s)
        ya_ref[rs, :] = _rms(jnp.concatenate(ys, axis=1), goa_ref[...]).astype(BF16)


def _in_proj(x, lw, tables, gate_w, gate_b, *, rows, batch=None, seq=None):
    n = x.shape[0]
    dilated = batch is not None
    tbase, toff, tsign = tables
    n_base = tbase.shape[0]
    row_spec = lambda w: pl.BlockSpec((rows, w), lambda i: (i, 0))
    flat = lambda dt: jax.ShapeDtypeStruct((n, 512), dt)
    scratch = []
    per_b = None
    if dilated:
        per_b = seq // rows
        split_shape = lambda d, dt: jax.ShapeDtypeStruct((batch, d, seq // d, D_B), dt)
        split_spec = lambda d: pl.BlockSpec((None, d, rows // d, D_B),
                                            lambda i: (i // per_b, 0, i % per_b, 0))
        kv_shape = [flat(BF16), split_shape(4, BF16), split_shape(16, BF16)]
        kv_spec = [row_spec(512), split_spec(4), split_spec(16)]
        first_win = per_b - MAX_WINDOW // rows
        win_shape = jax.ShapeDtypeStruct((batch, D_B, MAX_WINDOW), F32)
        win_spec = pl.BlockSpec((None, D_B, rows),
                                lambda i: (i // per_b, 0, jnp.maximum(i % per_b - first_win, 0)))
        out_shape = [flat(BF16), split_shape(16, F32)] + kv_shape * 2 + [win_shape] * 2
        out_specs = [row_spec(512), split_spec(16)] + kv_spec * 2 + [win_spec] * 2
        scratch = [pltpu.VMEM((N_SLABS, rows, LANES), F32)] * 2
    else:
        out_shape = [flat(BF16), flat(BF16), flat(F32), flat(F32), flat(F32)]
        out_specs = [row_spec(512)] * 5
    return pl.pallas_call(
        functools.partial(_in_proj_body, rows=rows, tiles_per_seq=per_b),
        grid=(n // rows,),
        in_specs=[row_spec(D_MODEL), _const_spec((1, D_MODEL)), _const_spec((D_MODEL, 5 * 512)),
                  _const_spec((1, D_A)), _const_spec((1, D_B)), _const_spec((1, D_B)),
                  _const_spec((D_B, D_B)),
                  pl.BlockSpec((None, 2, LANES), lambda i: (i % n_base, 0, 0)),
                  _const_spec((2, rows, LANES)), _const_spec((2, LANES)),
                  _const_spec((N_HEADS, CHUNK, CHUNK)), _const_spec((CHUNK, D_A)),
                  _const_spec((1, D_A))],
        out_specs=out_specs,
        out_shape=out_shape,
        scratch_shapes=scratch,
        compiler_params=_params(1),
        name="in_proj",
    )(x, lw["g_attn"], lw["w_in"], lw["g_v"], lw["g_q"], lw["g_k"], lw["hsum"], tbase, toff, tsign,
      gate_w, gate_b, lw["g_oa"])


def _band_block(q, kp, kc, vp, vc, bias):
    lane_lo = lax.broadcasted_iota(jnp.int32, (BAND, LANES), 1) < HEAD_DIM
    head_mask = [(lax.broadcasted_iota(jnp.int32, (1, LANES), 1) // HEAD_DIM == hh)
                 .astype(F32).astype(BF16) for hh in range(2)]
    ones = jnp.ones((2 * BAND, LANES), BF16)
    pick = lambda t: jnp.where(lane_lo, t[:BAND], t[BAND:])
    out = []
    for p in range(N_SLABS):
        qp = q[:, _slab(p)]
        kk = jnp.concatenate([kp[:, _slab(p)], kc[:, _slab(p)]], axis=0)
        vv = jnp.concatenate([vp[:, _slab(p)], vc[:, _slab(p)]], axis=0)
        qq = jnp.concatenate([qp * head_mask[0], qp * head_mask[1]], axis=0)
        s = _dot_nt(qq, kk) + bias
        m = jnp.max(s, axis=-1, keepdims=True)
        pr = jnp.exp(s - m).astype(BF16)
        a = _dot(pr, jnp.concatenate([vv, ones], axis=1))
        l = pick(a[:, LANES:])
        out.append((pick(a[:, :LANES]) * (1.0 / l), pick(m) + jnp.log(l)))
    return out


def _attn_prompt_body(q1, k1c, k1p, v1c, v1p, q4, k4c, k4p, v4c, v4p, q16, k16c, k16p, v16c, v16p,
                      bias_ref, gob_ref, out_ref, o1, s1, o4, s4, o16, s16):
    j = pl.program_id(1)
    g = pl.program_id(2)
    n_groups = pl.num_programs(2)

    def run(branch, q, kc, kp, vc, vp, first, refs, runs):
        bias = bias_ref[branch, first] if isinstance(first, int) else \
            bias_ref[branch, first.astype(jnp.int32)]
        res = _band_block(q.reshape(BAND, D_B).astype(BF16), kp, kc, vp, vc, bias)
        n = BAND // len(runs)
        for p in range(N_SLABS):
            for ref, val in zip(refs, res[p]):
                for i, row0 in enumerate(runs):
                    ref[p, pl.ds(pl.multiple_of(row0, n), n), :] = val[i * n:(i + 1) * n]

    for i in range(SUB):
        c = SUB * g + i
        blk = slice(i * BAND, (i + 1) * BAND)
        if i == 0:
            kp, vp, first = k1p[...], v1p[...], (n_groups * j + g) == 0
        else:
            before = slice((i - 1) * BAND, i * BAND)
            kp, vp, first = k1c[before, :], v1c[before, :], 0
        run(0, q1[:, 8 * i:8 * (i + 1), :], k1c[blk, :], kp, v1c[blk, :], vp, first, (o1, s1),
            [BAND * r + 8 * c for r in range(16)])
        run(1, q4[:, i], k4c[i], k4p[i], v4c[i], v4p[i], (n_groups * j + g) == 0, (o4, s4),
            [BAND * (4 * cc + i) + 32 * g for cc in range(4)])
        run(2, q16[i], k16c[i], k16p[i], v16c[i], v16p[i], j == 0, (o16, s16), [BAND * c])

    @pl.when(g == n_groups - 1)
    def _finalize():
        parts = [(o1, s1), (o4, s4), (o16, s16)]
        for r in range(16):
            rows = pl.ds(r * BAND, BAND)
            slabs = []
            for p in range(N_SLABS):
                lse = [s_ref[p, rows, :] for (_, s_ref) in parts]
                top = jnp.maximum(jnp.maximum(lse[0], lse[1]), lse[2])
                num = den = None
                for e, (o_ref, _) in zip(lse, parts):
                    w = jnp.exp(e - top)
                    t = o_ref[p, rows, :] * w
                    num, den = (t, w) if num is None else (num + t, den + w)
                slabs.append(num / den)
            y = _rms(jnp.concatenate(slabs, axis=1), gob_ref[...])
            for p in range(N_SLABS):
                o16[p, rows, :] = y[:, _slab(p)]
        for p in range(N_SLABS):
            for r in range(16):
                o1[p, pl.ds(r, BAND, stride=16), :] = o16[p, r * BAND:(r + 1) * BAND, :]
            out_ref[:, _slab(p)] = o1[p].astype(BF16)


def _attn_prompt(q16, kv, bias, g_ob, batch, seq):
    assert SUB == 4
    n_tiles = seq // TILE
    n_groups = 16 // SUB
    span = SUB * BAND
    prev = lambda i: jnp.maximum(i - 1, 0)
    grp = lambda j, g: n_groups * j + g
    branches = [
        (pl.BlockSpec((None, 16, span // 16, D_B), lambda b, j, g: (b, 0, grp(j, g), 0)), q16,
         ((None, span, D_B), lambda b, j, g: (b, grp(j, g), 0)),
         ((None, BAND, D_B), lambda b, j, g: (b, prev(SUB * grp(j, g)), 0))),
        (pl.BlockSpec((None, 4, SUB, span // 16, D_B), lambda b, j, g: (b, 0, 0, grp(j, g), 0)),
         q16.reshape(batch, 4, 4, seq // 16, D_B),
         ((None, SUB, BAND, D_B), lambda b, j, g: (b, 0, grp(j, g), 0)),
         ((None, SUB, BAND, D_B), lambda b, j, g: (b, 0, prev(grp(j, g)), 0))),
        (pl.BlockSpec((None, SUB, BAND, D_B), lambda b, j, g: (b, g, j, 0)), q16,
         ((None, SUB, BAND, D_B), lambda b, j, g: (b, g, j, 0)),
         ((None, SUB, BAND, D_B), lambda b, j, g: (b, g, prev(j), 0))),
    ]
    in_specs, args = [], []
    for di, (q_spec, q_arr, cur, prv) in enumerate(branches):
        cs, ps = pl.BlockSpec(*cur), pl.BlockSpec(*prv)
        in_specs += [q_spec, cs, ps, cs, ps]
        kd, vd = (t[di] if di else t[di].reshape(batch, seq, D_B) for t in kv)
        args += [q_arr, kd, kd, vd, vd]
    in_specs += [_const_spec((3, 2, 2 * BAND, 2 * BAND)), _const_spec((1, D_B))]
    args += [bias, g_ob]
    out = pl.pallas_call(
        _attn_prompt_body,
        grid=(batch, n_tiles, n_groups),
        in_specs=in_specs,
        out_specs=pl.BlockSpec((None, TILE, D_B), lambda b, j, g: (b, j, 0)),
        out_shape=jax.ShapeDtypeStruct((batch, seq, D_B), BF16),
        scratch_shapes=[pltpu.VMEM((N_SLABS, TILE, LANES), F32)] * 6,
        compiler_params=_params(3),
        name="attn_prompt",
    )(*args)
    return out.reshape(batch * seq, D_B)


def _band_bias():
    kj = np.arange(2 * BAND)[None, :]
    rho = np.arange(BAND)
    out = []
    for runs in (16, 4, 1):
        per = BAND // runs
        qi = (runs * (rho % per) + rho // per)[:, None]
        ok = (kj >= qi) & (kj <= qi + BAND)
        out.append(np.stack([np.where(ok, 0.0, -np.inf), np.where(ok & (kj >= BAND), 0.0, -np.inf)]))
    return np.tile(np.stack(out), (1, 1, 2, 1)).astype(np.float32)


def _attn_sample_body(kt_ref, vt_ref, kn_ref, vn_ref, q_ref, mult_ref, *rest):
    okt_ref, ovt_ref, o_ref = rest[-3:]
    body = MAX_WINDOW - LANES
    is_new = lax.broadcasted_iota(jnp.int32, (HEAD_DIM, LANES), 1) >= LANES - T_NEW
    mult = mult_ref[...]

    def shift(src_ref, new_ref, dst_ref, h):
        x = src_ref[h]
        y = pltpu.roll(x, MAX_WINDOW - T_NEW, 1)
        tail = jnp.where(is_new, new_ref[h], y[:, body:])
        dst_ref[h, :, 0:body] = y[:, 0:body]
        dst_ref[h, :, body:MAX_WINDOW] = tail
        return jnp.concatenate([y[:, 0:body], tail, x[:, 0:LANES]], axis=1).astype(BF16)

    for h in range(N_HEADS):
        kcat = shift(kt_ref, kn_ref, okt_ref, h)
        vcat = shift(vt_ref, vn_ref, ovt_ref, h)
        s = jnp.where(mult > 0, _dot(q_ref[h], kcat), -jnp.inf)
        p = mult * jnp.exp(s - jnp.max(s, axis=-1, keepdims=True))
        o_ref[h] = _dot_nt(p.astype(BF16), vcat) / jnp.sum(p, axis=-1, keepdims=True)


def _attn_sample(layer, cache_kt, cache_vt, k_new, v_new, q, mult, prev_out):
    depth, db = cache_kt.shape[:2]
    win_blk = (None, None, N_HEADS, HEAD_DIM, MAX_WINDOW)
    cache_spec = pl.BlockSpec(win_blk, lambda b: (layer, b, 0, 0, 0))
    new_spec = pl.BlockSpec((None, N_HEADS, HEAD_DIM, LANES), lambda b: (b, 0, 0, 0))
    q_spec = pl.BlockSpec((None, N_HEADS, 8, HEAD_DIM), lambda b: (b, 0, 0, 0))
    in_specs = [cache_spec, cache_spec, new_spec, new_spec, q_spec,
                _const_spec((8, MAX_WINDOW + LANES))]
    args = [cache_kt, cache_vt, k_new, v_new, q, mult]
    aliases = {}
    if prev_out is not None:
        in_specs += [pl.BlockSpec(memory_space=pl.ANY)] * 2
        aliases = {len(args): 0, len(args) + 1: 1}
        args += list(prev_out)
    win_shape = jax.ShapeDtypeStruct(cache_kt.shape, F32)
    return pl.pallas_call(
        _attn_sample_body,
        grid=(db,),
        in_specs=in_specs,
        out_specs=[cache_spec, cache_spec, q_spec],
        out_shape=[win_shape, win_shape, jax.ShapeDtypeStruct((db, N_HEADS, 8, HEAD_DIM), F32)],
        input_output_aliases=aliases,
        compiler_params=_params(1),
        name="attn_sample",
    )(*args)


def _sample_multiplicity():
    def mult(d):
        ok = d >= 0
        return (ok & (d <= 128)).astype(np.int32) + (ok & (d % 4 == 0) & (d <= 512)) \
            + (ok & (d % 16 == 0) & (d <= 2048))
    t = (np.arange(8) % T_NEW)[:, None]
    kept = mult(MAX_WINDOW - T_NEW + t - np.arange(MAX_WINDOW)[None, :])
    old = np.arange(LANES)[None, :]
    dropped = mult(MAX_WINDOW + t - old) * (old < T_NEW)
    return np.concatenate([kept, dropped], axis=1).astype(np.float32)


def _out_ffn_body(x_ref, ya_ref, yb_ref, gob_ref, wout_ref, gffn_ref, wup_ref, wdown_ref, o_ref,
                  *, norm_b):
    yb = yb_ref[...]
    if norm_b:
        yb = _rms(yb, gob_ref[...])
    x1 = x_ref[...] + _dot(ya_ref[...], wout_ref[0:D_A, :]) \
        + _dot(yb.astype(BF16), wout_ref[D_A:D_A + D_B, :])
    hb = _rms(x1, gffn_ref[...]).astype(BF16)
    step = D_FF // 4
    mlp = None
    for c in range(4):
        f = _dot(hb, wup_ref[:, c * step:(c + 1) * step])
        f = jnp.square(jnp.maximum(f, 0.0)).astype(BF16)
        d = _dot(f, wdown_ref[c * step:(c + 1) * step, :])
        mlp = d if mlp is None else mlp + d
    o_ref[...] = x1 + mlp


def _out_ffn(x, ya, yb, lw, *, rows, norm_b):
    n = x.shape[0]
    row_spec = lambda w: pl.BlockSpec((rows, w), lambda i: (i, 0))
    return pl.pallas_call(
        functools.partial(_out_ffn_body, norm_b=norm_b),
        grid=(n // rows,),
        in_specs=[row_spec(D_MODEL), row_spec(D_A), row_spec(D_B), _const_spec((1, D_B)),
                  _const_spec((D_A + D_B, D_MODEL)), _const_spec((1, D_MODEL)),
                  _const_spec((D_MODEL, D_FF)), _const_spec((D_FF, D_MODEL))],
        out_specs=row_spec(D_MODEL),
        out_shape=jax.ShapeDtypeStruct((n, D_MODEL), F32),
        compiler_params=_params(1),
        name="out_ffn",
    )(x, ya, yb, lw["g_ob"], lw["w_out"], lw["g_ffn"], lw["w_up"], lw["w_down"])


def _rope_tables(base_pos, offsets):
    half = ROT_DIM // 2
    inv = jnp.power(ROPE_THETA, -jnp.arange(half, dtype=F32) * (2.0 / ROT_DIM))
    lane = np.arange(LANES) % HEAD_DIM
    freq = jnp.where(lane < ROT_DIM, inv[lane % half], 0.0)
    ang = lambda pos: pos.astype(F32)[:, None] * freq[None, :]
    ab, ao = ang(base_pos), ang(offsets)
    tbase = jnp.stack([jnp.cos(ab), jnp.sin(ab)], axis=1)
    toff = jnp.stack([jnp.cos(ao), jnp.sin(ao)], axis=0)
    tsign = jnp.asarray(np.stack([-1.0 * (lane < half), 1.0 * ((lane >= half) & (lane < ROT_DIM))]), F32)
    return tbase, toff, tsign


def kernel(x_prompt, x_sample, cache_win_k, cache_win_v, attn_norm_g, w_in, v_norm_g, w_spatial, b_spatial, q_norm_g, k_norm_g, out_norm_a_g, out_norm_b_g, w_out, ffn_norm_g, w_up, w_down):
    depth = w_in.shape[0]
    batch, seq, _ = x_prompt.shape
    db, t_new, _ = x_sample.shape
    n_s = db * t_new
    assert n_s == CHUNK and t_new == T_NEW and cache_win_k.shape[2] == MAX_WINDOW
    assert seq % TILE == 0

    rows_p = 512
    tab_p = _rope_tables(jnp.arange(seq // rows_p) * rows_p, jnp.arange(rows_p))
    tab_s = _rope_tables(jnp.full((1,), PAST_LEN), jnp.arange(n_s) % t_new)

    tril = jnp.tril(jnp.ones((CHUNK, CHUNK), bool))
    tril_s = jnp.tril(jnp.ones((t_new, t_new), bool))
    hsum = jnp.kron(jnp.eye(N_HEADS, dtype=F32), jnp.full((HEAD_DIM, HEAD_DIM), 1.0 / HEAD_DIM, F32)).astype(BF16)

    bias = jnp.asarray(_band_bias())
    mult = jnp.asarray(_sample_multiplicity())

    cache_kt = cache_win_k.transpose(0, 1, 3, 4, 2)
    cache_vt = cache_win_v.transpose(0, 1, 3, 4, 2)

    xp = x_prompt.reshape(batch * seq, D_MODEL)
    xs = x_sample.reshape(n_s, D_MODEL)
    kp_list, vp_list, cv_list = [], [], []
    win_out = None
    row = lambda g: g.reshape(1, -1)
    for l in range(depth):
        lw = dict(g_attn=row(attn_norm_g[l]), w_in=w_in[l].astype(BF16), g_v=row(v_norm_g[l]),
                  g_q=row(jnp.tile(q_norm_g[l], N_HEADS)), g_k=row(jnp.tile(k_norm_g[l], N_HEADS)),
                  hsum=hsum, g_oa=row(out_norm_a_g[l]), g_ob=row(out_norm_b_g[l]),
                  w_out=w_out[l].astype(BF16), g_ffn=row(ffn_norm_g[l]),
                  w_up=w_up[l].astype(BF16), w_down=w_down[l].astype(BF16))
        gate_w_p = jnp.where(tril, w_spatial[l], 0).astype(BF16)
        gate_b_p = jnp.repeat(b_spatial[l].T, HEAD_DIM, axis=1)
        w4 = jnp.where(tril_s, w_spatial[l][:, :t_new, :t_new], 0)
        gate_w_s = (jnp.eye(db, dtype=F32)[None, :, None, :, None] * w4[:, None, :, None, :]) \
            .reshape(N_HEADS, CHUNK, CHUNK).astype(BF16)
        gate_b_s = jnp.tile(jnp.repeat(b_spatial[l][:, :t_new].T, HEAD_DIM, axis=1), (db, 1))

        ya, q16, *kv, k32, v32 = _in_proj(xp, lw, tab_p, gate_w_p, gate_b_p, rows=rows_p,
                                          batch=batch, seq=seq)
        yb = _attn_prompt(q16, (kv[0:3], kv[3:6]), bias, lw["g_ob"], batch, seq)
        xp = _out_ffn(xp, ya, yb, lw, rows=512, norm_b=False)
        kp_list.append(k32.reshape(batch, N_HEADS, HEAD_DIM, MAX_WINDOW))
        vp_list.append(v32.reshape(batch, N_HEADS, HEAD_DIM, MAX_WINDOW))

        ya, q, k32, v32, va = _in_proj(xs, lw, tab_s, gate_w_s, gate_b_s, rows=CHUNK)
        cv_list.append(va.reshape(db, t_new, D_A))
        heads = lambda t: t.reshape(db, t_new, N_HEADS, HEAD_DIM)
        new_t = lambda t: jnp.pad(heads(t).transpose(0, 2, 3, 1), ((0, 0), (0, 0), (0, 0), (LANES - t_new, 0)))
        q8 = jnp.tile(heads(q).transpose(0, 2, 1, 3), (1, 1, 2, 1))
        okt, ovt, o = _attn_sample(l, cache_kt, cache_vt, new_t(k32), new_t(v32), q8, mult, win_out)
        win_out = (okt, ovt)
        yb = o[:, :, :t_new].transpose(0, 2, 1, 3).reshape(n_s, D_B)
        xs = _out_ffn(xs, ya, yb, lw, rows=CHUNK, norm_b=True)

    unt = lambda w: w.transpose(0, 1, 4, 2, 3)
    return (xp.reshape(batch, seq, D_MODEL), xs.reshape(db, t_new, D_MODEL),
            unt(jnp.stack(kp_list)), unt(jnp.stack(vp_list)), unt(win_out[0]), unt(win_out[1]),
            jnp.stack(cv_list))
```

```python
import functools

import numpy as np
import jax
import jax.numpy as jnp
from jax import lax
from jax.experimental import pallas as pl
from jax.experimental.pallas import tpu as pltpu

D_MODEL = 1024
D_A = 512
D_B = 512
HEAD_DIM = 64
N_HEADS = 8
CHUNK = 128
D_FF = 4 * D_MODEL
ROT_DIM = HEAD_DIM // 4
ROPE_THETA = 500000.0
EPS = 1e-6
ATTN_SCALE = HEAD_DIM ** -0.5
DILATIONS = (1, 4, 16)
BAND = 128
TILE = 16 * BAND
SUB = 4
MAX_WINDOW = 2048
PAST_LEN = 16384
T_NEW = 4

LANES = 128
N_SLABS = D_B // LANES
VMEM_LIMIT = 56 * 1024 * 1024

F32 = jnp.float32
BF16 = jnp.bfloat16


def _rms(x, g):
    return x * lax.rsqrt(jnp.mean(x * x, axis=-1, keepdims=True) + EPS) * g


def _gelu(x):
    c = np.float32(np.sqrt(2.0 / np.pi))
    return x * (0.5 * (1.0 + jnp.tanh(c * (x + 0.044715 * (x * x * x)))))


def _dot(a, b):
    return jnp.dot(a, b, preferred_element_type=F32)


def _dot_nt(a, b):
    return lax.dot_general(a, b, (((1,), (1,)), ((), ())), preferred_element_type=F32)


def _params(n_axes):
    return pltpu.CompilerParams(dimension_semantics=("arbitrary",) * n_axes,
                                vmem_limit_bytes=VMEM_LIMIT)


def _const_spec(shape):
    return pl.BlockSpec(shape, lambda *_: (0,) * len(shape), pipeline_mode=pl.Buffered(1))


def _slab(p):
    return slice(p * LANES, (p + 1) * LANES)


def _in_proj_body(x_ref, gattn_ref, win_ref, gv_ref, gq_ref, gk_ref, hsum_ref, tbase_ref, toff_ref,
                  tsign_ref, wg_ref, bg_ref, goa_ref, *outs, rows, tiles_per_seq):
    dilated = tiles_per_seq is not None
    hb = _rms(x_ref[...], gattn_ref[...]).astype(BF16)

    cb, sb = tbase_ref[0:1, :], tbase_ref[1:2, :]
    co, so = toff_ref[0], toff_ref[1]
    cos = cb * co - sb * so
    sin = sb * co + cb * so
    sin_up, sin_dn = sin * tsign_ref[0:1, :], sin * tsign_ref[1:2, :]

    def proj(i):
        return _dot(hb, win_ref[:, i * 512:(i + 1) * 512])

    def head_norm_rope(z, g_ref):
        zz = (z * z).astype(BF16)
        half = D_B // 2
        ms = jnp.concatenate([_dot(zz[:, :half], hsum_ref[...]), _dot(zz[:, half:], hsum_ref[...])],
                             axis=1)
        zn = z * lax.rsqrt(ms + EPS) * g_ref[...]
        slabs = []
        for p in range(N_SLABS):
            s = zn[:, _slab(p)]
            slabs.append(s * cos + pltpu.roll(s, LANES - ROT_DIM // 2, 1) * sin_up
                         + pltpu.roll(s, ROT_DIM // 2, 1) * sin_dn)
        return jnp.concatenate(slabs, axis=1)

    if dilated:
        ya_ref, q16, k1, k4, k16, v1, v4, v16, kwin_ref, vwin_ref, scr1, scr4 = outs

        def split(val, r4_ref, r16_ref):
            for p in range(N_SLABS):
                scr1[p] = val[:, _slab(p)]
            n4, n16 = rows // 4, rows // 16
            for p in range(N_SLABS):
                for r in range(4):
                    part = scr1[p, pl.ds(r, n4, stride=4), :]
                    scr4[p, r * n4:(r + 1) * n4, :] = part
                    if r4_ref is not None:
                        r4_ref[r, :, _slab(p)] = part.astype(BF16)
                for r in range(4):
                    for c in range(4):
                        part = scr4[p, pl.ds(r * n4 + c, n16, stride=4), :]
                        r16_ref[r + 4 * c, :, _slab(p)] = part.astype(r16_ref.dtype)
    else:
        ya_ref, q1, kt_ref, vt_ref, va_ref = outs

    zu = proj(0)
    zq = proj(2)
    u = _gelu(zu)
    q = head_norm_rope(zq, gq_ref) * ATTN_SCALE
    zva = proj(1)
    if dilated:
        split(q, None, q16)
    else:
        q1[...] = q.astype(BF16)
    zk = proj(3)
    va = _rms(_gelu(zva), gv_ref[...])
    k = head_norm_rope(zk, gk_ref)
    v = proj(4)
    if dilated:
        k1[...] = k.astype(BF16)
        split(k, k4, k16)
        v1[...] = v.astype(BF16)
        split(v, v4, v16)
    else:
        va_ref[...] = va
        kt_ref[...] = k.T
        vt_ref[...] = v.T

    lane_lo = lax.broadcasted_iota(jnp.int32, (CHUNK, LANES), 1) < HEAD_DIM
    n_chunks = rows // CHUNK
    side = min(2, n_chunks)
    ys = [[None] * N_SLABS for _ in range(n_chunks)]
    for c0 in range(0, n_chunks, side):
        for p in range(N_SLABS):
            parts = [va[c * CHUNK:(c + 1) * CHUNK, _slab(p)] for c in range(c0, c0 + side)]
            lo = jnp.concatenate([jnp.where(lane_lo, t, 0.0) for t in parts], axis=1)
            hi = jnp.concatenate([jnp.where(lane_lo, 0.0, t) for t in parts], axis=1)
            s = _dot(wg_ref[p], jnp.concatenate([lo, hi], axis=0).astype(BF16))
            for i in range(side):
                c = c0 + i
                ys[c][p] = u[c * CHUNK:(c + 1) * CHUNK, _slab(p)] * (s[:, _slab(i)] + bg_ref[:, _slab(p)])
    for c in range(n_chunks):
        ya_ref[c * CHUNK:(c + 1) * CHUNK, :] = _rms(jnp.concatenate(ys[c], axis=1), goa_ref[...]).astype(BF16)

    if dilated:
        @pl.when(pl.program_id(0) % tiles_per_seq >= tiles_per_seq - MAX_WINDOW // rows)
        def _():
            kwin_ref[...] = k.T
            vwin_ref[...] = v.T


def _in_proj(x, lw, tables, gate_w, gate_b, *, rows, batch=None, seq=None):
    n = x.shape[0]
    dilated = batch is not None
    tbase, toff, tsign = tables
    n_base = tbase.shape[0]
    row_spec = lambda w: pl.BlockSpec((rows, w), lambda i: (i, 0))
    flat = lambda dt: jax.ShapeDtypeStruct((n, 512), dt)
    scratch = []
    per_b = None
    if dilated:
        per_b = seq // rows
        split_shape = lambda d, dt: jax.ShapeDtypeStruct((batch, d, seq // d, D_B), dt)
        split_spec = lambda d: pl.BlockSpec((None, d, rows // d, D_B),
                                            lambda i: (i // per_b, 0, i % per_b, 0))
        kv_shape = [flat(BF16), split_shape(4, BF16), split_shape(16, BF16)]
        kv_spec = [row_spec(512), split_spec(4), split_spec(16)]
        first_win = per_b - MAX_WINDOW // rows
        win_shape = jax.ShapeDtypeStruct((batch, D_B, MAX_WINDOW), F32)
        win_spec = pl.BlockSpec((None, D_B, rows),
                                lambda i: (i // per_b, 0, jnp.maximum(i % per_b - first_win, 0)))
        out_shape = [flat(BF16), split_shape(16, F32)] + kv_shape * 2 + [win_shape] * 2
        out_specs = [row_spec(512), split_spec(16)] + kv_spec * 2 + [win_spec] * 2
        scratch = [pltpu.VMEM((N_SLABS, rows, LANES), F32)] * 2
    else:
        t_shape = jax.ShapeDtypeStruct((512, n), F32)
        t_spec = pl.BlockSpec((512, rows), lambda i: (0, i))
        out_shape = [flat(BF16), flat(BF16), t_shape, t_shape, flat(F32)]
        out_specs = [row_spec(512), row_spec(512), t_spec, t_spec, row_spec(512)]
    return pl.pallas_call(
        functools.partial(_in_proj_body, rows=rows, tiles_per_seq=per_b),
        grid=(n // rows,),
        in_specs=[row_spec(D_MODEL), _const_spec((1, D_MODEL)), _const_spec((D_MODEL, 5 * 512)),
                  _const_spec((1, D_A)), _const_spec((1, D_B)), _const_spec((1, D_B)),
                  _const_spec((D_B // 2, D_B // 2)),
                  pl.BlockSpec((None, 2, LANES), lambda i: (i % n_base, 0, 0)),
                  _const_spec((2, rows, LANES)), _const_spec((2, LANES)),
                  _const_spec((N_SLABS, CHUNK, 2 * CHUNK)), _const_spec((CHUNK, D_A)),
                  _const_spec((1, D_A))],
        out_specs=out_specs,
        out_shape=out_shape,
        scratch_shapes=scratch,
        compiler_params=_params(1),
        name="in_proj",
    )(x, lw["g_attn"], lw["w_in"], lw["g_v"], lw["g_q"], lw["g_k"], lw["hsum"], tbase, toff, tsign,
      gate_w, gate_b, lw["g_oa"])


def _band_block(q, kp, kc, vp, vc, bias):
    lane_lo = lax.broadcasted_iota(jnp.int32, (BAND, LANES), 1) < HEAD_DIM
    head_mask = [(lax.broadcasted_iota(jnp.int32, (1, LANES), 1) // HEAD_DIM == hh)
                 .astype(F32).astype(BF16) for hh in range(2)]
    ones = jnp.ones((2 * BAND, LANES), BF16)
    pick = lambda t: jnp.where(lane_lo, t[:BAND], t[BAND:])
    out = []
    for p in range(N_SLABS):
        qp = q[:, _slab(p)]
        kk = jnp.concatenate([kp[:, _slab(p)], kc[:, _slab(p)]], axis=0)
        vv = jnp.concatenate([vp[:, _slab(p)], vc[:, _slab(p)]], axis=0)
        qq = jnp.concatenate([qp * head_mask[0], qp * head_mask[1]], axis=0)
        s = _dot_nt(qq, kk) + bias
        m = jnp.max(s, axis=-1, keepdims=True)
        pr = jnp.exp(s - m).astype(BF16)
        a = _dot(pr, jnp.concatenate([vv, ones], axis=1))
        l = pick(a[:, LANES:])
        out.append((pick(a[:, :LANES]) * (1.0 / l), pick(m) + jnp.log(l)))
    return out


def _attn_prompt_body(q1, k1c, k1p, v1c, v1p, q4, k4c, k4p, v4c, v4p, q16, k16c, k16p, v16c, v16p,
                      bias_ref, gob_ref, out_ref, o1, s1, o4, s4, o16, s16):
    j = pl.program_id(1)
    g = pl.program_id(2)
    n_groups = pl.num_programs(2)

    def run(branch, q, kc, kp, vc, vp, first, refs, runs):
        bias = bias_ref[branch, first] if isinstance(first, int) else \
            bias_ref[branch, first.astype(jnp.int32)]
        res = _band_block(q.reshape(BAND, D_B).astype(BF16), kp, kc, vp, vc, bias)
        n = BAND // len(runs)
        for p in range(N_SLABS):
            for ref, val in zip(refs, res[p]):
                for i, row0 in enumerate(runs):
                    ref[p, pl.ds(pl.multiple_of(row0, n), n), :] = val[i * n:(i + 1) * n]

    for i in range(SUB):
        c = SUB * g + i
        blk = slice(i * BAND, (i + 1) * BAND)
        if i == 0:
            kp, vp, first = k1p[...], v1p[...], (n_groups * j + g) == 0
        else:
            before = slice((i - 1) * BAND, i * BAND)
            kp, vp, first = k1c[before, :], v1c[before, :], 0
        run(0, q1[:, 8 * i:8 * (i + 1), :], k1c[blk, :], kp, v1c[blk, :], vp, first, (o1, s1),
            [BAND * r + 8 * c for r in range(16)])
        run(1, q4[:, i], k4c[i], k4p[i], v4c[i], v4p[i], (n_groups * j + g) == 0, (o4, s4),
            [BAND * (4 * cc + i) + 32 * g for cc in range(4)])
        run(2, q16[i], k16c[i], k16p[i], v16c[i], v16p[i], j == 0, (o16, s16), [BAND * c])

    @pl.when(g == n_groups - 1)
    def _finalize():
        parts = [(o1, s1), (o4, s4), (o16, s16)]
        for r in range(16):
            rows = pl.ds(r * BAND, BAND)
            slabs = []
            for p in range(N_SLABS):
                lse = [s_ref[p, rows, :] for (_, s_ref) in parts]
                top = jnp.maximum(jnp.maximum(lse[0], lse[1]), lse[2])
                num = den = None
                for e, (o_ref, _) in zip(lse, parts):
                    w = jnp.exp(e - top)
                    t = o_ref[p, rows, :] * w
                    num, den = (t, w) if num is None else (num + t, den + w)
                slabs.append(num / den)
            y = _rms(jnp.concatenate(slabs, axis=1), gob_ref[...])
            for p in range(N_SLABS):
                o16[p, rows, :] = y[:, _slab(p)]
        for p in range(N_SLABS):
            for r in range(16):
                o1[p, pl.ds(r, BAND, stride=16), :] = o16[p, r * BAND:(r + 1) * BAND, :]
            out_ref[:, _slab(p)] = o1[p].astype(BF16)


def _attn_prompt(q16, kv, bias, g_ob, batch, seq):
    assert SUB == 4
    n_tiles = seq // TILE
    n_groups = 16 // SUB
    span = SUB * BAND
    prev = lambda i: jnp.maximum(i - 1, 0)
    grp = lambda j, g: n_groups * j + g
    branches = [
        (pl.BlockSpec((None, 16, span // 16, D_B), lambda b, j, g: (b, 0, grp(j, g), 0)), q16,
         ((None, span, D_B), lambda b, j, g: (b, grp(j, g), 0)),
         ((None, BAND, D_B), lambda b, j, g: (b, prev(SUB * grp(j, g)), 0))),
        (pl.BlockSpec((None, 4, SUB, span // 16, D_B), lambda b, j, g: (b, 0, 0, grp(j, g), 0)),
         q16.reshape(batch, 4, 4, seq // 16, D_B),
         ((None, SUB, BAND, D_B), lambda b, j, g: (b, 0, grp(j, g), 0)),
         ((None, SUB, BAND, D_B), lambda b, j, g: (b, 0, prev(grp(j, g)), 0))),
        (pl.BlockSpec((None, SUB, BAND, D_B), lambda b, j, g: (b, g, j, 0)), q16,
         ((None, SUB, BAND, D_B), lambda b, j, g: (b, g, j, 0)),
         ((None, SUB, BAND, D_B), lambda b, j, g: (b, g, prev(j), 0))),
    ]
    in_specs, args = [], []
    for di, (q_spec, q_arr, cur, prv) in enumerate(branches):
        cs, ps = pl.BlockSpec(*cur), pl.BlockSpec(*prv)
        in_specs += [q_spec, cs, ps, cs, ps]
        kd, vd = (t[di] if di else t[di].reshape(batch, seq, D_B) for t in kv)
        args += [q_arr, kd, kd, vd, vd]
    in_specs += [_const_spec((3, 2, 2 * BAND, 2 * BAND)), _const_spec((1, D_B))]
    args += [bias, g_ob]
    out = pl.pallas_call(
        _attn_prompt_body,
        grid=(batch, n_tiles, n_groups),
        in_specs=in_specs,
        out_specs=pl.BlockSpec((None, TILE, D_B), lambda b, j, g: (b, j, 0)),
        out_shape=jax.ShapeDtypeStruct((batch, seq, D_B), BF16),
        scratch_shapes=[pltpu.VMEM((N_SLABS, TILE, LANES), F32)] * 6,
        compiler_params=_params(3),
        name="attn_prompt",
    )(*args)
    return out.reshape(batch * seq, D_B)


def _band_bias():
    kj = np.arange(2 * BAND)[None, :]
    rho = np.arange(BAND)
    out = []
    for runs in (16, 4, 1):
        per = BAND // runs
        qi = (runs * (rho % per) + rho // per)[:, None]
        ok = (kj >= qi) & (kj <= qi + BAND)
        out.append(np.stack([np.where(ok, 0.0, -np.inf), np.where(ok & (kj >= BAND), 0.0, -np.inf)]))
    return np.tile(np.stack(out), (1, 1, 2, 1)).astype(np.float32)


def _window_attention(kt_ref, vt_ref, kn_ref, vn_ref, q_ref, mult_ref, okt_ref, ovt_ref, o_ref,
                      new_shift):
    body = MAX_WINDOW - LANES
    is_new = lax.broadcasted_iota(jnp.int32, (HEAD_DIM, LANES), 1) >= LANES - T_NEW
    mult = mult_ref[...]

    def shift(src_ref, new_ref, dst_ref, h):
        x = src_ref[h]
        y = pltpu.roll(x, MAX_WINDOW - T_NEW, 1)
        tail = jnp.where(is_new, pltpu.roll(new_ref[h], new_shift, 1), y[:, body:])
        dst_ref[h, :, 0:body] = y[:, 0:body]
        dst_ref[h, :, body:MAX_WINDOW] = tail
        return jnp.concatenate([y[:, 0:body], tail, x[:, 0:LANES]], axis=1).astype(BF16)

    cats = []
    for h in range(kt_ref.shape[0]):
        cats.append((shift(kt_ref, kn_ref, okt_ref, h), shift(vt_ref, vn_ref, ovt_ref, h)))
    yield
    for h, (kcat, vcat) in enumerate(cats):
        s = jnp.where(mult > 0, _dot(q_ref[h], kcat), -jnp.inf)
        p = mult * jnp.exp(s - jnp.max(s, axis=-1, keepdims=True))
        o_ref[h] = _dot_nt(p.astype(BF16), vcat) / jnp.sum(p, axis=-1, keepdims=True)


def _sample_multiplicity():
    def mult(d):
        ok = d >= 0
        return (ok & (d <= 128)).astype(np.int32) + (ok & (d % 4 == 0) & (d <= 512)) \
            + (ok & (d % 16 == 0) & (d <= 2048))
    t = (np.arange(8) % T_NEW)[:, None]
    kept = mult(MAX_WINDOW - T_NEW + t - np.arange(MAX_WINDOW)[None, :])
    old = np.arange(LANES)[None, :]
    dropped = mult(MAX_WINDOW + t - old) * (old < T_NEW)
    return np.concatenate([kept, dropped], axis=1).astype(np.float32)


def _ffn_compute(x_ref, ya_ref, yb_ref, gob_ref, wout_ref, gffn_ref, wup_ref, wdown_ref, o_ref,
                 norm_b):
    yb = yb_ref[...]
    if norm_b:
        yb = _rms(yb, gob_ref[...])
    x1 = x_ref[...] + _dot(ya_ref[...], wout_ref[0:D_A, :])
    yield
    x1 = x1 + _dot(yb.astype(BF16), wout_ref[D_A:D_A + D_B, :])
    yield
    hb = _rms(x1, gffn_ref[...]).astype(BF16)
    step = D_FF // 4
    mlp = None
    for c in range(4):
        f = _dot(hb, wup_ref[:, c * step:(c + 1) * step])
        yield
        f = jnp.square(jnp.maximum(f, 0.0)).astype(BF16)
        d = _dot(f, wdown_ref[c * step:(c + 1) * step, :])
        mlp = d if mlp is None else mlp + d
        yield
    o_ref[...] = x1 + mlp


def _out_ffn_body(*refs, norm_b):
    for _ in _ffn_compute(*refs, norm_b):
        pass


def _ffn_window_body(*refs, steps_per_seq):
    n_ffn = 8
    ffn_in, (kt, vt, kn, vn, q, mult) = refs[:n_ffn], refs[n_ffn:n_ffn + 6]
    x_out, okt, ovt, o_att = refs[-4:]
    seq_idx = pl.program_id(0) // steps_per_seq
    window = _window_attention(kt, vt, kn, vn, q, mult, okt, ovt, o_att,
                               (LANES - T_NEW * (seq_idx + 1)) % LANES)
    next(window)
    for _ in _ffn_compute(*ffn_in, x_out, False):
        pass
    for _ in window:
        pass


def _ffn_specs(rows):
    row_spec = lambda w: pl.BlockSpec((rows, w), lambda i: (i, 0))
    return [row_spec(D_MODEL), row_spec(D_A), row_spec(D_B), _const_spec((1, D_B)),
            _const_spec((D_A + D_B, D_MODEL)), _const_spec((1, D_MODEL)),
            _const_spec((D_MODEL, D_FF)), _const_spec((D_FF, D_MODEL))], row_spec(D_MODEL)


def _out_ffn(x, ya, yb, lw, *, rows, norm_b):
    n = x.shape[0]
    in_specs, out_spec = _ffn_specs(rows)
    return pl.pallas_call(
        functools.partial(_out_ffn_body, norm_b=norm_b),
        grid=(n // rows,),
        in_specs=in_specs,
        out_specs=out_spec,
        out_shape=jax.ShapeDtypeStruct((n, D_MODEL), F32),
        compiler_params=_params(1),
        name="out_ffn",
    )(x, ya, yb, lw["g_ob"], lw["w_out"], lw["g_ffn"], lw["w_up"], lw["w_down"])


def _ffn_window(x, ya, yb, lw, layer, cache_kt, cache_vt, k_new, v_new, q, mult, prev_out, *, rows):
    n = x.shape[0]
    steps = n // rows
    depth, db = cache_kt.shape[:2]
    per_seq = steps // db
    hps = N_HEADS // per_seq
    assert per_seq * db == steps and hps * per_seq == N_HEADS
    in_specs, out_spec = _ffn_specs(rows)
    cache_spec = pl.BlockSpec((None, None, hps, HEAD_DIM, MAX_WINDOW),
                              lambda i: (layer, i // per_seq, i % per_seq, 0, 0))
    new_spec = pl.BlockSpec((hps, HEAD_DIM, LANES), lambda i: (i % per_seq, 0, 0))
    q_spec = pl.BlockSpec((None, hps, 8, HEAD_DIM), lambda i: (i // per_seq, i % per_seq, 0, 0))
    in_specs += [cache_spec, cache_spec, new_spec, new_spec, q_spec,
                 _const_spec((8, MAX_WINDOW + LANES))]
    args = [x, ya, yb, lw["g_ob"], lw["w_out"], lw["g_ffn"], lw["w_up"], lw["w_down"],
            cache_kt, cache_vt, k_new, v_new, q, mult]
    aliases = {}
    if prev_out is not None:
        in_specs += [pl.BlockSpec(memory_space=pl.ANY)] * 2
        aliases = {len(args): 1, len(args) + 1: 2}
        args += list(prev_out)
    win_shape = jax.ShapeDtypeStruct(cache_kt.shape, F32)
    return pl.pallas_call(
        functools.partial(_ffn_window_body, steps_per_seq=per_seq),
        grid=(steps,),
        in_specs=in_specs,
        out_specs=[out_spec, cache_spec, cache_spec, q_spec],
        out_shape=[jax.ShapeDtypeStruct((n, D_MODEL), F32), win_shape, win_shape,
                   jax.ShapeDtypeStruct((db, N_HEADS, 8, HEAD_DIM), F32)],
        input_output_aliases=aliases,
        compiler_params=_params(1),
        name="ffn_window",
    )(*args)


def _rope_tables(base_pos, offsets):
    half = ROT_DIM // 2
    inv = jnp.power(ROPE_THETA, -jnp.arange(half, dtype=F32) * (2.0 / ROT_DIM))
    lane = np.arange(LANES) % HEAD_DIM
    freq = jnp.where(lane < ROT_DIM, inv[lane % half], 0.0)
    ang = lambda pos: pos.astype(F32)[:, None] * freq[None, :]
    ab, ao = ang(base_pos), ang(offsets)
    tbase = jnp.stack([jnp.cos(ab), jnp.sin(ab)], axis=1)
    toff = jnp.stack([jnp.cos(ao), jnp.sin(ao)], axis=0)
    tsign = jnp.asarray(np.stack([-1.0 * (lane < half), 1.0 * ((lane >= half) & (lane < ROT_DIM))]), F32)
    return tbase, toff, tsign


def kernel(x_prompt, x_sample, cache_win_k, cache_win_v, attn_norm_g, w_in, v_norm_g, w_spatial, b_spatial, q_norm_g, k_norm_g, out_norm_a_g, out_norm_b_g, w_out, ffn_norm_g, w_up, w_down):
    depth = w_in.shape[0]
    batch, seq, _ = x_prompt.shape
    db, t_new, _ = x_sample.shape
    n_s = db * t_new
    assert n_s == CHUNK and t_new == T_NEW and cache_win_k.shape[2] == MAX_WINDOW
    assert seq % TILE == 0

    rows_p = 512
    tab_p = _rope_tables(jnp.arange(seq // rows_p) * rows_p, jnp.arange(rows_p))
    tab_s = _rope_tables(jnp.full((1,), PAST_LEN), jnp.arange(n_s) % t_new)

    tril = jnp.tril(jnp.ones((CHUNK, CHUNK), bool))
    tril_s = jnp.tril(jnp.ones((t_new, t_new), bool))
    hsum = jnp.kron(jnp.eye(N_HEADS // 2, dtype=F32), jnp.full((HEAD_DIM, HEAD_DIM), 1.0 / HEAD_DIM, F32)).astype(BF16)

    bias = jnp.asarray(_band_bias())
    mult = jnp.asarray(_sample_multiplicity())

    cache_kt = cache_win_k.transpose(0, 1, 3, 4, 2)
    cache_vt = cache_win_v.transpose(0, 1, 3, 4, 2)

    xp = x_prompt.reshape(batch * seq, D_MODEL)
    xs = x_sample.reshape(n_s, D_MODEL)
    kp_list, vp_list, cv_list = [], [], []
    win_out = None
    row = lambda g: g.reshape(1, -1)
    for l in range(depth):
        lw = dict(g_attn=row(attn_norm_g[l]), w_in=w_in[l].astype(BF16), g_v=row(v_norm_g[l]),
                  g_q=row(jnp.tile(q_norm_g[l], N_HEADS)), g_k=row(jnp.tile(k_norm_g[l], N_HEADS)),
                  hsum=hsum, g_oa=row(out_norm_a_g[l]), g_ob=row(out_norm_b_g[l]),
                  w_out=w_out[l].astype(BF16), g_ffn=row(ffn_norm_g[l]),
                  w_up=w_up[l].astype(BF16), w_down=w_down[l].astype(BF16))
        pair = lambda w: w.reshape(N_SLABS, 2, CHUNK, CHUNK).transpose(0, 2, 1, 3).reshape(N_SLABS, CHUNK, 2 * CHUNK)
        gate_w_p = pair(jnp.where(tril, w_spatial[l], 0).astype(BF16))
        gate_b_p = jnp.repeat(b_spatial[l].T, HEAD_DIM, axis=1)
        w4 = jnp.where(tril_s, w_spatial[l][:, :t_new, :t_new], 0)
        gate_w_s = (jnp.eye(db, dtype=F32)[None, :, None, :, None] * w4[:, None, :, None, :]) \
            .reshape(N_HEADS, CHUNK, CHUNK).astype(BF16)
        gate_w_s = pair(gate_w_s)
        gate_b_s = jnp.tile(jnp.repeat(b_spatial[l][:, :t_new].T, HEAD_DIM, axis=1), (db, 1))

        ya_p, q16, *kv, k32, v32 = _in_proj(xp, lw, tab_p, gate_w_p, gate_b_p, rows=rows_p,
                                            batch=batch, seq=seq)
        yb_p = _attn_prompt(q16, (kv[0:3], kv[3:6]), bias, lw["g_ob"], batch, seq)
        kp_list.append(k32.reshape(batch, N_HEADS, HEAD_DIM, MAX_WINDOW))
        vp_list.append(v32.reshape(batch, N_HEADS, HEAD_DIM, MAX_WINDOW))
        ya_s, q, kt, vt, va = _in_proj(xs, lw, tab_s, gate_w_s, gate_b_s, rows=CHUNK)
        cv_list.append(va.reshape(db, t_new, D_A))
        q8 = jnp.tile(q.reshape(db, t_new, N_HEADS, HEAD_DIM).transpose(0, 2, 1, 3), (1, 1, 2, 1))
        heads = lambda t: t.reshape(N_HEADS, HEAD_DIM, n_s)

        xp, okt, ovt, o = _ffn_window(xp, ya_p, yb_p, lw, l, cache_kt, cache_vt, heads(kt), heads(vt),
                                      q8, mult, win_out, rows=256)
        win_out = (okt, ovt)
        yb_s = o[:, :, :t_new].transpose(0, 2, 1, 3).reshape(n_s, D_B)
        xs = _out_ffn(xs, ya_s, yb_s, lw, rows=CHUNK, norm_b=True)

    unt = lambda w: w.transpose(0, 1, 4, 2, 3)
    return (xp.reshape(batch, seq, D_MODEL), xs.reshape(db, t_new, D_MODEL),
            unt(jnp.stack(kp_list)), unt(jnp.stack(vp_list)), unt(win_out[0]), unt(win_out[1]),
            jnp.stack(cv_list))
```

```python
import functools

import numpy as np
import jax
import jax.numpy as jnp
from jax import lax
from jax.experimental import pallas as pl
from jax.experimental.pallas import tpu as pltpu

D_MODEL = 1024
D_A = 512
D_B = 512
HEAD_DIM = 64
N_HEADS = 8
CHUNK = 128
D_FF = 4 * D_MODEL
ROT_DIM = HEAD_DIM // 4
ROPE_THETA = 500000.0
EPS = 1e-6
ATTN_SCALE = HEAD_DIM ** -0.5
DILATIONS = (1, 4, 16)
BAND = 128
TILE = 16 * BAND
SUB = 4
MAX_WINDOW = 2048
PAST_LEN = 16384
T_NEW = 4

LANES = 128
N_SLABS = D_B // LANES
VMEM_LIMIT = 56 * 1024 * 1024

F32 = jnp.float32
BF16 = jnp.bfloat16


def _rms(x, g):
    return x * lax.rsqrt(jnp.mean(x * x, axis=-1, keepdims=True) + EPS) * g


def _gelu(x):
    c = np.float32(np.sqrt(2.0 / np.pi))
    return x * (0.5 * (1.0 + jnp.tanh(c * (x + 0.044715 * (x * x * x)))))


def _dot(a, b):
    return jnp.dot(a, b, preferred_element_type=F32)


def _dot_nt(a, b):
    return lax.dot_general(a, b, (((1,), (1,)), ((), ())), preferred_element_type=F32)


def _params(n_axes):
    return pltpu.CompilerParams(dimension_semantics=("arbitrary",) * n_axes,
                                vmem_limit_bytes=VMEM_LIMIT)


def _const_spec(shape):
    return pl.BlockSpec(shape, lambda *_: (0,) * len(shape), pipeline_mode=pl.Buffered(1))


def _layer_spec(shape, layer):
    return pl.BlockSpec((None,) + shape, lambda *_: (layer,) + (0,) * len(shape),
                        pipeline_mode=pl.Buffered(1))


def _slab(p):
    return slice(p * LANES, (p + 1) * LANES)


def _in_proj_body(x_ref, gattn_ref, win_ref, gv_ref, gq_ref, gk_ref, hsum_ref, tbase_ref, toff_ref,
                  tsign_ref, wg_ref, bg_ref, goa_ref, *outs, rows, tiles_per_seq, n_in):
    dilated = tiles_per_seq is not None
    outs = outs[n_in - 13:]
    hb = _rms(x_ref[...], gattn_ref[...]).astype(BF16)

    cb, sb = tbase_ref[0:1, :], tbase_ref[1:2, :]
    co, so = toff_ref[0], toff_ref[1]
    cos = cb * co - sb * so
    sin = sb * co + cb * so
    sin_up, sin_dn = sin * tsign_ref[0:1, :], sin * tsign_ref[1:2, :]

    def proj(i):
        return _dot(hb, win_ref[:, i * 512:(i + 1) * 512])

    def head_norm_rope(z, g_ref):
        zz = (z * z).astype(BF16)
        half = D_B // 2
        ms = jnp.concatenate([_dot(zz[:, :half], hsum_ref[...]), _dot(zz[:, half:], hsum_ref[...])],
                             axis=1)
        zn = z * lax.rsqrt(ms + EPS) * g_ref[...]
        slabs = []
        for p in range(N_SLABS):
            s = zn[:, _slab(p)]
            slabs.append(s * cos + pltpu.roll(s, LANES - ROT_DIM // 2, 1) * sin_up
                         + pltpu.roll(s, ROT_DIM // 2, 1) * sin_dn)
        return jnp.concatenate(slabs, axis=1)

    if dilated:
        ya_ref, q16, k1, k4, k16, v1, v4, v16, kwin_ref, vwin_ref, scr1, scr4 = outs

        def split(val, r4_ref, r16_ref):
            for p in range(N_SLABS):
                scr1[p] = val[:, _slab(p)]
            n4, n16 = rows // 4, rows // 16
            for p in range(N_SLABS):
                for r in range(4):
                    part = scr1[p, pl.ds(r, n4, stride=4), :]
                    scr4[p, r * n4:(r + 1) * n4, :] = part
                    if r4_ref is not None:
                        r4_ref[r, :, _slab(p)] = part.astype(BF16)
                for r in range(4):
                    for c in range(4):
                        part = scr4[p, pl.ds(r * n4 + c, n16, stride=4), :]
                        r16_ref[r + 4 * c, :, _slab(p)] = part.astype(r16_ref.dtype)
    else:
        ya_ref, q1, kt_ref, vt_ref, va_ref = outs

    zu = proj(0)
    zq = proj(2)
    u = _gelu(zu)
    q = head_norm_rope(zq, gq_ref) * ATTN_SCALE
    zva = proj(1)
    if dilated:
        split(q, None, q16)
    else:
        q1[...] = q.astype(BF16)
    zk = proj(3)
    va = _rms(_gelu(zva), gv_ref[...])
    k = head_norm_rope(zk, gk_ref)
    v = proj(4)
    if dilated:
        k1[...] = k.astype(BF16)
        split(k, k4, k16)
        v1[...] = v.astype(BF16)
        split(v, v4, v16)
    else:
        va_ref[...] = va
        kt_ref[...] = k.T
        vt_ref[...] = v.T

    lane_lo = lax.broadcasted_iota(jnp.int32, (CHUNK, LANES), 1) < HEAD_DIM
    n_chunks = rows // CHUNK
    side = min(2, n_chunks)
    ys = [[None] * N_SLABS for _ in range(n_chunks)]
    for c0 in range(0, n_chunks, side):
        for p in range(N_SLABS):
            parts = [va[c * CHUNK:(c + 1) * CHUNK, _slab(p)] for c in range(c0, c0 + side)]
            lo = jnp.concatenate([jnp.where(lane_lo, t, 0.0) for t in parts], axis=1)
            hi = jnp.concatenate([jnp.where(lane_lo, 0.0, t) for t in parts], axis=1)
            s = _dot(wg_ref[p], jnp.concatenate([lo, hi], axis=0).astype(BF16))
            for i in range(side):
                c = c0 + i
                ys[c][p] = u[c * CHUNK:(c + 1) * CHUNK, _slab(p)] * (s[:, _slab(i)] + bg_ref[:, _slab(p)])
    for c in range(n_chunks):
        ya_ref[c * CHUNK:(c + 1) * CHUNK, :] = _rms(jnp.concatenate(ys[c], axis=1), goa_ref[...]).astype(BF16)

    if dilated:
        @pl.when(pl.program_id(0) % tiles_per_seq >= tiles_per_seq - MAX_WINDOW // rows)
        def _():
            kwin_ref[...] = k.T
            vwin_ref[...] = v.T


def _in_proj(x, pw, layer, tables, gate_w, gate_b, *, rows, batch=None, seq=None, prev_win=None):
    n = x.shape[0]
    depth = pw["w_in"].shape[0]
    dilated = batch is not None
    tbase, toff, tsign = tables
    n_base = tbase.shape[0]
    row_spec = lambda w: pl.BlockSpec((rows, w), lambda i: (i, 0))
    lspec = lambda *shape: _layer_spec(shape, layer)
    flat = lambda dt: jax.ShapeDtypeStruct((n, 512), dt)
    scratch = []
    per_b = None
    in_specs = [row_spec(D_MODEL), lspec(1, D_MODEL), lspec(D_MODEL, 5 * 512),
                lspec(1, D_A), lspec(1, D_B), lspec(1, D_B),
                _const_spec((D_B // 2, D_B // 2)),
                pl.BlockSpec((None, 2, LANES), lambda i: (i % n_base, 0, 0)),
                _const_spec((2, rows, LANES)), _const_spec((2, LANES)),
                lspec(N_SLABS, CHUNK, 2 * CHUNK), lspec(CHUNK, D_A), lspec(1, D_A)]
    args = [x, pw["g_attn"], pw["w_in"], pw["g_v"], pw["g_q"], pw["g_k"], pw["hsum"], tbase, toff,
            tsign, gate_w, gate_b, pw["g_oa"]]
    aliases = {}
    if dilated:
        per_b = seq // rows
        split_shape = lambda d, dt: jax.ShapeDtypeStruct((batch, d, seq // d, D_B), dt)
        split_spec = lambda d: pl.BlockSpec((None, d, rows // d, D_B),
                                            lambda i: (i // per_b, 0, i % per_b, 0))
        kv_shape = [flat(BF16), split_shape(4, BF16), split_shape(16, BF16)]
        kv_spec = [row_spec(512), split_spec(4), split_spec(16)]
        first_win = per_b - MAX_WINDOW // rows
        win_shape = jax.ShapeDtypeStruct((depth, batch, D_B, MAX_WINDOW), F32)
        win_spec = pl.BlockSpec((None, None, D_B, rows),
                                lambda i: (layer, i // per_b, 0, jnp.maximum(i % per_b - first_win, 0)))
        out_shape = [flat(BF16), split_shape(16, F32)] + kv_shape * 2 + [win_shape] * 2
        out_specs = [row_spec(512), split_spec(16)] + kv_spec * 2 + [win_spec] * 2
        scratch = [pltpu.VMEM((N_SLABS, rows, LANES), F32)] * 2
        if prev_win is not None:
            in_specs += [pl.BlockSpec(memory_space=pl.ANY)] * 2
            aliases = {len(args): len(out_shape) - 2, len(args) + 1: len(out_shape) - 1}
            args += list(prev_win)
    else:
        t_shape = jax.ShapeDtypeStruct((512, n), F32)
        t_spec = pl.BlockSpec((512, rows), lambda i: (0, i))
        out_shape = [flat(BF16), flat(BF16), t_shape, t_shape, flat(F32)]
        out_specs = [row_spec(512), row_spec(512), t_spec, t_spec, row_spec(512)]
    return pl.pallas_call(
        functools.partial(_in_proj_body, rows=rows, tiles_per_seq=per_b, n_in=len(args)),
        grid=(n // rows,),
        in_specs=in_specs,
        out_specs=out_specs,
        out_shape=out_shape,
        scratch_shapes=scratch,
        input_output_aliases=aliases,
        compiler_params=_params(1),
        name="in_proj",
    )(*args)


def _band_block(q, kp, kc, vp, vc, bias):
    lane_lo = lax.broadcasted_iota(jnp.int32, (BAND, LANES), 1) < HEAD_DIM
    head_mask = [(lax.broadcasted_iota(jnp.int32, (1, LANES), 1) // HEAD_DIM == hh)
                 .astype(F32).astype(BF16) for hh in range(2)]
    ones = jnp.ones((2 * BAND, LANES), BF16)
    pick = lambda t: jnp.where(lane_lo, t[:BAND], t[BAND:])
    out = []
    for p in range(N_SLABS):
        qp = q[:, _slab(p)]
        kk = jnp.concatenate([kp[:, _slab(p)], kc[:, _slab(p)]], axis=0)
        vv = jnp.concatenate([vp[:, _slab(p)], vc[:, _slab(p)]], axis=0)
        qq = jnp.concatenate([qp * head_mask[0], qp * head_mask[1]], axis=0)
        s = _dot_nt(qq, kk) + bias
        m = jnp.max(s, axis=-1, keepdims=True)
        pr = jnp.exp(s - m).astype(BF16)
        a = _dot(pr, jnp.concatenate([vv, ones], axis=1))
        l = pick(a[:, LANES:])
        out.append((pick(a[:, :LANES]) * (1.0 / l), pick(m) + jnp.log(l)))
    return out


def _attn_prompt_body(q1, k1c, k1p, v1c, v1p, q4, k4c, k4p, v4c, v4p, q16, k16c, k16p, v16c, v16p,
                      bias_ref, gob_ref, out_ref, o1, s1, o4, s4, o16, s16):
    j = pl.program_id(1)
    g = pl.program_id(2)
    n_groups = pl.num_programs(2)

    def run(branch, q, kc, kp, vc, vp, first, refs, runs):
        bias = bias_ref[branch, first] if isinstance(first, int) else \
            bias_ref[branch, first.astype(jnp.int32)]
        res = _band_block(q.reshape(BAND, D_B).astype(BF16), kp, kc, vp, vc, bias)
        n = BAND // len(runs)
        for p in range(N_SLABS):
            for ref, val in zip(refs, res[p]):
                for i, row0 in enumerate(runs):
                    ref[p, pl.ds(pl.multiple_of(row0, n), n), :] = val[i * n:(i + 1) * n]

    for i in range(SUB):
        c = SUB * g + i
        blk = slice(i * BAND, (i + 1) * BAND)
        if i == 0:
            kp, vp, first = k1p[...], v1p[...], (n_groups * j + g) == 0
        else:
            before = slice((i - 1) * BAND, i * BAND)
            kp, vp, first = k1c[before, :], v1c[before, :], 0
        run(0, q1[:, 8 * i:8 * (i + 1), :], k1c[blk, :], kp, v1c[blk, :], vp, first, (o1, s1),
            [BAND * r + 8 * c for r in range(16)])
        run(1, q4[:, i], k4c[i], k4p[i], v4c[i], v4p[i], (n_groups * j + g) == 0, (o4, s4),
            [BAND * (4 * cc + i) + 32 * g for cc in range(4)])
        run(2, q16[i], k16c[i], k16p[i], v16c[i], v16p[i], j == 0, (o16, s16), [BAND * c])

    @pl.when(g == n_groups - 1)
    def _finalize():
        parts = [(o1, s1), (o4, s4), (o16, s16)]
        for r in range(16):
            rows = pl.ds(r * BAND, BAND)
            slabs = []
            for p in range(N_SLABS):
                lse = [s_ref[p, rows, :] for (_, s_ref) in parts]
                top = jnp.maximum(jnp.maximum(lse[0], lse[1]), lse[2])
                num = den = None
                for e, (o_ref, _) in zip(lse, parts):
                    w = jnp.exp(e - top)
                    t = o_ref[p, rows, :] * w
                    num, den = (t, w) if num is None else (num + t, den + w)
                slabs.append(num / den)
            y = _rms(jnp.concatenate(slabs, axis=1), gob_ref[...])
            for p in range(N_SLABS):
                o16[p, rows, :] = y[:, _slab(p)]
        for p in range(N_SLABS):
            for r in range(16):
                o1[p, pl.ds(r, BAND, stride=16), :] = o16[p, r * BAND:(r + 1) * BAND, :]
            out_ref[:, _slab(p)] = o1[p].astype(BF16)


def _attn_prompt(q16, kv, bias, g_ob, layer, batch, seq):
    assert SUB == 4
    n_tiles = seq // TILE
    n_groups = 16 // SUB
    span = SUB * BAND
    prev = lambda i: jnp.maximum(i - 1, 0)
    grp = lambda j, g: n_groups * j + g
    branches = [
        (pl.BlockSpec((None, 16, span // 16, D_B), lambda b, j, g: (b, 0, grp(j, g), 0)), q16,
         ((None, span, D_B), lambda b, j, g: (b, grp(j, g), 0)),
         ((None, BAND, D_B), lambda b, j, g: (b, prev(SUB * grp(j, g)), 0))),
        (pl.BlockSpec((None, 4, SUB, span // 16, D_B), lambda b, j, g: (b, 0, 0, grp(j, g), 0)),
         q16.reshape(batch, 4, 4, seq // 16, D_B),
         ((None, SUB, BAND, D_B), lambda b, j, g: (b, 0, grp(j, g), 0)),
         ((None, SUB, BAND, D_B), lambda b, j, g: (b, 0, prev(grp(j, g)), 0))),
        (pl.BlockSpec((None, SUB, BAND, D_B), lambda b, j, g: (b, g, j, 0)), q16,
         ((None, SUB, BAND, D_B), lambda b, j, g: (b, g, j, 0)),
         ((None, SUB, BAND, D_B), lambda b, j, g: (b, g, prev(j), 0))),
    ]
    in_specs, args = [], []
    for di, (q_spec, q_arr, cur, prv) in enumerate(branches):
        cs, ps = pl.BlockSpec(*cur), pl.BlockSpec(*prv)
        in_specs += [q_spec, cs, ps, cs, ps]
        kd, vd = (t[di] if di else t[di].reshape(batch, seq, D_B) for t in kv)
        args += [q_arr, kd, kd, vd, vd]
    in_specs += [_const_spec((3, 2, 2 * BAND, 2 * BAND)), _layer_spec((1, D_B), layer)]
    args += [bias, g_ob]
    out = pl.pallas_call(
        _attn_prompt_body,
        grid=(batch, n_tiles, n_groups),
        in_specs=in_specs,
        out_specs=pl.BlockSpec((None, TILE, D_B), lambda b, j, g: (b, j, 0)),
        out_shape=jax.ShapeDtypeStruct((batch, seq, D_B), BF16),
        scratch_shapes=[pltpu.VMEM((N_SLABS, TILE, LANES), F32)] * 6,
        compiler_params=_params(3),
        name="attn_prompt",
    )(*args)
    return out.reshape(batch * seq, D_B)


def _band_bias():
    kj = np.arange(2 * BAND)[None, :]
    rho = np.arange(BAND)
    out = []
    for runs in (16, 4, 1):
        per = BAND // runs
        qi = (runs * (rho % per) + rho // per)[:, None]
        ok = (kj >= qi) & (kj <= qi + BAND)
        out.append(np.stack([np.where(ok, 0.0, -np.inf), np.where(ok & (kj >= BAND), 0.0, -np.inf)]))
    return np.tile(np.stack(out), (1, 1, 2, 1)).astype(np.float32)


def _window_attention(kt_ref, vt_ref, kn_ref, vn_ref, q_ref, mult_ref, okt_ref, ovt_ref, o_ref,
                      new_shift):
    body = MAX_WINDOW - LANES
    is_new = lax.broadcasted_iota(jnp.int32, (HEAD_DIM, LANES), 1) >= LANES - T_NEW
    mult = mult_ref[...]

    def shift(src_ref, new_ref, dst_ref, h):
        x = src_ref[h]
        y = pltpu.roll(x, MAX_WINDOW - T_NEW, 1)
        tail = jnp.where(is_new, pltpu.roll(new_ref[h], new_shift, 1), y[:, body:])
        dst_ref[h, :, 0:body] = y[:, 0:body]
        dst_ref[h, :, body:MAX_WINDOW] = tail
        return jnp.concatenate([y[:, 0:body], tail, x[:, 0:LANES]], axis=1).astype(BF16)

    cats = []
    for h in range(kt_ref.shape[0]):
        cats.append((shift(kt_ref, kn_ref, okt_ref, h), shift(vt_ref, vn_ref, ovt_ref, h)))
    yield
    for h, (kcat, vcat) in enumerate(cats):
        s = jnp.where(mult > 0, _dot(q_ref[h], kcat), -jnp.inf)
        p = mult * jnp.exp(s - jnp.max(s, axis=-1, keepdims=True))
        o_ref[h] = _dot_nt(p.astype(BF16), vcat) / jnp.sum(p, axis=-1, keepdims=True)


def _sample_multiplicity():
    def mult(d):
        ok = d >= 0
        return (ok & (d <= 128)).astype(np.int32) + (ok & (d % 4 == 0) & (d <= 512)) \
            + (ok & (d % 16 == 0) & (d <= 2048))
    t = (np.arange(8) % T_NEW)[:, None]
    kept = mult(MAX_WINDOW - T_NEW + t - np.arange(MAX_WINDOW)[None, :])
    old = np.arange(LANES)[None, :]
    dropped = mult(MAX_WINDOW + t - old) * (old < T_NEW)
    return np.concatenate([kept, dropped], axis=1).astype(np.float32)


def _ffn_compute(x_ref, ya_ref, yb_ref, gob_ref, wout_ref, gffn_ref, wup_ref, wdown_ref, o_ref,
                 norm_b):
    yb = yb_ref[...]
    if norm_b:
        yb = _rms(yb, gob_ref[...])
    x1 = x_ref[...] + _dot(ya_ref[...], wout_ref[0:D_A, :])
    yield
    x1 = x1 + _dot(yb.astype(BF16), wout_ref[D_A:D_A + D_B, :])
    yield
    hb = _rms(x1, gffn_ref[...]).astype(BF16)
    step = D_FF // 4
    mlp = None
    for c in range(4):
        f = _dot(hb, wup_ref[:, c * step:(c + 1) * step])
        yield
        f = jnp.square(jnp.maximum(f, 0.0)).astype(BF16)
        d = _dot(f, wdown_ref[c * step:(c + 1) * step, :])
        mlp = d if mlp is None else mlp + d
        yield
    o_ref[...] = x1 + mlp


def _out_ffn_body(*refs, norm_b):
    for _ in _ffn_compute(*refs, norm_b):
        pass


def _ffn_window_body(*refs, steps_per_seq):
    n_ffn = 8
    ffn_in, (kt, vt, kn, vn, q, mult) = refs[:n_ffn], refs[n_ffn:n_ffn + 6]
    x_out, okt, ovt, o_att = refs[-4:]
    seq_idx = pl.program_id(0) // steps_per_seq
    window = _window_attention(kt, vt, kn, vn, q, mult, okt, ovt, o_att,
                               (LANES - T_NEW * (seq_idx + 1)) % LANES)
    next(window)
    for _ in _ffn_compute(*ffn_in, x_out, False):
        pass
    for _ in window:
        pass


def _ffn_specs(rows, layer):
    row_spec = lambda w: pl.BlockSpec((rows, w), lambda i: (i, 0))
    lspec = lambda *shape: _layer_spec(shape, layer)
    return [row_spec(D_MODEL), row_spec(D_A), row_spec(D_B), lspec(1, D_B),
            lspec(D_A + D_B, D_MODEL), lspec(1, D_MODEL),
            lspec(D_MODEL, D_FF), lspec(D_FF, D_MODEL)], row_spec(D_MODEL)


def _out_ffn(x, ya, yb, pw, layer, *, rows, norm_b):
    n = x.shape[0]
    in_specs, out_spec = _ffn_specs(rows, layer)
    return pl.pallas_call(
        functools.partial(_out_ffn_body, norm_b=norm_b),
        grid=(n // rows,),
        in_specs=in_specs,
        out_specs=out_spec,
        out_shape=jax.ShapeDtypeStruct((n, D_MODEL), F32),
        compiler_params=_params(1),
        name="out_ffn",
    )(x, ya, yb, pw["g_ob"], pw["w_out"], pw["g_ffn"], pw["w_up"], pw["w_down"])


def _ffn_window(x, ya, yb, pw, layer, cache_kt, cache_vt, k_new, v_new, q, mult, prev_out, *, rows):
    n = x.shape[0]
    steps = n // rows
    depth, db = cache_kt.shape[:2]
    per_seq = steps // db
    hps = N_HEADS // per_seq
    assert per_seq * db == steps and hps * per_seq == N_HEADS
    in_specs, out_spec = _ffn_specs(rows, layer)
    cache_spec = pl.BlockSpec((None, None, hps, HEAD_DIM, MAX_WINDOW),
                              lambda i: (layer, i // per_seq, i % per_seq, 0, 0))
    new_spec = pl.BlockSpec((hps, HEAD_DIM, LANES), lambda i: (i % per_seq, 0, 0))
    q_spec = pl.BlockSpec((None, hps, 8, HEAD_DIM), lambda i: (i // per_seq, i % per_seq, 0, 0))
    in_specs += [cache_spec, cache_spec, new_spec, new_spec, q_spec,
                 _const_spec((8, MAX_WINDOW + LANES))]
    args = [x, ya, yb, pw["g_ob"], pw["w_out"], pw["g_ffn"], pw["w_up"], pw["w_down"],
            cache_kt, cache_vt, k_new, v_new, q, mult]
    aliases = {}
    if prev_out is not None:
        in_specs += [pl.BlockSpec(memory_space=pl.ANY)] * 2
        aliases = {len(args): 1, len(args) + 1: 2}
        args += list(prev_out)
    win_shape = jax.ShapeDtypeStruct(cache_kt.shape, F32)
    return pl.pallas_call(
        functools.partial(_ffn_window_body, steps_per_seq=per_seq),
        grid=(steps,),
        in_specs=in_specs,
        out_specs=[out_spec, cache_spec, cache_spec, q_spec],
        out_shape=[jax.ShapeDtypeStruct((n, D_MODEL), F32), win_shape, win_shape,
                   jax.ShapeDtypeStruct((db, N_HEADS, 8, HEAD_DIM), F32)],
        input_output_aliases=aliases,
        compiler_params=_params(1),
        name="ffn_window",
    )(*args)


def _rope_tables(base_pos, offsets):
    half = ROT_DIM // 2
    inv = jnp.power(ROPE_THETA, -jnp.arange(half, dtype=F32) * (2.0 / ROT_DIM))
    lane = np.arange(LANES) % HEAD_DIM
    freq = jnp.where(lane < ROT_DIM, inv[lane % half], 0.0)
    ang = lambda pos: pos.astype(F32)[:, None] * freq[None, :]
    ab, ao = ang(base_pos), ang(offsets)
    tbase = jnp.stack([jnp.cos(ab), jnp.sin(ab)], axis=1)
    toff = jnp.stack([jnp.cos(ao), jnp.sin(ao)], axis=0)
    tsign = jnp.asarray(np.stack([-1.0 * (lane < half), 1.0 * ((lane >= half) & (lane < ROT_DIM))]), F32)
    return tbase, toff, tsign


def kernel(x_prompt, x_sample, cache_win_k, cache_win_v, attn_norm_g, w_in, v_norm_g, w_spatial, b_spatial, q_norm_g, k_norm_g, out_norm_a_g, out_norm_b_g, w_out, ffn_norm_g, w_up, w_down):
    depth = w_in.shape[0]
    batch, seq, _ = x_prompt.shape
    db, t_new, _ = x_sample.shape
    n_s = db * t_new
    assert n_s == CHUNK and t_new == T_NEW and cache_win_k.shape[2] == MAX_WINDOW
    assert seq % TILE == 0

    rows_p = 512
    tab_p = _rope_tables(jnp.arange(seq // rows_p) * rows_p, jnp.arange(rows_p))
    tab_s = _rope_tables(jnp.full((1,), PAST_LEN), jnp.arange(n_s) % t_new)

    tril = jnp.tril(jnp.ones((CHUNK, CHUNK), bool))
    hsum = jnp.kron(jnp.eye(N_HEADS // 2, dtype=F32), jnp.full((HEAD_DIM, HEAD_DIM), 1.0 / HEAD_DIM, F32)).astype(BF16)
    row = lambda g: g[:, None, :]
    per_head = lambda g: jnp.tile(g, (1, N_HEADS))
    pw = dict(g_attn=row(attn_norm_g), w_in=w_in.astype(BF16), g_v=row(v_norm_g),
              g_q=row(per_head(q_norm_g)), g_k=row(per_head(k_norm_g)), hsum=hsum,
              g_oa=row(out_norm_a_g), g_ob=row(out_norm_b_g), w_out=w_out.astype(BF16),
              g_ffn=row(ffn_norm_g), w_up=w_up.astype(BF16), w_down=w_down.astype(BF16))
    pair = lambda w: w.reshape(depth, N_SLABS, 2, CHUNK, CHUNK).transpose(0, 1, 3, 2, 4) \
        .reshape(depth, N_SLABS, CHUNK, 2 * CHUNK).astype(BF16)
    gate_w_p = pair(jnp.where(tril, w_spatial, 0))
    gate_b_p = jnp.repeat(b_spatial.transpose(0, 2, 1), HEAD_DIM, axis=2)
    t_of = np.arange(CHUNK) % t_new
    same_seq = (np.arange(CHUNK)[:, None] // t_new) == (np.arange(CHUNK)[None, :] // t_new)
    causal_s = jnp.asarray(same_seq & (t_of[:, None] >= t_of[None, :]))
    gate_w_s = pair(jnp.where(causal_s, w_spatial[:, :, t_of[:, None], t_of[None, :]], 0))
    gate_b_s = jnp.repeat(b_spatial[:, :, t_of].transpose(0, 2, 1), HEAD_DIM, axis=2)

    bias = jnp.asarray(_band_bias())
    mult = jnp.asarray(_sample_multiplicity())

    cache_kt = cache_win_k.transpose(0, 1, 3, 4, 2)
    cache_vt = cache_win_v.transpose(0, 1, 3, 4, 2)

    xp = x_prompt.reshape(batch * seq, D_MODEL)
    xs = x_sample.reshape(n_s, D_MODEL)
    cv_list = []
    win_p = win_s = None
    for l in range(depth):
        ya_p, q16, *kv, kwin, vwin = _in_proj(xp, pw, l, tab_p, gate_w_p, gate_b_p, rows=rows_p,
                                              batch=batch, seq=seq, prev_win=win_p)
        win_p = (kwin, vwin)
        yb_p = _attn_prompt(q16, (kv[0:3], kv[3:6]), bias, pw["g_ob"], l, batch, seq)
        ya_s, q, kt, vt, va = _in_proj(xs, pw, l, tab_s, gate_w_s, gate_b_s, rows=CHUNK)
        cv_list.append(va.reshape(db, t_new, D_A))
        q8 = jnp.tile(q.reshape(db, t_new, N_HEADS, HEAD_DIM).transpose(0, 2, 1, 3), (1, 1, 2, 1))
        heads = lambda t: t.reshape(N_HEADS, HEAD_DIM, n_s)

        xp, okt, ovt, o = _ffn_window(xp, ya_p, yb_p, pw, l, cache_kt, cache_vt, heads(kt), heads(vt),
                                      q8, mult, win_s, rows=256)
        win_s = (okt, ovt)
        yb_s = o[:, :, :t_new].transpose(0, 2, 1, 3).reshape(n_s, D_B)
        xs = _out_ffn(xs, ya_s, yb_s, pw, l, rows=CHUNK, norm_b=True)

    win_p = [w.reshape(depth, batch, N_HEADS, HEAD_DIM, MAX_WINDOW) for w in win_p]
    unt = lambda w: w.transpose(0, 1, 4, 2, 3)
    return (xp.reshape(batch, seq, D_MODEL), xs.reshape(db, t_new, D_MODEL),
            unt(win_p[0]), unt(win_p[1]), unt(win_s[0]), unt(win_s[1]),
            jnp.stack(cv_list))
```

```python
import functools

import numpy as np
import jax
import jax.numpy as jnp
from jax import lax
from jax.experimental import pallas as pl
from jax.experimental.pallas import tpu as pltpu

D_MODEL = 1024
D_A = 512
D_B = 512
HEAD_DIM = 64
N_HEADS = 8
CHUNK = 128
D_FF = 4 * D_MODEL
ROT_DIM = HEAD_DIM // 4
ROPE_THETA = 500000.0
EPS = 1e-6
ATTN_SCALE = HEAD_DIM ** -0.5
DILATIONS = (1, 4, 16)
BAND = 128
TILE = 16 * BAND
SUB = 4
MAX_WINDOW = 2048
PAST_LEN = 16384
T_NEW = 4

LANES = 128
N_SLABS = D_B // LANES
VMEM_LIMIT = 56 * 1024 * 1024

F32 = jnp.float32
BF16 = jnp.bfloat16


def _rms(x, g):
    return x * lax.rsqrt(jnp.mean(x * x, axis=-1, keepdims=True) + EPS) * g


def _gelu(x):
    c = np.float32(np.sqrt(2.0 / np.pi))
    return x * (0.5 * (1.0 + jnp.tanh(c * (x + 0.044715 * (x * x * x)))))


def _dot(a, b):
    return jnp.dot(a, b, preferred_element_type=F32)


def _dot_nt(a, b):
    return lax.dot_general(a, b, (((1,), (1,)), ((), ())), preferred_element_type=F32)


def _params(n_axes):
    return pltpu.CompilerParams(dimension_semantics=("arbitrary",) * n_axes,
                                vmem_limit_bytes=VMEM_LIMIT)


def _const_spec(shape):
    return pl.BlockSpec(shape, lambda *_: (0,) * len(shape), pipeline_mode=pl.Buffered(1))


def _layer_spec(shape, layer):
    return pl.BlockSpec((None,) + shape, lambda *_: (layer,) + (0,) * len(shape),
                        pipeline_mode=pl.Buffered(1))


def _slab(p):
    return slice(p * LANES, (p + 1) * LANES)


def _in_proj_body(x_ref, gattn_ref, win_ref, gv_ref, gq_ref, gk_ref, hsum_ref, tbase_ref, toff_ref,
                  tsign_ref, wg_ref, bg_ref, goa_ref, *outs, rows, tiles_per_seq, n_in):
    dilated = tiles_per_seq is not None
    outs = outs[n_in - 13:]
    hb = _rms(x_ref[...], gattn_ref[...]).astype(BF16)

    cb, sb = tbase_ref[0:1, :], tbase_ref[1:2, :]
    co, so = toff_ref[0], toff_ref[1]
    cos = cb * co - sb * so
    sin = sb * co + cb * so
    sin_up, sin_dn = sin * tsign_ref[0:1, :], sin * tsign_ref[1:2, :]

    def proj(i):
        return _dot(hb, win_ref[:, i * 512:(i + 1) * 512])

    def head_norm_rope(z, g_ref):
        zz = (z * z).astype(BF16)
        half = D_B // 2
        ms = jnp.concatenate([_dot(zz[:, :half], hsum_ref[...]), _dot(zz[:, half:], hsum_ref[...])],
                             axis=1)
        zn = z * lax.rsqrt(ms + EPS) * g_ref[...]
        slabs = []
        for p in range(N_SLABS):
            s = zn[:, _slab(p)]
            slabs.append(s * cos + pltpu.roll(s, LANES - ROT_DIM // 2, 1) * sin_up
                         + pltpu.roll(s, ROT_DIM // 2, 1) * sin_dn)
        return jnp.concatenate(slabs, axis=1)

    if dilated:
        ya_ref, q16, k1, k4, k16, v1, v4, v16, kwin_ref, vwin_ref, scr1, scr4 = outs

        def split(val, r4_ref, r16_ref):
            for p in range(N_SLABS):
                scr1[p] = val[:, _slab(p)]
            n4, n16 = rows // 4, rows // 16
            for p in range(N_SLABS):
                for r in range(4):
                    part = scr1[p, pl.ds(r, n4, stride=4), :]
                    scr4[p, r * n4:(r + 1) * n4, :] = part
                    if r4_ref is not None:
                        r4_ref[r, :, _slab(p)] = part.astype(BF16)
                for r in range(4):
                    for c in range(4):
                        part = scr4[p, pl.ds(r * n4 + c, n16, stride=4), :]
                        r16_ref[r + 4 * c, :, _slab(p)] = part.astype(r16_ref.dtype)
    else:
        ya_ref, q1, kt_ref, vt_ref, va_ref = outs

    zu = proj(0)
    zq = proj(2)
    u = _gelu(zu)
    q = head_norm_rope(zq, gq_ref) * ATTN_SCALE
    zva = proj(1)
    if dilated:
        split(q, None, q16)
    else:
        q1[...] = q.astype(BF16)
    zk = proj(3)
    va = _rms(_gelu(zva), gv_ref[...])
    k = head_norm_rope(zk, gk_ref)
    v = proj(4)
    if dilated:
        k1[...] = k.astype(BF16)
        split(k, k4, k16)
        v1[...] = v.astype(BF16)
        split(v, v4, v16)
    else:
        va_ref[...] = va
        kt_ref[...] = k.T
        vt_ref[...] = v.T

    lane_lo = lax.broadcasted_iota(jnp.int32, (CHUNK, LANES), 1) < HEAD_DIM
    n_chunks = rows // CHUNK
    side = min(2, n_chunks)
    ys = [[None] * N_SLABS for _ in range(n_chunks)]
    for c0 in range(0, n_chunks, side):
        for p in range(N_SLABS):
            parts = [va[c * CHUNK:(c + 1) * CHUNK, _slab(p)] for c in range(c0, c0 + side)]
            lo = jnp.concatenate([jnp.where(lane_lo, t, 0.0) for t in parts], axis=1)
            hi = jnp.concatenate([jnp.where(lane_lo, 0.0, t) for t in parts], axis=1)
            s = _dot(wg_ref[p], jnp.concatenate([lo, hi], axis=0).astype(BF16))
            for i in range(side):
                c = c0 + i
                ys[c][p] = u[c * CHUNK:(c + 1) * CHUNK, _slab(p)] * (s[:, _slab(i)] + bg_ref[:, _slab(p)])
    for c in range(n_chunks):
        ya_ref[c * CHUNK:(c + 1) * CHUNK, :] = _rms(jnp.concatenate(ys[c], axis=1), goa_ref[...]).astype(BF16)

    if dilated:
        @pl.when(pl.program_id(0) % tiles_per_seq >= tiles_per_seq - MAX_WINDOW // rows)
        def _():
            kwin_ref[...] = k.T
            vwin_ref[...] = v.T


def _in_proj(x, pw, layer, tables, gate_w, gate_b, *, rows, batch=None, seq=None, prev_win=None):
    n = x.shape[0]
    depth = pw["w_in"].shape[0]
    dilated = batch is not None
    tbase, toff, tsign = tables
    n_base = tbase.shape[0]
    row_spec = lambda w: pl.BlockSpec((rows, w), lambda i: (i, 0))
    lspec = lambda *shape: _layer_spec(shape, layer)
    flat = lambda dt: jax.ShapeDtypeStruct((n, 512), dt)
    scratch = []
    per_b = None
    in_specs = [row_spec(D_MODEL), lspec(1, D_MODEL), lspec(D_MODEL, 5 * 512),
                lspec(1, D_A), lspec(1, D_B), lspec(1, D_B),
                _const_spec((D_B // 2, D_B // 2)),
                pl.BlockSpec((None, 2, LANES), lambda i: (i % n_base, 0, 0)),
                _const_spec((2, rows, LANES)), _const_spec((2, LANES)),
                lspec(N_SLABS, CHUNK, 2 * CHUNK), lspec(CHUNK, D_A), lspec(1, D_A)]
    args = [x, pw["g_attn"], pw["w_in"], pw["g_v"], pw["g_q"], pw["g_k"], pw["hsum"], tbase, toff,
            tsign, gate_w, gate_b, pw["g_oa"]]
    aliases = {}
    if dilated:
        per_b = seq // rows
        split_shape = lambda d, dt: jax.ShapeDtypeStruct((batch, d, seq // d, D_B), dt)
        split_spec = lambda d: pl.BlockSpec((None, d, rows // d, D_B),
                                            lambda i: (i // per_b, 0, i % per_b, 0))
        kv_shape = [flat(BF16), split_shape(4, BF16), split_shape(16, BF16)]
        kv_spec = [row_spec(512), split_spec(4), split_spec(16)]
        first_win = per_b - MAX_WINDOW // rows
        win_shape = jax.ShapeDtypeStruct((depth, batch, D_B, MAX_WINDOW), F32)
        win_spec = pl.BlockSpec((None, None, D_B, rows),
                                lambda i: (layer, i // per_b, 0, jnp.maximum(i % per_b - first_win, 0)))
        out_shape = [flat(BF16), split_shape(16, F32)] + kv_shape * 2 + [win_shape] * 2
        out_specs = [row_spec(512), split_spec(16)] + kv_spec * 2 + [win_spec] * 2
        scratch = [pltpu.VMEM((N_SLABS, rows, LANES), F32)] * 2
        if prev_win is not None:
            in_specs += [pl.BlockSpec(memory_space=pl.ANY)] * 2
            aliases = {len(args): len(out_shape) - 2, len(args) + 1: len(out_shape) - 1}
            args += list(prev_win)
    else:
        t_shape = jax.ShapeDtypeStruct((512, n), F32)
        t_spec = pl.BlockSpec((512, rows), lambda i: (0, i))
        out_shape = [flat(BF16), flat(BF16), t_shape, t_shape, flat(F32)]
        out_specs = [row_spec(512), row_spec(512), t_spec, t_spec, row_spec(512)]
    return pl.pallas_call(
        functools.partial(_in_proj_body, rows=rows, tiles_per_seq=per_b, n_in=len(args)),
        grid=(n // rows,),
        in_specs=in_specs,
        out_specs=out_specs,
        out_shape=out_shape,
        scratch_shapes=scratch,
        input_output_aliases=aliases,
        compiler_params=_params(1),
        name="in_proj",
    )(*args)


def _band_block(q, kp, kc, vp, vc, bias):
    lane_lo = lax.broadcasted_iota(jnp.int32, (BAND, LANES), 1) < HEAD_DIM
    head_mask = [(lax.broadcasted_iota(jnp.int32, (1, LANES), 1) // HEAD_DIM == hh)
                 .astype(F32).astype(BF16) for hh in range(2)]
    ones = jnp.ones((2 * BAND, LANES), BF16)
    pick = lambda t: jnp.where(lane_lo, t[:BAND], t[BAND:])
    out = []
    for p in range(N_SLABS):
        qp = q[:, _slab(p)]
        kk = jnp.concatenate([kp[:, _slab(p)], kc[:, _slab(p)]], axis=0)
        vv = jnp.concatenate([vp[:, _slab(p)], vc[:, _slab(p)]], axis=0)
        qq = jnp.concatenate([qp * head_mask[0], qp * head_mask[1]], axis=0)
        s = _dot_nt(qq, kk) + bias
        m = jnp.max(s, axis=-1, keepdims=True)
        pr = jnp.exp(s - m).astype(BF16)
        a = _dot(pr, jnp.concatenate([vv, ones], axis=1))
        l = pick(a[:, LANES:])
        out.append((pick(a[:, :LANES]) * (1.0 / l), pick(m) + jnp.log(l)))
    return out


def _attn_prompt_body(q1, k1c, k1p, v1c, v1p, q4, k4c, k4p, v4c, v4p, q16, k16c, k16p, v16c, v16p,
                      bias_ref, gob_ref, out_ref, o1, s1, o4, s4, o16, s16):
    j = pl.program_id(1)
    g = pl.program_id(2)
    n_groups = pl.num_programs(2)

    def run(branch, q, kc, kp, vc, vp, first, refs, runs):
        bias = bias_ref[branch, first] if isinstance(first, int) else \
            bias_ref[branch, first.astype(jnp.int32)]
        res = _band_block(q.reshape(BAND, D_B).astype(BF16), kp, kc, vp, vc, bias)
        n = BAND // len(runs)
        for p in range(N_SLABS):
            for ref, val in zip(refs, res[p]):
                for i, row0 in enumerate(runs):
                    ref[p, pl.ds(pl.multiple_of(row0, n), n), :] = val[i * n:(i + 1) * n]

    for i in range(SUB):
        c = SUB * g + i
        blk = slice(i * BAND, (i + 1) * BAND)
        if i == 0:
            kp, vp, first = k1p[...], v1p[...], (n_groups * j + g) == 0
        else:
            before = slice((i - 1) * BAND, i * BAND)
            kp, vp, first = k1c[before, :], v1c[before, :], 0
        run(0, q1[:, 8 * i:8 * (i + 1), :], k1c[blk, :], kp, v1c[blk, :], vp, first, (o1, s1),
            [BAND * r + 8 * c for r in range(16)])
        run(1, q4[:, i], k4c[i], k4p[i], v4c[i], v4p[i], (n_groups * j + g) == 0, (o4, s4),
            [BAND * (4 * cc + i) + 32 * g for cc in range(4)])
        run(2, q16[i], k16c[i], k16p[i], v16c[i], v16p[i], j == 0, (o16, s16), [BAND * c])

    @pl.when(g == n_groups - 1)
    def _finalize():
        parts = [(o1, s1), (o4, s4), (o16, s16)]
        for r in range(16):
            rows = pl.ds(r * BAND, BAND)
            slabs = []
            for p in range(N_SLABS):
                lse = [s_ref[p, rows, :] for (_, s_ref) in parts]
                top = jnp.maximum(jnp.maximum(lse[0], lse[1]), lse[2])
                num = den = None
                for e, (o_ref, _) in zip(lse, parts):
                    w = jnp.exp(e - top)
                    t = o_ref[p, rows, :] * w
                    num, den = (t, w) if num is None else (num + t, den + w)
                slabs.append(num / den)
            y = _rms(jnp.concatenate(slabs, axis=1), gob_ref[...])
            for p in range(N_SLABS):
                o16[p, rows, :] = y[:, _slab(p)]
        for p in range(N_SLABS):
            for r in range(16):
                o1[p, pl.ds(r, BAND, stride=16), :] = o16[p, r * BAND:(r + 1) * BAND, :]
            out_ref[:, _slab(p)] = o1[p].astype(BF16)


def _attn_prompt(q16, kv, bias, g_ob, layer, batch, seq):
    assert SUB == 4
    n_tiles = seq // TILE
    n_groups = 16 // SUB
    span = SUB * BAND
    prev = lambda i: jnp.maximum(i - 1, 0)
    grp = lambda j, g: n_groups * j + g
    branches = [
        (pl.BlockSpec((None, 16, span // 16, D_B), lambda b, j, g: (b, 0, grp(j, g), 0)), q16,
         ((None, span, D_B), lambda b, j, g: (b, grp(j, g), 0)),
         ((None, BAND, D_B), lambda b, j, g: (b, prev(SUB * grp(j, g)), 0))),
        (pl.BlockSpec((None, 4, SUB, span // 16, D_B), lambda b, j, g: (b, 0, 0, grp(j, g), 0)),
         q16.reshape(batch, 4, 4, seq // 16, D_B),
         ((None, SUB, BAND, D_B), lambda b, j, g: (b, 0, grp(j, g), 0)),
         ((None, SUB, BAND, D_B), lambda b, j, g: (b, 0, prev(grp(j, g)), 0))),
        (pl.BlockSpec((None, SUB, BAND, D_B), lambda b, j, g: (b, g, j, 0)), q16,
         ((None, SUB, BAND, D_B), lambda b, j, g: (b, g, j, 0)),
         ((None, SUB, BAND, D_B), lambda b, j, g: (b, g, prev(j), 0))),
    ]
    in_specs, args = [], []
    for di, (q_spec, q_arr, cur, prv) in enumerate(branches):
        cs, ps = pl.BlockSpec(*cur), pl.BlockSpec(*prv)
        in_specs += [q_spec, cs, ps, cs, ps]
        kd, vd = (t[di] if di else t[di].reshape(batch, seq, D_B) for t in kv)
        args += [q_arr, kd, kd, vd, vd]
    in_specs += [_const_spec((3, 2, 2 * BAND, 2 * BAND)), _layer_spec((1, D_B), layer)]
    args += [bias, g_ob]
    out = pl.pallas_call(
        _attn_prompt_body,
        grid=(batch, n_tiles, n_groups),
        in_specs=in_specs,
        out_specs=pl.BlockSpec((None, TILE, D_B), lambda b, j, g: (b, j, 0)),
        out_shape=jax.ShapeDtypeStruct((batch, seq, D_B), BF16),
        scratch_shapes=[pltpu.VMEM((N_SLABS, TILE, LANES), F32)] * 6,
        compiler_params=_params(3),
        name="attn_prompt",
    )(*args)
    return out.reshape(batch * seq, D_B)


def _band_bias():
    kj = np.arange(2 * BAND)[None, :]
    rho = np.arange(BAND)
    out = []
    for runs in (16, 4, 1):
        per = BAND // runs
        qi = (runs * (rho % per) + rho // per)[:, None]
        ok = (kj >= qi) & (kj <= qi + BAND)
        out.append(np.stack([np.where(ok, 0.0, -np.inf), np.where(ok & (kj >= BAND), 0.0, -np.inf)]))
    return np.tile(np.stack(out), (1, 1, 2, 1)).astype(np.float32)


def _window_attention(kt_ref, vt_ref, kn_ref, vn_ref, q_ref, mult_ref, okt_ref, ovt_ref, o_ref,
                      new_shift):
    body = MAX_WINDOW - LANES
    is_new = lax.broadcasted_iota(jnp.int32, (HEAD_DIM, LANES), 1) >= LANES - T_NEW
    mult = mult_ref[...]

    def shift(src_ref, new_ref, dst_ref, h):
        x = src_ref[h]
        y = pltpu.roll(x, MAX_WINDOW - T_NEW, 1)
        tail = jnp.where(is_new, pltpu.roll(new_ref[h], new_shift, 1), y[:, body:])
        dst_ref[h, :, 0:body] = y[:, 0:body]
        dst_ref[h, :, body:MAX_WINDOW] = tail
        return jnp.concatenate([y[:, 0:body], tail, x[:, 0:LANES]], axis=1).astype(BF16)

    for h in range(kt_ref.shape[0]):
        kcat = shift(kt_ref, kn_ref, okt_ref, h)
        vcat = shift(vt_ref, vn_ref, ovt_ref, h)
        s = jnp.where(mult > 0, _dot(q_ref[h], kcat), -jnp.inf)
        p = mult * jnp.exp(s - jnp.max(s, axis=-1, keepdims=True))
        o_ref[h] = _dot_nt(p.astype(BF16), vcat) / jnp.sum(p, axis=-1, keepdims=True)


def _sample_multiplicity():
    def mult(d):
        ok = d >= 0
        return (ok & (d <= 128)).astype(np.int32) + (ok & (d % 4 == 0) & (d <= 512)) \
            + (ok & (d % 16 == 0) & (d <= 2048))
    t = (np.arange(8) % T_NEW)[:, None]
    kept = mult(MAX_WINDOW - T_NEW + t - np.arange(MAX_WINDOW)[None, :])
    old = np.arange(LANES)[None, :]
    dropped = mult(MAX_WINDOW + t - old) * (old < T_NEW)
    return np.concatenate([kept, dropped], axis=1).astype(np.float32)


def _ffn_compute(x_ref, ya_ref, yb_ref, gob_ref, wout_ref, gffn_ref, wup_ref, wdown_ref, o_ref,
                 norm_b):
    yb = yb_ref[...]
    if norm_b:
        yb = _rms(yb, gob_ref[...])
    x1 = x_ref[...] + _dot(ya_ref[...], wout_ref[0:D_A, :]) \
        + _dot(yb.astype(BF16), wout_ref[D_A:D_A + D_B, :])
    hb = _rms(x1, gffn_ref[...]).astype(BF16)
    step = D_FF // 4
    mlp = None
    for c in range(4):
        f = _dot(hb, wup_ref[:, c * step:(c + 1) * step])
        f = jnp.square(jnp.maximum(f, 0.0)).astype(BF16)
        d = _dot(f, wdown_ref[c * step:(c + 1) * step, :])
        mlp = d if mlp is None else mlp + d
    o_ref[...] = x1 + mlp


def _out_ffn_body(*refs, norm_b):
    _ffn_compute(*refs, norm_b)


def _ffn_window_body(*refs, steps_per_seq):
    n_ffn = 8
    ffn_in, (kt, vt, kn, vn, q, mult) = refs[:n_ffn], refs[n_ffn:n_ffn + 6]
    x_out, okt, ovt, o_att = refs[-4:]
    seq_idx = pl.program_id(0) // steps_per_seq
    _ffn_compute(*ffn_in, x_out, False)
    _window_attention(kt, vt, kn, vn, q, mult, okt, ovt, o_att,
                      (LANES - T_NEW * (seq_idx + 1)) % LANES)


def _ffn_specs(rows, layer):
    row_spec = lambda w: pl.BlockSpec((rows, w), lambda i: (i, 0))
    lspec = lambda *shape: _layer_spec(shape, layer)
    return [row_spec(D_MODEL), row_spec(D_A), row_spec(D_B), lspec(1, D_B),
            lspec(D_A + D_B, D_MODEL), lspec(1, D_MODEL),
            lspec(D_MODEL, D_FF), lspec(D_FF, D_MODEL)], row_spec(D_MODEL)


def _out_ffn(x, ya, yb, pw, layer, *, rows, norm_b):
    n = x.shape[0]
    in_specs, out_spec = _ffn_specs(rows, layer)
    return pl.pallas_call(
        functools.partial(_out_ffn_body, norm_b=norm_b),
        grid=(n // rows,),
        in_specs=in_specs,
        out_specs=out_spec,
        out_shape=jax.ShapeDtypeStruct((n, D_MODEL), F32),
        compiler_params=_params(1),
        name="out_ffn",
    )(x, ya, yb, pw["g_ob"], pw["w_out"], pw["g_ffn"], pw["w_up"], pw["w_down"])


def _ffn_window(x, ya, yb, pw, layer, cache_kt, cache_vt, k_new, v_new, q, mult, prev_out, *, rows):
    n = x.shape[0]
    steps = n // rows
    depth, db = cache_kt.shape[:2]
    per_seq = steps // db
    hps = N_HEADS // per_seq
    assert per_seq * db == steps and hps * per_seq == N_HEADS
    in_specs, out_spec = _ffn_specs(rows, layer)
    cache_spec = pl.BlockSpec((None, None, hps, HEAD_DIM, MAX_WINDOW),
                              lambda i: (layer, i // per_seq, i % per_seq, 0, 0))
    new_spec = pl.BlockSpec((hps, HEAD_DIM, LANES), lambda i: (i % per_seq, 0, 0))
    q_spec = pl.BlockSpec((None, hps, 8, HEAD_DIM), lambda i: (i // per_seq, i % per_seq, 0, 0))
    in_specs += [cache_spec, cache_spec, new_spec, new_spec, q_spec,
                 _const_spec((8, MAX_WINDOW + LANES))]
    args = [x, ya, yb, pw["g_ob"], pw["w_out"], pw["g_ffn"], pw["w_up"], pw["w_down"],
            cache_kt, cache_vt, k_new, v_new, q, mult]
    aliases = {}
    if prev_out is not None:
        in_specs += [pl.BlockSpec(memory_space=pl.ANY)] * 2
        aliases = {len(args): 1, len(args) + 1: 2}
        args += list(prev_out)
    win_shape = jax.ShapeDtypeStruct(cache_kt.shape, F32)
    return pl.pallas_call(
        functools.partial(_ffn_window_body, steps_per_seq=per_seq),
        grid=(steps,),
        in_specs=in_specs,
        out_specs=[out_spec, cache_spec, cache_spec, q_spec],
        out_shape=[jax.ShapeDtypeStruct((n, D_MODEL), F32), win_shape, win_shape,
                   jax.ShapeDtypeStruct((db, N_HEADS, 8, HEAD_DIM), F32)],
        input_output_aliases=aliases,
        compiler_params=_params(1),
        name="ffn_window",
    )(*args)


def _rope_tables(base_pos, offsets):
    half = ROT_DIM // 2
    inv = jnp.power(ROPE_THETA, -jnp.arange(half, dtype=F32) * (2.0 / ROT_DIM))
    lane = np.arange(LANES) % HEAD_DIM
    freq = jnp.where(lane < ROT_DIM, inv[lane % half], 0.0)
    ang = lambda pos: pos.astype(F32)[:, None] * freq[None, :]
    ab, ao = ang(base_pos), ang(offsets)
    tbase = jnp.stack([jnp.cos(ab), jnp.sin(ab)], axis=1)
    toff = jnp.stack([jnp.cos(ao), jnp.sin(ao)], axis=0)
    tsign = jnp.asarray(np.stack([-1.0 * (lane < half), 1.0 * ((lane >= half) & (lane < ROT_DIM))]), F32)
    return tbase, toff, tsign


def kernel(x_prompt, x_sample, cache_win_k, cache_win_v, attn_norm_g, w_in, v_norm_g, w_spatial, b_spatial, q_norm_g, k_norm_g, out_norm_a_g, out_norm_b_g, w_out, ffn_norm_g, w_up, w_down):
    depth = w_in.shape[0]
    batch, seq, _ = x_prompt.shape
    db, t_new, _ = x_sample.shape
    n_s = db * t_new
    assert n_s == CHUNK and t_new == T_NEW and cache_win_k.shape[2] == MAX_WINDOW
    assert seq % TILE == 0

    rows_p = 512
    tab_p = _rope_tables(jnp.arange(seq // rows_p) * rows_p, jnp.arange(rows_p))
    tab_s = _rope_tables(jnp.full((1,), PAST_LEN), jnp.arange(n_s) % t_new)

    tril = jnp.tril(jnp.ones((CHUNK, CHUNK), bool))
    hsum = jnp.kron(jnp.eye(N_HEADS // 2, dtype=F32), jnp.full((HEAD_DIM, HEAD_DIM), 1.0 / HEAD_DIM, F32)).astype(BF16)
    row = lambda g: g[:, None, :]
    per_head = lambda g: jnp.tile(g, (1, N_HEADS))
    pw = dict(g_attn=row(attn_norm_g), w_in=w_in.astype(BF16), g_v=row(v_norm_g),
              g_q=row(per_head(q_norm_g)), g_k=row(per_head(k_norm_g)), hsum=hsum,
              g_oa=row(out_norm_a_g), g_ob=row(out_norm_b_g), w_out=w_out.astype(BF16),
              g_ffn=row(ffn_norm_g), w_up=w_up.astype(BF16), w_down=w_down.astype(BF16))
    pair = lambda w: w.reshape(depth, N_SLABS, 2, CHUNK, CHUNK).transpose(0, 1, 3, 2, 4) \
        .reshape(depth, N_SLABS, CHUNK, 2 * CHUNK).astype(BF16)
    gate_w_p = pair(jnp.where(tril, w_spatial, 0))
    gate_b_p = jnp.repeat(b_spatial.transpose(0, 2, 1), HEAD_DIM, axis=2)
    t_of = np.arange(CHUNK) % t_new
    same_seq = (np.arange(CHUNK)[:, None] // t_new) == (np.arange(CHUNK)[None, :] // t_new)
    causal_s = jnp.asarray(same_seq & (t_of[:, None] >= t_of[None, :]))
    onehot = jnp.asarray(t_of[:, None] == np.arange(t_new)[None, :], F32)
    w_rep = jnp.einsum("ia,lhab,jb->lhij", onehot, w_spatial[:, :, :t_new, :t_new], onehot,
                       precision=lax.Precision.HIGHEST)
    gate_w_s = pair(jnp.where(causal_s, w_rep, 0))
    b_rep = jnp.einsum("ia,lha->lih", onehot, b_spatial[:, :, :t_new], precision=lax.Precision.HIGHEST)
    gate_b_s = jnp.repeat(b_rep, HEAD_DIM, axis=2)

    bias = jnp.asarray(_band_bias())
    mult = jnp.asarray(_sample_multiplicity())

    cache_kt = cache_win_k.transpose(0, 1, 3, 4, 2)
    cache_vt = cache_win_v.transpose(0, 1, 3, 4, 2)

    xp = x_prompt.reshape(batch * seq, D_MODEL)
    xs = x_sample.reshape(n_s, D_MODEL)
    cv_list = []
    win_p = win_s = None
    for l in range(depth):
        ya_p, q16, *kv, kwin, vwin = _in_proj(xp, pw, l, tab_p, gate_w_p, gate_b_p, rows=rows_p,
                                              batch=batch, seq=seq, prev_win=win_p)
        win_p = (kwin, vwin)
        yb_p = _attn_prompt(q16, (kv[0:3], kv[3:6]), bias, pw["g_ob"], l, batch, seq)
        ya_s, q, kt, vt, va = _in_proj(xs, pw, l, tab_s, gate_w_s, gate_b_s, rows=CHUNK)
        cv_list.append(va.reshape(db, t_new, D_A))
        q8 = jnp.tile(q.reshape(db, t_new, N_HEADS, HEAD_DIM).transpose(0, 2, 1, 3), (1, 1, 2, 1))
        heads = lambda t: t.reshape(N_HEADS, HEAD_DIM, n_s)

        xp, okt, ovt, o = _ffn_window(xp, ya_p, yb_p, pw, l, cache_kt, cache_vt, heads(kt), heads(vt),
                                      q8, mult, win_s, rows=256)
        win_s = (okt, ovt)
        yb_s = o[:, :, :t_new].transpose(0, 2, 1, 3).reshape(n_s, D_B)
        xs = _out_ffn(xs, ya_s, yb_s, pw, l, rows=CHUNK, norm_b=True)

    win_p = [w.reshape(depth, batch, N_HEADS, HEAD_DIM, MAX_WINDOW) for w in win_p]
    unt = lambda w: w.transpose(0, 1, 4, 2, 3)
    return (xp.reshape(batch, seq, D_MODEL), xs.reshape(db, t_new, D_MODEL),
            unt(win_p[0]), unt(win_p[1]), unt(win_s[0]), unt(win_s[1]),
            jnp.stack(cv_list))
```

```python
import functools

import numpy as np
import jax
import jax.numpy as jnp
from jax import lax
from jax.experimental import pallas as pl
from jax.experimental.pallas import tpu as pltpu

D_MODEL = 1024
D_A = 512
D_B = 512
HEAD_DIM = 64
N_HEADS = 8
CHUNK = 128
D_FF = 4 * D_MODEL
ROT_DIM = HEAD_DIM // 4
ROPE_THETA = 500000.0
EPS = 1e-6
ATTN_SCALE = HEAD_DIM ** -0.5
DILATIONS = (1, 4, 16)
BAND = 128
TILE = 16 * BAND
SUB = 4
MAX_WINDOW = 2048
PAST_LEN = 16384
T_NEW = 4

LANES = 128
N_SLABS = D_B // LANES
VMEM_LIMIT = 56 * 1024 * 1024

F32 = jnp.float32
BF16 = jnp.bfloat16


def _rms(x, g):
    return x * lax.rsqrt(jnp.mean(x * x, axis=-1, keepdims=True) + EPS) * g


def _gelu(x):
    c = np.float32(np.sqrt(2.0 / np.pi))
    return x * (0.5 * (1.0 + jnp.tanh(c * (x + 0.044715 * (x * x * x)))))


def _dot(a, b):
    return jnp.dot(a, b, preferred_element_type=F32)


def _dot_nt(a, b):
    return lax.dot_general(a, b, (((1,), (1,)), ((), ())), preferred_element_type=F32)


def _params(n_axes):
    return pltpu.CompilerParams(dimension_semantics=("arbitrary",) * n_axes,
                                vmem_limit_bytes=VMEM_LIMIT)


def _const_spec(shape):
    return pl.BlockSpec(shape, lambda *_: (0,) * len(shape), pipeline_mode=pl.Buffered(1))


def _layer_spec(shape, layer):
    return pl.BlockSpec((None,) + shape, lambda *_: (layer,) + (0,) * len(shape),
                        pipeline_mode=pl.Buffered(1))


def _slab(p):
    return slice(p * LANES, (p + 1) * LANES)


def _in_proj_body(x_ref, gattn_ref, win_ref, gv_ref, gq_ref, gk_ref, hsum_ref, tbase_ref, toff_ref,
                  tsign_ref, wg_ref, bg_ref, goa_ref, *outs, rows, tiles_per_seq, n_in):
    dilated = tiles_per_seq is not None
    outs = outs[n_in - 13:]
    hb = _rms(x_ref[...], gattn_ref[...]).astype(BF16)

    cb, sb = tbase_ref[0:1, :], tbase_ref[1:2, :]
    co, so = toff_ref[0], toff_ref[1]
    cos = cb * co - sb * so
    sin = sb * co + cb * so
    sin_up, sin_dn = sin * tsign_ref[0:1, :], sin * tsign_ref[1:2, :]

    def proj(i):
        return _dot(hb, win_ref[:, i * 512:(i + 1) * 512])

    def head_norm_rope(z, g_ref):
        zz = (z * z).astype(BF16)
        half = D_B // 2
        ms = jnp.concatenate([_dot(zz[:, :half], hsum_ref[...]), _dot(zz[:, half:], hsum_ref[...])],
                             axis=1)
        zn = z * lax.rsqrt(ms + EPS) * g_ref[...]
        slabs = []
        for p in range(N_SLABS):
            s = zn[:, _slab(p)]
            slabs.append(s * cos + pltpu.roll(s, LANES - ROT_DIM // 2, 1) * sin_up
                         + pltpu.roll(s, ROT_DIM // 2, 1) * sin_dn)
        return jnp.concatenate(slabs, axis=1)

    if dilated:
        ya_ref, q16, k1, k4, k16, v1, v4, v16, kwin_ref, vwin_ref, scr1, scr4 = outs

        def split(val, r4_ref, r16_ref):
            for p in range(N_SLABS):
                scr1[p] = val[:, _slab(p)]
            n4, n16 = rows // 4, rows // 16
            for p in range(N_SLABS):
                for r in range(4):
                    part = scr1[p, pl.ds(r, n4, stride=4), :]
                    scr4[p, r * n4:(r + 1) * n4, :] = part
                    if r4_ref is not None:
                        r4_ref[r, :, _slab(p)] = part.astype(BF16)
                for r in range(4):
                    for c in range(4):
                        part = scr4[p, pl.ds(r * n4 + c, n16, stride=4), :]
                        r16_ref[r + 4 * c, :, _slab(p)] = part.astype(r16_ref.dtype)
    else:
        ya_ref, q1, kt_ref, vt_ref, va_ref = outs

    zu = proj(0)
    zq = proj(2)
    u = _gelu(zu)
    q = head_norm_rope(zq, gq_ref) * ATTN_SCALE
    zva = proj(1)
    if dilated:
        split(q, None, q16)
    else:
        q1[...] = q.astype(BF16)
    zk = proj(3)
    va = _rms(_gelu(zva), gv_ref[...])
    k = head_norm_rope(zk, gk_ref)
    v = proj(4)
    if dilated:
        k1[...] = k.astype(BF16)
        split(k, k4, k16)
        v1[...] = v.astype(BF16)
        split(v, v4, v16)
    else:
        va_ref[...] = va
        kt_ref[...] = k.T
        vt_ref[...] = v.T

    lane_lo = lax.broadcasted_iota(jnp.int32, (CHUNK, LANES), 1) < HEAD_DIM
    n_chunks = rows // CHUNK
    side = min(2, n_chunks)
    ys = [[None] * N_SLABS for _ in range(n_chunks)]
    for c0 in range(0, n_chunks, side):
        for p in range(N_SLABS):
            parts = [va[c * CHUNK:(c + 1) * CHUNK, _slab(p)] for c in range(c0, c0 + side)]
            lo = jnp.concatenate([jnp.where(lane_lo, t, 0.0) for t in parts], axis=1)
            hi = jnp.concatenate([jnp.where(lane_lo, 0.0, t) for t in parts], axis=1)
            s = _dot(wg_ref[p], jnp.concatenate([lo, hi], axis=0).astype(BF16))
            for i in range(side):
                c = c0 + i
                ys[c][p] = u[c * CHUNK:(c + 1) * CHUNK, _slab(p)] * (s[:, _slab(i)] + bg_ref[:, _slab(p)])
    for c in range(n_chunks):
        ya_ref[c * CHUNK:(c + 1) * CHUNK, :] = _rms(jnp.concatenate(ys[c], axis=1), goa_ref[...]).astype(BF16)

    if dilated:
        @pl.when(pl.program_id(0) % tiles_per_seq >= tiles_per_seq - MAX_WINDOW // rows)
        def _():
            kwin_ref[...] = k.T
            vwin_ref[...] = v.T


def _in_proj(x, pw, layer, tables, gate_w, gate_b, *, rows, batch=None, seq=None, prev_win=None):
    n = x.shape[0]
    depth = pw["w_in"].shape[0]
    dilated = batch is not None
    tbase, toff, tsign = tables
    n_base = tbase.shape[0]
    row_spec = lambda w: pl.BlockSpec((rows, w), lambda i: (i, 0))
    lspec = lambda *shape: _layer_spec(shape, layer)
    flat = lambda dt: jax.ShapeDtypeStruct((n, 512), dt)
    scratch = []
    per_b = None
    in_specs = [row_spec(D_MODEL), lspec(1, D_MODEL), lspec(D_MODEL, 5 * 512),
                lspec(1, D_A), lspec(1, D_B), lspec(1, D_B),
                _const_spec((D_B // 2, D_B // 2)),
                pl.BlockSpec((None, 2, LANES), lambda i: (i % n_base, 0, 0)),
                _const_spec((2, rows, LANES)), _const_spec((2, LANES)),
                lspec(N_SLABS, CHUNK, 2 * CHUNK), lspec(CHUNK, D_A), lspec(1, D_A)]
    args = [x, pw["g_attn"], pw["w_in"], pw["g_v"], pw["g_q"], pw["g_k"], pw["hsum"], tbase, toff,
            tsign, gate_w, gate_b, pw["g_oa"]]
    aliases = {}
    if dilated:
        per_b = seq // rows
        split_shape = lambda d, dt: jax.ShapeDtypeStruct((batch, d, seq // d, D_B), dt)
        split_spec = lambda d: pl.BlockSpec((None, d, rows // d, D_B),
                                            lambda i: (i // per_b, 0, i % per_b, 0))
        kv_shape = [flat(BF16), split_shape(4, BF16), split_shape(16, BF16)]
        kv_spec = [row_spec(512), split_spec(4), split_spec(16)]
        first_win = per_b - MAX_WINDOW // rows
        win_shape = jax.ShapeDtypeStruct((depth, batch, D_B, MAX_WINDOW), F32)
        win_spec = pl.BlockSpec((None, None, D_B, rows),
                                lambda i: (layer, i // per_b, 0, jnp.maximum(i % per_b - first_win, 0)))
        out_shape = [flat(BF16), split_shape(16, F32)] + kv_shape * 2 + [win_shape] * 2
        out_specs = [row_spec(512), split_spec(16)] + kv_spec * 2 + [win_spec] * 2
        scratch = [pltpu.VMEM((N_SLABS, rows, LANES), F32)] * 2
        if prev_win is not None:
            in_specs += [pl.BlockSpec(memory_space=pl.ANY)] * 2
            aliases = {len(args): len(out_shape) - 2, len(args) + 1: len(out_shape) - 1}
            args += list(prev_win)
    else:
        t_shape = jax.ShapeDtypeStruct((512, n), F32)
        t_spec = pl.BlockSpec((512, rows), lambda i: (0, i))
        out_shape = [flat(BF16), flat(BF16), t_shape, t_shape, flat(F32)]
        out_specs = [row_spec(512), row_spec(512), t_spec, t_spec, row_spec(512)]
    return pl.pallas_call(
        functools.partial(_in_proj_body, rows=rows, tiles_per_seq=per_b, n_in=len(args)),
        grid=(n // rows,),
        in_specs=in_specs,
        out_specs=out_specs,
        out_shape=out_shape,
        scratch_shapes=scratch,
        input_output_aliases=aliases,
        compiler_params=_params(1),
        name="in_proj",
    )(*args)


def _band_block(q, kp, kc, vp, vc, bias):
    lane_lo = lax.broadcasted_iota(jnp.int32, (BAND, LANES), 1) < HEAD_DIM
    head_mask = [(lax.broadcasted_iota(jnp.int32, (1, LANES), 1) // HEAD_DIM == hh)
                 .astype(F32).astype(BF16) for hh in range(2)]
    ones = jnp.ones((2 * BAND, LANES), BF16)
    pick = lambda t: jnp.where(lane_lo, t[:BAND], t[BAND:])
    out = []
    for p in range(N_SLABS):
        qp = q[:, _slab(p)]
        kk = jnp.concatenate([kp[:, _slab(p)], kc[:, _slab(p)]], axis=0)
        vv = jnp.concatenate([vp[:, _slab(p)], vc[:, _slab(p)]], axis=0)
        qq = jnp.concatenate([qp * head_mask[0], qp * head_mask[1]], axis=0)
        s = _dot_nt(qq, kk) + bias
        m = jnp.max(s, axis=-1, keepdims=True)
        pr = jnp.exp(s - m).astype(BF16)
        a = _dot(pr, jnp.concatenate([vv, ones], axis=1))
        l = pick(a[:, LANES:])
        out.append((pick(a[:, :LANES]) * (1.0 / l), pick(m) + jnp.log(l)))
    return out


def _attn_prompt_body(q1, k1c, k1p, v1c, v1p, q4, k4c, k4p, v4c, v4p, q16, k16c, k16p, v16c, v16p,
                      bias_ref, gob_ref, out_ref, o1, s1, o4, s4, o16, s16):
    j = pl.program_id(1)
    g = pl.program_id(2)
    n_groups = pl.num_programs(2)

    def run(branch, q, kc, kp, vc, vp, first, refs, runs):
        bias = bias_ref[branch, first] if isinstance(first, int) else \
            bias_ref[branch, first.astype(jnp.int32)]
        res = _band_block(q.reshape(BAND, D_B).astype(BF16), kp, kc, vp, vc, bias)
        n = BAND // len(runs)
        for p in range(N_SLABS):
            for ref, val in zip(refs, res[p]):
                for i, row0 in enumerate(runs):
                    ref[p, pl.ds(pl.multiple_of(row0, n), n), :] = val[i * n:(i + 1) * n]

    for i in range(SUB):
        c = SUB * g + i
        blk = slice(i * BAND, (i + 1) * BAND)
        if i == 0:
            kp, vp, first = k1p[...], v1p[...], (n_groups * j + g) == 0
        else:
            before = slice((i - 1) * BAND, i * BAND)
            kp, vp, first = k1c[before, :], v1c[before, :], 0
        run(0, q1[:, 8 * i:8 * (i + 1), :], k1c[blk, :], kp, v1c[blk, :], vp, first, (o1, s1),
            [BAND * r + 8 * c for r in range(16)])
        run(1, q4[:, i], k4c[i], k4p[i], v4c[i], v4p[i], (n_groups * j + g) == 0, (o4, s4),
            [BAND * (4 * cc + i) + 32 * g for cc in range(4)])
        run(2, q16[i], k16c[i], k16p[i], v16c[i], v16p[i], j == 0, (o16, s16), [BAND * c])

    @pl.when(g == n_groups - 1)
    def _finalize():
        parts = [(o1, s1), (o4, s4), (o16, s16)]
        for r in range(16):
            rows = pl.ds(r * BAND, BAND)
            slabs = []
            for p in range(N_SLABS):
                lse = [s_ref[p, rows, :] for (_, s_ref) in parts]
                top = jnp.maximum(jnp.maximum(lse[0], lse[1]), lse[2])
                num = den = None
                for e, (o_ref, _) in zip(lse, parts):
                    w = jnp.exp(e - top)
                    t = o_ref[p, rows, :] * w
                    num, den = (t, w) if num is None else (num + t, den + w)
                slabs.append(num / den)
            y = _rms(jnp.concatenate(slabs, axis=1), gob_ref[...])
            for p in range(N_SLABS):
                o16[p, rows, :] = y[:, _slab(p)]
        for p in range(N_SLABS):
            for r in range(16):
                o1[p, pl.ds(r, BAND, stride=16), :] = o16[p, r * BAND:(r + 1) * BAND, :]
            out_ref[:, _slab(p)] = o1[p].astype(BF16)


def _attn_prompt(q16, kv, bias, g_ob, layer, batch, seq):
    assert SUB == 4
    n_tiles = seq // TILE
    n_groups = 16 // SUB
    span = SUB * BAND
    prev = lambda i: jnp.maximum(i - 1, 0)
    grp = lambda j, g: n_groups * j + g
    branches = [
        (pl.BlockSpec((None, 16, span // 16, D_B), lambda b, j, g: (b, 0, grp(j, g), 0)), q16,
         ((None, span, D_B), lambda b, j, g: (b, grp(j, g), 0)),
         ((None, BAND, D_B), lambda b, j, g: (b, prev(SUB * grp(j, g)), 0))),
        (pl.BlockSpec((None, 4, SUB, span // 16, D_B), lambda b, j, g: (b, 0, 0, grp(j, g), 0)),
         q16.reshape(batch, 4, 4, seq // 16, D_B),
         ((None, SUB, BAND, D_B), lambda b, j, g: (b, 0, grp(j, g), 0)),
         ((None, SUB, BAND, D_B), lambda b, j, g: (b, 0, prev(grp(j, g)), 0))),
        (pl.BlockSpec((None, SUB, BAND, D_B), lambda b, j, g: (b, g, j, 0)), q16,
         ((None, SUB, BAND, D_B), lambda b, j, g: (b, g, j, 0)),
         ((None, SUB, BAND, D_B), lambda b, j, g: (b, g, prev(j), 0))),
    ]
    in_specs, args = [], []
    for di, (q_spec, q_arr, cur, prv) in enumerate(branches):
        cs, ps = pl.BlockSpec(*cur), pl.BlockSpec(*prv)
        in_specs += [q_spec, cs, ps, cs, ps]
        kd, vd = (t[di] if di else t[di].reshape(batch, seq, D_B) for t in kv)
        args += [q_arr, kd, kd, vd, vd]
    in_specs += [_const_spec((3, 2, 2 * BAND, 2 * BAND)), _layer_spec((1, D_B), layer)]
    args += [bias, g_ob]
    out = pl.pallas_call(
        _attn_prompt_body,
        grid=(batch, n_tiles, n_groups),
        in_specs=in_specs,
        out_specs=pl.BlockSpec((None, TILE, D_B), lambda b, j, g: (b, j, 0)),
        out_shape=jax.ShapeDtypeStruct((batch, seq, D_B), BF16),
        scratch_shapes=[pltpu.VMEM((N_SLABS, TILE, LANES), F32)] * 6,
        compiler_params=_params(3),
        name="attn_prompt",
    )(*args)
    return out.reshape(batch * seq, D_B)


def _band_bias():
    kj = np.arange(2 * BAND)[None, :]
    rho = np.arange(BAND)
    out = []
    for runs in (16, 4, 1):
        per = BAND // runs
        qi = (runs * (rho % per) + rho // per)[:, None]
        ok = (kj >= qi) & (kj <= qi + BAND)
        out.append(np.stack([np.where(ok, 0.0, -np.inf), np.where(ok & (kj >= BAND), 0.0, -np.inf)]))
    return np.tile(np.stack(out), (1, 1, 2, 1)).astype(np.float32)


def _window_attention(kt_ref, vt_ref, kn_ref, vn_ref, q_ref, mult_ref, okt_ref, ovt_ref, o_ref,
                      seq_idx):
    n_heads = kt_ref.shape[0]
    n_feat = n_heads * HEAD_DIM
    stack = lambda w_ref, n_ref: jnp.concatenate(
        [w_ref[...].reshape(n_feat, MAX_WINDOW), n_ref[...].reshape(n_feat, LANES)], axis=1).astype(BF16)
    state = {}

    def scores():
        own = lax.broadcasted_iota(jnp.int32, (8 * n_heads, LANES), 1) // T_NEW == seq_idx
        mult = jnp.concatenate([mult_ref[:, 0:MAX_WINDOW],
                                jnp.where(own, mult_ref[:, MAX_WINDOW:], 0.0)], axis=1)
        s = jnp.where(mult > 0, _dot(q_ref[...], stack(kt_ref, kn_ref)), -jnp.inf)
        state["p"] = mult * jnp.exp(s - jnp.max(s, axis=-1, keepdims=True))

    def output():
        p = state["p"]
        o_ref[...] = _dot_nt(p.astype(BF16), stack(vt_ref, vn_ref)) / jnp.sum(p, axis=-1, keepdims=True)

    def shift():
        body = MAX_WINDOW - LANES
        new_shift = (LANES - T_NEW * (seq_idx + 1)) % LANES
        is_new = lax.broadcasted_iota(jnp.int32, (HEAD_DIM, LANES), 1) >= LANES - T_NEW
        for src_ref, new_ref, dst_ref in ((kt_ref, kn_ref, okt_ref), (vt_ref, vn_ref, ovt_ref)):
            for h in range(n_heads):
                y = pltpu.roll(src_ref[h], MAX_WINDOW - T_NEW, 1)
                dst_ref[h, :, 0:body] = y[:, 0:body]
                dst_ref[h, :, body:MAX_WINDOW] = jnp.where(
                    is_new, pltpu.roll(new_ref[h], new_shift, 1), y[:, body:])

    return scores, output, shift


def _sample_multiplicity(n_rows):
    def mult(d):
        ok = d >= 0
        return (ok & (d <= 128)).astype(np.int32) + (ok & (d % 4 == 0) & (d <= 512)) \
            + (ok & (d % 16 == 0) & (d <= 2048))
    t = (np.arange(n_rows) % 8 % T_NEW)[:, None]
    old = mult(MAX_WINDOW + t - np.arange(MAX_WINDOW)[None, :])
    new = mult(t - (np.arange(LANES) % T_NEW)[None, :])
    return np.concatenate([old, new], axis=1).astype(np.float32)


def _ffn_compute(x_ref, ya_ref, yb_ref, gob_ref, wout_ref, gffn_ref, wup_ref, wdown_ref, o_ref,
                 norm_b, between=lambda stage: None):
    yb = yb_ref[...]
    if norm_b:
        yb = _rms(yb, gob_ref[...])
    x1 = x_ref[...] + _dot(ya_ref[...], wout_ref[0:D_A, :]) \
        + _dot(yb.astype(BF16), wout_ref[D_A:D_A + D_B, :])
    hb = _rms(x1, gffn_ref[...]).astype(BF16)
    between(0)
    step = D_FF // 4
    mlp = None
    for c in range(4):
        f = _dot(hb, wup_ref[:, c * step:(c + 1) * step])
        f = jnp.square(jnp.maximum(f, 0.0)).astype(BF16)
        d = _dot(f, wdown_ref[c * step:(c + 1) * step, :])
        mlp = d if mlp is None else mlp + d
        between(c + 1)
    o_ref[...] = x1 + mlp


def _out_ffn_body(*refs, norm_b):
    _ffn_compute(*refs, norm_b)


def _ffn_window_body(*refs, steps_per_seq):
    n_ffn = 8
    ffn_in, (kt, vt, kn, vn, q, mult) = refs[:n_ffn], refs[n_ffn:n_ffn + 6]
    x_out, okt, ovt, o_att = refs[-4:]
    seq_idx = pl.program_id(0) // steps_per_seq
    scores, output, shift = _window_attention(kt, vt, kn, vn, q, mult, okt, ovt, o_att, seq_idx)
    stages = {0: scores, 2: output, 4: shift}
    _ffn_compute(*ffn_in, x_out, False, between=lambda stage: stages.get(stage, lambda: None)())


def _ffn_specs(rows, layer):
    row_spec = lambda w: pl.BlockSpec((rows, w), lambda i: (i, 0))
    lspec = lambda *shape: _layer_spec(shape, layer)
    return [row_spec(D_MODEL), row_spec(D_A), row_spec(D_B), lspec(1, D_B),
            lspec(D_A + D_B, D_MODEL), lspec(1, D_MODEL),
            lspec(D_MODEL, D_FF), lspec(D_FF, D_MODEL)], row_spec(D_MODEL)


def _out_ffn(x, ya, yb, pw, layer, *, rows, norm_b):
    n = x.shape[0]
    in_specs, out_spec = _ffn_specs(rows, layer)
    return pl.pallas_call(
        functools.partial(_out_ffn_body, norm_b=norm_b),
        grid=(n // rows,),
        in_specs=in_specs,
        out_specs=out_spec,
        out_shape=jax.ShapeDtypeStruct((n, D_MODEL), F32),
        compiler_params=_params(1),
        name="out_ffn",
    )(x, ya, yb, pw["g_ob"], pw["w_out"], pw["g_ffn"], pw["w_up"], pw["w_down"])


def _ffn_window(x, ya, yb, pw, layer, cache_kt, cache_vt, k_new, v_new, q, mult, prev_out, *, rows):
    n = x.shape[0]
    steps = n // rows
    depth, db = cache_kt.shape[:2]
    per_seq = steps // db
    hps = N_HEADS // per_seq
    assert per_seq * db == steps and hps * per_seq == N_HEADS
    assert q.shape == (db, per_seq, 8 * hps, HEAD_DIM * hps)
    in_specs, out_spec = _ffn_specs(rows, layer)
    cache_spec = pl.BlockSpec((None, None, hps, HEAD_DIM, MAX_WINDOW),
                              lambda i: (layer, i // per_seq, i % per_seq, 0, 0))
    new_spec = pl.BlockSpec((hps, HEAD_DIM, LANES), lambda i: (i % per_seq, 0, 0))
    q_spec = pl.BlockSpec((None, None) + q.shape[2:], lambda i: (i // per_seq, i % per_seq, 0, 0))
    in_specs += [cache_spec, cache_spec, new_spec, new_spec, q_spec,
                 _const_spec((8 * hps, MAX_WINDOW + LANES))]
    args = [x, ya, yb, pw["g_ob"], pw["w_out"], pw["g_ffn"], pw["w_up"], pw["w_down"],
            cache_kt, cache_vt, k_new, v_new, q, mult]
    aliases = {}
    if prev_out is not None:
        in_specs += [pl.BlockSpec(memory_space=pl.ANY)] * 2
        aliases = {len(args): 1, len(args) + 1: 2}
        args += list(prev_out)
    win_shape = jax.ShapeDtypeStruct(cache_kt.shape, F32)
    return pl.pallas_call(
        functools.partial(_ffn_window_body, steps_per_seq=per_seq),
        grid=(steps,),
        in_specs=in_specs,
        out_specs=[out_spec, cache_spec, cache_spec, q_spec],
        out_shape=[jax.ShapeDtypeStruct((n, D_MODEL), F32), win_shape, win_shape,
                   jax.ShapeDtypeStruct(q.shape, F32)],
        input_output_aliases=aliases,
        compiler_params=_params(1),
        name="ffn_window",
    )(*args)


def _rope_tables(base_pos, offsets):
    half = ROT_DIM // 2
    inv = jnp.power(ROPE_THETA, -jnp.arange(half, dtype=F32) * (2.0 / ROT_DIM))
    lane = np.arange(LANES) % HEAD_DIM
    freq = jnp.where(lane < ROT_DIM, inv[lane % half], 0.0)
    ang = lambda pos: pos.astype(F32)[:, None] * freq[None, :]
    ab, ao = ang(base_pos), ang(offsets)
    tbase = jnp.stack([jnp.cos(ab), jnp.sin(ab)], axis=1)
    toff = jnp.stack([jnp.cos(ao), jnp.sin(ao)], axis=0)
    tsign = jnp.asarray(np.stack([-1.0 * (lane < half), 1.0 * ((lane >= half) & (lane < ROT_DIM))]), F32)
    return tbase, toff, tsign


def kernel(x_prompt, x_sample, cache_win_k, cache_win_v, attn_norm_g, w_in, v_norm_g, w_spatial, b_spatial, q_norm_g, k_norm_g, out_norm_a_g, out_norm_b_g, w_out, ffn_norm_g, w_up, w_down):
    depth = w_in.shape[0]
    batch, seq, _ = x_prompt.shape
    db, t_new, _ = x_sample.shape
    n_s = db * t_new
    assert n_s == CHUNK and t_new == T_NEW and cache_win_k.shape[2] == MAX_WINDOW
    assert seq % TILE == 0

    rows_p = 512
    tab_p = _rope_tables(jnp.arange(seq // rows_p) * rows_p, jnp.arange(rows_p))
    tab_s = _rope_tables(jnp.full((1,), PAST_LEN), jnp.arange(n_s) % t_new)

    tril = jnp.tril(jnp.ones((CHUNK, CHUNK), bool))
    hsum = jnp.kron(jnp.eye(N_HEADS // 2, dtype=F32), jnp.full((HEAD_DIM, HEAD_DIM), 1.0 / HEAD_DIM, F32)).astype(BF16)
    row = lambda g: g[:, None, :]
    per_head = lambda g: jnp.tile(g, (1, N_HEADS))
    pw = dict(g_attn=row(attn_norm_g), w_in=w_in.astype(BF16), g_v=row(v_norm_g),
              g_q=row(per_head(q_norm_g)), g_k=row(per_head(k_norm_g)), hsum=hsum,
              g_oa=row(out_norm_a_g), g_ob=row(out_norm_b_g), w_out=w_out.astype(BF16),
              g_ffn=row(ffn_norm_g), w_up=w_up.astype(BF16), w_down=w_down.astype(BF16))
    pair = lambda w: w.reshape(depth, N_SLABS, 2, CHUNK, CHUNK).transpose(0, 1, 3, 2, 4) \
        .reshape(depth, N_SLABS, CHUNK, 2 * CHUNK).astype(BF16)
    gate_w_p = pair(jnp.where(tril, w_spatial, 0))
    gate_b_p = jnp.repeat(b_spatial.transpose(0, 2, 1), HEAD_DIM, axis=2)
    t_of = np.arange(CHUNK) % t_new
    same_seq = (np.arange(CHUNK)[:, None] // t_new) == (np.arange(CHUNK)[None, :] // t_new)
    causal_s = jnp.asarray(same_seq & (t_of[:, None] >= t_of[None, :]))
    onehot = jnp.asarray(t_of[:, None] == np.arange(t_new)[None, :], F32)
    w_rep = jnp.einsum("ia,lhab,jb->lhij", onehot, w_spatial[:, :, :t_new, :t_new], onehot,
                       precision=lax.Precision.HIGHEST)
    gate_w_s = pair(jnp.where(causal_s, w_rep, 0))
    b_rep = jnp.einsum("ia,lha->lih", onehot, b_spatial[:, :, :t_new], precision=lax.Precision.HIGHEST)
    gate_b_s = jnp.repeat(b_rep, HEAD_DIM, axis=2)

    bias = jnp.asarray(_band_bias())
    rows_ffn = 256
    per_seq = batch * seq // rows_ffn // db
    hps = N_HEADS // per_seq
    mult = jnp.asarray(_sample_multiplicity(8 * hps))
    eye_h = jnp.eye(hps, dtype=F32)[None, None, :, None, :, None]

    cache_kt = cache_win_k.transpose(0, 1, 3, 4, 2)
    cache_vt = cache_win_v.transpose(0, 1, 3, 4, 2)

    xp = x_prompt.reshape(batch * seq, D_MODEL)
    xs = x_sample.reshape(n_s, D_MODEL)
    cv_list = []
    win_p = win_s = None
    for l in range(depth):
        ya_p, q16, *kv, kwin, vwin = _in_proj(xp, pw, l, tab_p, gate_w_p, gate_b_p, rows=rows_p,
                                              batch=batch, seq=seq, prev_win=win_p)
        win_p = (kwin, vwin)
        yb_p = _attn_prompt(q16, (kv[0:3], kv[3:6]), bias, pw["g_ob"], l, batch, seq)
        ya_s, q, kt, vt, va = _in_proj(xs, pw, l, tab_s, gate_w_s, gate_b_s, rows=CHUNK)
        cv_list.append(va.reshape(db, t_new, D_A))
        q8 = jnp.tile(q.reshape(db, t_new, per_seq, hps, HEAD_DIM).transpose(0, 2, 3, 1, 4), (1, 1, 1, 2, 1))
        q_blk = (q8[:, :, :, :, None, :] * eye_h.astype(BF16)).reshape(db, per_seq, 8 * hps, hps * HEAD_DIM)
        heads = lambda t: t.reshape(N_HEADS, HEAD_DIM, n_s)

        xp, okt, ovt, o = _ffn_window(xp, ya_p, yb_p, pw, l, cache_kt, cache_vt, heads(kt), heads(vt),
                                      q_blk, mult, win_s, rows=rows_ffn)
        win_s = (okt, ovt)
        o = (o.reshape(db, per_seq, hps, 8, hps, HEAD_DIM) * eye_h).sum(axis=4)
        yb_s = o[:, :, :, :t_new].transpose(0, 3, 1, 2, 4).reshape(n_s, D_B)
        xs = _out_ffn(xs, ya_s, yb_s, pw, l, rows=CHUNK, norm_b=True)

    win_p = [w.reshape(depth, batch, N_HEADS, HEAD_DIM, MAX_WINDOW) for w in win_p]
    unt = lambda w: w.transpose(0, 1, 4, 2, 3)
    return (xp.reshape(batch, seq, D_MODEL), xs.reshape(db, t_new, D_MODEL),
            unt(win_p[0]), unt(win_p[1]), unt(win_s[0]), unt(win_s[1]),
            jnp.stack(cv_list))
```

```python
import functools

import numpy as np
import jax
import jax.numpy as jnp
from jax import lax
from jax.experimental import pallas as pl
from jax.experimental.pallas import tpu as pltpu

D_MODEL = 1024
D_A = 512
D_B = 512
HEAD_DIM = 64
N_HEADS = 8
CHUNK = 128
D_FF = 4 * D_MODEL
ROT_DIM = HEAD_DIM // 4
ROPE_THETA = 500000.0
EPS = 1e-6
ATTN_SCALE = HEAD_DIM ** -0.5
LOG2_E = float(np.log2(np.e))
DILATIONS = (1, 4, 16)
BAND = 128
TILE = 16 * BAND
SUB = 4
MAX_WINDOW = 2048
PAST_LEN = 16384
T_NEW = 4

LANES = 128
N_SLABS = D_B // LANES
VMEM_LIMIT = 56 * 1024 * 1024

F32 = jnp.float32
BF16 = jnp.bfloat16


def _rms(x, g):
    return x * lax.rsqrt(jnp.mean(x * x, axis=-1, keepdims=True) + EPS) * g


def _gelu(x):
    c = np.float32(np.sqrt(2.0 / np.pi))
    return x * (0.5 * (1.0 + jnp.tanh(c * (x + 0.044715 * (x * x * x)))))


def _dot(a, b):
    return jnp.dot(a, b, preferred_element_type=F32)


def _dot_nt(a, b):
    return lax.dot_general(a, b, (((1,), (1,)), ((), ())), preferred_element_type=F32)


def _params(n_axes):
    return pltpu.CompilerParams(dimension_semantics=("arbitrary",) * n_axes,
                                vmem_limit_bytes=VMEM_LIMIT)


def _const_spec(shape):
    return pl.BlockSpec(shape, lambda *_: (0,) * len(shape), pipeline_mode=pl.Buffered(1))


def _layer_spec(shape, layer):
    return pl.BlockSpec((None,) + shape, lambda *_: (layer,) + (0,) * len(shape),
                        pipeline_mode=pl.Buffered(1))


def _slab(p):
    return slice(p * LANES, (p + 1) * LANES)


def _in_proj_body(x_ref, gattn_ref, win_ref, gv_ref, gq_ref, gk_ref, hsum_ref, tbase_ref, toff_ref,
                  tsign_ref, wg_ref, bg_ref, goa_ref, *outs, rows, tiles_per_seq, n_in):
    dilated = tiles_per_seq is not None
    outs = outs[n_in - 13:]
    hb = _rms(x_ref[...], gattn_ref[...]).astype(BF16)

    cb, sb = tbase_ref[0:1, :], tbase_ref[1:2, :]
    co, so = toff_ref[0], toff_ref[1]
    cos = cb * co - sb * so
    sin = sb * co + cb * so
    sin_up, sin_dn = sin * tsign_ref[0:1, :], sin * tsign_ref[1:2, :]

    def proj(i):
        return _dot(hb, win_ref[:, i * 512:(i + 1) * 512])

    def head_norm_rope(z, g_ref):
        zz = (z * z).astype(BF16)
        half = D_B // 2
        ms = jnp.concatenate([_dot(zz[:, :half], hsum_ref[...]), _dot(zz[:, half:], hsum_ref[...])],
                             axis=1)
        zn = z * lax.rsqrt(ms + EPS) * g_ref[...]
        slabs = []
        for p in range(N_SLABS):
            s = zn[:, _slab(p)]
            slabs.append(s * cos + pltpu.roll(s, LANES - ROT_DIM // 2, 1) * sin_up
                         + pltpu.roll(s, ROT_DIM // 2, 1) * sin_dn)
        return jnp.concatenate(slabs, axis=1)

    if dilated:
        ya_ref, q16, k1, k4, k16, v1, v4, v16, kwin_ref, vwin_ref, scr1, scr4 = outs

        def split(val, r4_ref, r16_ref):
            for p in range(N_SLABS):
                scr1[p] = val[:, _slab(p)]
            n4, n16 = rows // 4, rows // 16
            for p in range(N_SLABS):
                for r in range(4):
                    part = scr1[p, pl.ds(r, n4, stride=4), :]
                    scr4[p, r * n4:(r + 1) * n4, :] = part
                    if r4_ref is not None:
                        r4_ref[r, :, _slab(p)] = part.astype(BF16)
                for r in range(4):
                    for c in range(4):
                        part = scr4[p, pl.ds(r * n4 + c, n16, stride=4), :]
                        r16_ref[r + 4 * c, :, _slab(p)] = part.astype(r16_ref.dtype)
    else:
        ya_ref, q1, kt_ref, vt_ref, va_ref = outs

    zu = proj(0)
    zq = proj(2)
    u = _gelu(zu)
    q = head_norm_rope(zq, gq_ref) * ATTN_SCALE
    zva = proj(1)
    if dilated:
        split(q, None, q16)
    else:
        q1[...] = q.astype(BF16)
    zk = proj(3)
    va = _rms(_gelu(zva), gv_ref[...])
    k = head_norm_rope(zk, gk_ref)
    v = proj(4)
    if dilated:
        k1[...] = k.astype(BF16)
        split(k, k4, k16)
        v1[...] = v.astype(BF16)
        split(v, v4, v16)
    else:
        va_ref[...] = va
        kt_ref[...] = k.T
        vt_ref[...] = v.T

    lane_lo = lax.broadcasted_iota(jnp.int32, (CHUNK, LANES), 1) < HEAD_DIM
    n_chunks = rows // CHUNK
    side = min(2, n_chunks)
    ys = [[None] * N_SLABS for _ in range(n_chunks)]
    for c0 in range(0, n_chunks, side):
        for p in range(N_SLABS):
            parts = [va[c * CHUNK:(c + 1) * CHUNK, _slab(p)] for c in range(c0, c0 + side)]
            lo = jnp.concatenate([jnp.where(lane_lo, t, 0.0) for t in parts], axis=1)
            hi = jnp.concatenate([jnp.where(lane_lo, 0.0, t) for t in parts], axis=1)
            s = _dot(wg_ref[p], jnp.concatenate([lo, hi], axis=0).astype(BF16))
            for i in range(side):
                c = c0 + i
                ys[c][p] = u[c * CHUNK:(c + 1) * CHUNK, _slab(p)] * (s[:, _slab(i)] + bg_ref[:, _slab(p)])
    for c in range(n_chunks):
        ya_ref[c * CHUNK:(c + 1) * CHUNK, :] = _rms(jnp.concatenate(ys[c], axis=1), goa_ref[...]).astype(BF16)

    if dilated:
        @pl.when(pl.program_id(0) % tiles_per_seq >= tiles_per_seq - MAX_WINDOW // rows)
        def _():
            kwin_ref[...] = k.T
            vwin_ref[...] = v.T


def _in_proj(x, pw, layer, tables, gate_w, gate_b, *, rows, batch=None, seq=None, prev_win=None):
    n = x.shape[0]
    depth = pw["w_in"].shape[0]
    dilated = batch is not None
    tbase, toff, tsign = tables
    n_base = tbase.shape[0]
    row_spec = lambda w: pl.BlockSpec((rows, w), lambda i: (i, 0))
    lspec = lambda *shape: _layer_spec(shape, layer)
    flat = lambda dt: jax.ShapeDtypeStruct((n, 512), dt)
    scratch = []
    per_b = None
    in_specs = [row_spec(D_MODEL), lspec(1, D_MODEL), lspec(D_MODEL, 5 * 512),
                lspec(1, D_A), lspec(1, D_B), lspec(1, D_B),
                _const_spec((D_B // 2, D_B // 2)),
                pl.BlockSpec((None, 2, LANES), lambda i: (i % n_base, 0, 0)),
                _const_spec((2, rows, LANES)), _const_spec((2, LANES)),
                lspec(N_SLABS, CHUNK, 2 * CHUNK), lspec(CHUNK, D_A), lspec(1, D_A)]
    args = [x, pw["g_attn"], pw["w_in"], pw["g_v"], pw["g_q"], pw["g_k"], pw["hsum"], tbase, toff,
            tsign, gate_w, gate_b, pw["g_oa"]]
    aliases = {}
    if dilated:
        per_b = seq // rows
        split_shape = lambda d, dt: jax.ShapeDtypeStruct((batch, d, seq // d, D_B), dt)
        split_spec = lambda d: pl.BlockSpec((None, d, rows // d, D_B),
                                            lambda i: (i // per_b, 0, i % per_b, 0))
        kv_shape = [flat(BF16), split_shape(4, BF16), split_shape(16, BF16)]
        kv_spec = [row_spec(512), split_spec(4), split_spec(16)]
        first_win = per_b - MAX_WINDOW // rows
        win_shape = jax.ShapeDtypeStruct((depth, batch, D_B, MAX_WINDOW), F32)
        win_spec = pl.BlockSpec((None, None, D_B, rows),
                                lambda i: (layer, i // per_b, 0, jnp.maximum(i % per_b - first_win, 0)))
        out_shape = [flat(BF16), split_shape(16, F32)] + kv_shape * 2 + [win_shape] * 2
        out_specs = [row_spec(512), split_spec(16)] + kv_spec * 2 + [win_spec] * 2
        scratch = [pltpu.VMEM((N_SLABS, rows, LANES), F32)] * 2
        if prev_win is not None:
            in_specs += [pl.BlockSpec(memory_space=pl.ANY)] * 2
            aliases = {len(args): len(out_shape) - 2, len(args) + 1: len(out_shape) - 1}
            args += list(prev_win)
    else:
        t_shape = jax.ShapeDtypeStruct((512, n), F32)
        t_spec = pl.BlockSpec((512, rows), lambda i: (0, i))
        out_shape = [flat(BF16), flat(BF16), t_shape, t_shape, flat(F32)]
        out_specs = [row_spec(512), row_spec(512), t_spec, t_spec, row_spec(512)]
    return pl.pallas_call(
        functools.partial(_in_proj_body, rows=rows, tiles_per_seq=per_b, n_in=len(args)),
        grid=(n // rows,),
        in_specs=in_specs,
        out_specs=out_specs,
        out_shape=out_shape,
        scratch_shapes=scratch,
        input_output_aliases=aliases,
        compiler_params=_params(1),
        name="in_proj",
    )(*args)


def _band_block(q, kp, kc, vp, vc, bias):
    lane_lo = lax.broadcasted_iota(jnp.int32, (BAND, LANES), 1) < HEAD_DIM
    head_mask = [(lax.broadcasted_iota(jnp.int32, (1, LANES), 1) // HEAD_DIM == hh)
                 .astype(F32).astype(BF16) for hh in range(2)]
    ones = jnp.ones((2 * BAND, LANES), BF16)
    pick = lambda t: jnp.where(lane_lo, t[:BAND], t[BAND:])
    out = []
    for p in range(N_SLABS):
        qp = q[:, _slab(p)]
        kk = jnp.concatenate([kp[:, _slab(p)], kc[:, _slab(p)]], axis=0)
        vv = jnp.concatenate([vp[:, _slab(p)], vc[:, _slab(p)]], axis=0)
        qq = jnp.concatenate([qp * head_mask[0], qp * head_mask[1]], axis=0)
        s = _dot_nt(qq, kk) + bias
        m = jnp.max(s, axis=-1, keepdims=True)
        pr = jnp.exp(s - m).astype(BF16)
        a = _dot(pr, jnp.concatenate([vv, ones], axis=1))
        l = pick(a[:, LANES:])
        out.append((pick(a[:, :LANES]) * (1.0 / l), pick(m) * LOG2_E + jnp.log2(l)))
    return out


def _attn_prompt_body(q1, k1c, k1p, v1c, v1p, q4, k4c, k4p, v4c, v4p, q16, k16c, k16p, v16c, v16p,
                      bias_ref, gob_ref, *rest):
    n_w = (len(rest) - 7) // 2
    w_f32, out_ref, w_bf16 = rest[:n_w], rest[n_w], rest[n_w + 1:2 * n_w + 1]
    o1, s1, o4, s4, o16, s16 = rest[2 * n_w + 1:]
    for src, dst in zip(w_f32, w_bf16):
        dst[...] = src[...].astype(BF16)
    j = pl.program_id(1)
    g = pl.program_id(2)
    n_groups = pl.num_programs(2)

    def run(branch, q, kc, kp, vc, vp, first, refs, runs):
        bias = bias_ref[branch, first] if isinstance(first, int) else \
            bias_ref[branch, first.astype(jnp.int32)]
        res = _band_block(q.reshape(BAND, D_B).astype(BF16), kp, kc, vp, vc, bias)
        n = BAND // len(runs)
        for p in range(N_SLABS):
            for ref, val in zip(refs, res[p]):
                for i, row0 in enumerate(runs):
                    ref[p, pl.ds(pl.multiple_of(row0, n), n), :] = val[i * n:(i + 1) * n]

    for i in range(SUB):
        c = SUB * g + i
        blk = slice(i * BAND, (i + 1) * BAND)
        if i == 0:
            kp, vp, first = k1p[...], v1p[...], (n_groups * j + g) == 0
        else:
            before = slice((i - 1) * BAND, i * BAND)
            kp, vp, first = k1c[before, :], v1c[before, :], 0
        run(0, q1[:, 8 * i:8 * (i + 1), :], k1c[blk, :], kp, v1c[blk, :], vp, first, (o1, s1),
            [BAND * r + 8 * c for r in range(16)])
        run(1, q4[:, i], k4c[i], k4p[i], v4c[i], v4p[i], (n_groups * j + g) == 0, (o4, s4),
            [BAND * (4 * cc + i) + 32 * g for cc in range(4)])
        run(2, q16[i], k16c[i], k16p[i], v16c[i], v16p[i], j == 0, (o16, s16), [BAND * c])

    @pl.when(g == n_groups - 1)
    def _finalize():
        parts = [(o1, s1), (o4, s4), (o16, s16)]
        for r in range(16):
            rows = pl.ds(r * BAND, BAND)
            slabs = []
            for p in range(N_SLABS):
                lse = [s_ref[p, rows, :] for (_, s_ref) in parts]
                top = jnp.maximum(jnp.maximum(lse[0], lse[1]), lse[2])
                num = den = None
                for e, (o_ref, _) in zip(lse, parts):
                    w = jnp.exp2(e - top)
                    t = o_ref[p, rows, :] * w
                    num, den = (t, w) if num is None else (num + t, den + w)
                slabs.append(num / den)
            y = _rms(jnp.concatenate(slabs, axis=1), gob_ref[...])
            for p in range(N_SLABS):
                o16[p, rows, :] = y[:, _slab(p)]
        for p in range(N_SLABS):
            for r in range(16):
                o1[p, pl.ds(r, BAND, stride=16), :] = o16[p, r * BAND:(r + 1) * BAND, :]
            out_ref[:, _slab(p)] = o1[p].astype(BF16)


def _attn_prompt(q16, kv, bias, g_ob, layer, batch, seq, weights):
    assert SUB == 4
    n_tiles = seq // TILE
    n_groups = 16 // SUB
    span = SUB * BAND
    prev = lambda i: jnp.maximum(i - 1, 0)
    grp = lambda j, g: n_groups * j + g
    branches = [
        (pl.BlockSpec((None, 16, span // 16, D_B), lambda b, j, g: (b, 0, grp(j, g), 0)), q16,
         ((None, span, D_B), lambda b, j, g: (b, grp(j, g), 0)),
         ((None, BAND, D_B), lambda b, j, g: (b, prev(SUB * grp(j, g)), 0))),
        (pl.BlockSpec((None, 4, SUB, span // 16, D_B), lambda b, j, g: (b, 0, 0, grp(j, g), 0)),
         q16.reshape(batch, 4, 4, seq // 16, D_B),
         ((None, SUB, BAND, D_B), lambda b, j, g: (b, 0, grp(j, g), 0)),
         ((None, SUB, BAND, D_B), lambda b, j, g: (b, 0, prev(grp(j, g)), 0))),
        (pl.BlockSpec((None, SUB, BAND, D_B), lambda b, j, g: (b, g, j, 0)), q16,
         ((None, SUB, BAND, D_B), lambda b, j, g: (b, g, j, 0)),
         ((None, SUB, BAND, D_B), lambda b, j, g: (b, g, prev(j), 0))),
    ]
    in_specs, args = [], []
    for di, (q_spec, q_arr, cur, prv) in enumerate(branches):
        cs, ps = pl.BlockSpec(*cur), pl.BlockSpec(*prv)
        in_specs += [q_spec, cs, ps, cs, ps]
        kd, vd = (t[di] if di else t[di].reshape(batch, seq, D_B) for t in kv)
        args += [q_arr, kd, kd, vd, vd]
    in_specs += [_const_spec((3, 2, 2 * BAND, 2 * BAND)), _layer_spec((1, D_B), layer)]
    args += [bias, g_ob]
    n_steps = batch * n_tiles * n_groups
    step = lambda b, j, g: (b * n_tiles + j) * n_groups + g
    w_out_specs, w_out_shapes = [], []
    for w in weights:
        rows, cols = w.shape[1] // n_steps, w.shape[2]
        in_specs.append(pl.BlockSpec((None, rows, cols), lambda b, j, g: (layer, step(b, j, g), 0)))
        w_out_specs.append(pl.BlockSpec((rows, cols), lambda b, j, g: (step(b, j, g), 0)))
        w_out_shapes.append(jax.ShapeDtypeStruct(w.shape[1:], BF16))
    args += list(weights)
    out, *w_bf16 = pl.pallas_call(
        _attn_prompt_body,
        grid=(batch, n_tiles, n_groups),
        in_specs=in_specs,
        out_specs=[pl.BlockSpec((None, TILE, D_B), lambda b, j, g: (b, j, 0))] + w_out_specs,
        out_shape=[jax.ShapeDtypeStruct((batch, seq, D_B), BF16)] + w_out_shapes,
        scratch_shapes=[pltpu.VMEM((N_SLABS, TILE, LANES), F32)] * 6,
        compiler_params=_params(3),
        name="attn_prompt",
    )(*args)
    return out.reshape(batch * seq, D_B), w_bf16


def _band_bias():
    kj = np.arange(2 * BAND)[None, :]
    rho = np.arange(BAND)
    out = []
    for runs in (16, 4, 1):
        per = BAND // runs
        qi = (runs * (rho % per) + rho // per)[:, None]
        ok = (kj >= qi) & (kj <= qi + BAND)
        out.append(np.stack([np.where(ok, 0.0, -np.inf), np.where(ok & (kj >= BAND), 0.0, -np.inf)]))
    return np.tile(np.stack(out), (1, 1, 2, 1)).astype(np.float32)


def _window_attention(kt_ref, vt_ref, kn_ref, vn_ref, q_ref, mult_ref, okt_ref, ovt_ref, o_ref,
                      seq_idx):
    n_heads = kt_ref.shape[0]
    n_feat = n_heads * HEAD_DIM
    stack = lambda w_ref, n_ref: jnp.concatenate(
        [w_ref[...].reshape(n_feat, MAX_WINDOW), n_ref[...].reshape(n_feat, LANES)], axis=1).astype(BF16)
    state = {}

    def scores():
        own = lax.broadcasted_iota(jnp.int32, (8 * n_heads, LANES), 1) // T_NEW == seq_idx
        mult = jnp.concatenate([mult_ref[:, 0:MAX_WINDOW],
                                jnp.where(own, mult_ref[:, MAX_WINDOW:], 0.0)], axis=1)
        s = jnp.where(mult > 0, _dot(q_ref[...], stack(kt_ref, kn_ref)), -jnp.inf)
        state["p"] = mult * jnp.exp(s - jnp.max(s, axis=-1, keepdims=True))

    def output():
        p = state["p"]
        o_ref[...] = _dot_nt(p.astype(BF16), stack(vt_ref, vn_ref)) / jnp.sum(p, axis=-1, keepdims=True)

    def shift():
        body = MAX_WINDOW - LANES
        new_shift = (LANES - T_NEW * (seq_idx + 1)) % LANES
        is_new = lax.broadcasted_iota(jnp.int32, (HEAD_DIM, LANES), 1) >= LANES - T_NEW
        for src_ref, new_ref, dst_ref in ((kt_ref, kn_ref, okt_ref), (vt_ref, vn_ref, ovt_ref)):
            for h in range(n_heads):
                y = pltpu.roll(src_ref[h], MAX_WINDOW - T_NEW, 1)
                dst_ref[h, :, 0:body] = y[:, 0:body]
                dst_ref[h, :, body:MAX_WINDOW] = jnp.where(
                    is_new, pltpu.roll(new_ref[h], new_shift, 1), y[:, body:])

    return scores, output, shift


def _sample_multiplicity(n_rows):
    def mult(d):
        ok = d >= 0
        return (ok & (d <= 128)).astype(np.int32) + (ok & (d % 4 == 0) & (d <= 512)) \
            + (ok & (d % 16 == 0) & (d <= 2048))
    t = (np.arange(n_rows) % 8 % T_NEW)[:, None]
    old = mult(MAX_WINDOW + t - np.arange(MAX_WINDOW)[None, :])
    new = mult(t - (np.arange(LANES) % T_NEW)[None, :])
    return np.concatenate([old, new], axis=1).astype(np.float32)


def _ffn_compute(x_ref, ya_ref, yb_ref, gob_ref, wout_ref, gffn_ref, wup_ref, wdown_ref, o_ref,
                 norm_b, between=lambda stage: None):
    yb = yb_ref[...]
    if norm_b:
        yb = _rms(yb, gob_ref[...])
    x1 = x_ref[...] + _dot(ya_ref[...], wout_ref[0:D_A, :]) \
        + _dot(yb.astype(BF16), wout_ref[D_A:D_A + D_B, :])
    hb = _rms(x1, gffn_ref[...]).astype(BF16)
    between(0)
    step = D_FF // 4
    mlp = None
    for c in range(4):
        f = _dot(hb, wup_ref[:, c * step:(c + 1) * step])
        f = jnp.square(jnp.maximum(f, 0.0)).astype(BF16)
        d = _dot(f, wdown_ref[c * step:(c + 1) * step, :])
        mlp = d if mlp is None else mlp + d
        between(c + 1)
    o_ref[...] = x1 + mlp


def _out_ffn_body(*refs, norm_b):
    _ffn_compute(*refs, norm_b)


def _ffn_window_body(*refs, steps_per_seq):
    n_ffn = 8
    ffn_in, (kt, vt, kn, vn, q, mult) = refs[:n_ffn], refs[n_ffn:n_ffn + 6]
    x_out, okt, ovt, o_att = refs[-4:]
    seq_idx = pl.program_id(0) // steps_per_seq
    scores, output, shift = _window_attention(kt, vt, kn, vn, q, mult, okt, ovt, o_att, seq_idx)
    stages = {0: scores, 2: output, 4: shift}
    _ffn_compute(*ffn_in, x_out, False, between=lambda stage: stages.get(stage, lambda: None)())


def _ffn_specs(rows, layer):
    row_spec = lambda w: pl.BlockSpec((rows, w), lambda i: (i, 0))
    lspec = lambda *shape: _layer_spec(shape, layer)
    return [row_spec(D_MODEL), row_spec(D_A), row_spec(D_B), lspec(1, D_B),
            _const_spec((D_A + D_B, D_MODEL)), lspec(1, D_MODEL),
            _const_spec((D_MODEL, D_FF)), _const_spec((D_FF, D_MODEL))], row_spec(D_MODEL)


def _out_ffn(x, ya, yb, pw, layer, weights, *, rows, norm_b):
    n = x.shape[0]
    in_specs, out_spec = _ffn_specs(rows, layer)
    return pl.pallas_call(
        functools.partial(_out_ffn_body, norm_b=norm_b),
        grid=(n // rows,),
        in_specs=in_specs,
        out_specs=out_spec,
        out_shape=jax.ShapeDtypeStruct((n, D_MODEL), F32),
        compiler_params=_params(1),
        name="out_ffn",
    )(x, ya, yb, pw["g_ob"], weights[0], pw["g_ffn"], weights[1], weights[2])


def _ffn_window(x, ya, yb, pw, layer, weights, cache_kt, cache_vt, k_new, v_new, q, mult, prev_out,
                *, rows):
    n = x.shape[0]
    steps = n // rows
    depth, db = cache_kt.shape[:2]
    per_seq = steps // db
    hps = N_HEADS // per_seq
    assert per_seq * db == steps and hps * per_seq == N_HEADS
    assert q.shape == (db, per_seq, 8 * hps, HEAD_DIM * hps)
    in_specs, out_spec = _ffn_specs(rows, layer)
    cache_spec = pl.BlockSpec((None, None, hps, HEAD_DIM, MAX_WINDOW),
                              lambda i: (layer, i // per_seq, i % per_seq, 0, 0))
    new_spec = pl.BlockSpec((hps, HEAD_DIM, LANES), lambda i: (i % per_seq, 0, 0))
    q_spec = pl.BlockSpec((None, None) + q.shape[2:], lambda i: (i // per_seq, i % per_seq, 0, 0))
    in_specs += [cache_spec, cache_spec, new_spec, new_spec, q_spec,
                 _const_spec((8 * hps, MAX_WINDOW + LANES))]
    args = [x, ya, yb, pw["g_ob"], weights[0], pw["g_ffn"], weights[1], weights[2],
            cache_kt, cache_vt, k_new, v_new, q, mult]
    aliases = {}
    if prev_out is not None:
        in_specs += [pl.BlockSpec(memory_space=pl.ANY)] * 2
        aliases = {len(args): 1, len(args) + 1: 2}
        args += list(prev_out)
    win_shape = jax.ShapeDtypeStruct(cache_kt.shape, F32)
    return pl.pallas_call(
        functools.partial(_ffn_window_body, steps_per_seq=per_seq),
        grid=(steps,),
        in_specs=in_specs,
        out_specs=[out_spec, cache_spec, cache_spec, q_spec],
        out_shape=[jax.ShapeDtypeStruct((n, D_MODEL), F32), win_shape, win_shape,
                   jax.ShapeDtypeStruct(q.shape, F32)],
        input_output_aliases=aliases,
        compiler_params=_params(1),
        name="ffn_window",
    )(*args)


def _rope_tables(base_pos, offsets):
    half = ROT_DIM // 2
    inv = jnp.power(ROPE_THETA, -jnp.arange(half, dtype=F32) * (2.0 / ROT_DIM))
    lane = np.arange(LANES) % HEAD_DIM
    freq = jnp.where(lane < ROT_DIM, inv[lane % half], 0.0)
    ang = lambda pos: pos.astype(F32)[:, None] * freq[None, :]
    ab, ao = ang(base_pos), ang(offsets)
    tbase = jnp.stack([jnp.cos(ab), jnp.sin(ab)], axis=1)
    toff = jnp.stack([jnp.cos(ao), jnp.sin(ao)], axis=0)
    tsign = jnp.asarray(np.stack([-1.0 * (lane < half), 1.0 * ((lane >= half) & (lane < ROT_DIM))]), F32)
    return tbase, toff, tsign


def kernel(x_prompt, x_sample, cache_win_k, cache_win_v, attn_norm_g, w_in, v_norm_g, w_spatial, b_spatial, q_norm_g, k_norm_g, out_norm_a_g, out_norm_b_g, w_out, ffn_norm_g, w_up, w_down):
    depth = w_in.shape[0]
    batch, seq, _ = x_prompt.shape
    db, t_new, _ = x_sample.shape
    n_s = db * t_new
    assert n_s == CHUNK and t_new == T_NEW and cache_win_k.shape[2] == MAX_WINDOW
    assert seq % TILE == 0

    rows_p = 512
    tab_p = _rope_tables(jnp.arange(seq // rows_p) * rows_p, jnp.arange(rows_p))
    tab_s = _rope_tables(jnp.full((1,), PAST_LEN), jnp.arange(n_s) % t_new)

    tril = jnp.tril(jnp.ones((CHUNK, CHUNK), bool))
    hsum = jnp.kron(jnp.eye(N_HEADS // 2, dtype=F32), jnp.full((HEAD_DIM, HEAD_DIM), 1.0 / HEAD_DIM, F32)).astype(BF16)
    row = lambda g: g[:, None, :]
    per_head = lambda g: jnp.tile(g, (1, N_HEADS))
    pw = dict(g_attn=row(attn_norm_g), w_in=w_in.astype(BF16), g_v=row(v_norm_g),
              g_q=row(per_head(q_norm_g)), g_k=row(per_head(k_norm_g)), hsum=hsum,
              g_oa=row(out_norm_a_g), g_ob=row(out_norm_b_g), g_ffn=row(ffn_norm_g))
    pair = lambda w: w.reshape(depth, N_SLABS, 2, CHUNK, CHUNK).transpose(0, 1, 3, 2, 4) \
        .reshape(depth, N_SLABS, CHUNK, 2 * CHUNK).astype(BF16)
    gate_w_p = pair(jnp.where(tril, w_spatial, 0))
    gate_b_p = jnp.repeat(b_spatial.transpose(0, 2, 1), HEAD_DIM, axis=2)
    t_of = np.arange(CHUNK) % t_new
    same_seq = (np.arange(CHUNK)[:, None] // t_new) == (np.arange(CHUNK)[None, :] // t_new)
    causal_s = jnp.asarray(same_seq & (t_of[:, None] >= t_of[None, :]))
    onehot = jnp.asarray(t_of[:, None] == np.arange(t_new)[None, :], F32)
    w_rep = jnp.einsum("ia,lhab,jb->lhij", onehot, w_spatial[:, :, :t_new, :t_new], onehot,
                       precision=lax.Precision.HIGHEST)
    gate_w_s = pair(jnp.where(causal_s, w_rep, 0))
    b_rep = jnp.einsum("ia,lha->lih", onehot, b_spatial[:, :, :t_new], precision=lax.Precision.HIGHEST)
    gate_b_s = jnp.repeat(b_rep, HEAD_DIM, axis=2)

    bias = jnp.asarray(_band_bias())
    rows_ffn = 256
    per_seq = batch * seq // rows_ffn // db
    hps = N_HEADS // per_seq
    mult = jnp.asarray(_sample_multiplicity(8 * hps))
    eye_h = jnp.eye(hps, dtype=F32)[None, None, :, None, :, None]

    cache_kt = cache_win_k.transpose(0, 1, 3, 4, 2)
    cache_vt = cache_win_v.transpose(0, 1, 3, 4, 2)

    xp = x_prompt.reshape(batch * seq, D_MODEL)
    xs = x_sample.reshape(n_s, D_MODEL)
    cv_list = []
    win_p = win_s = None
    for l in range(depth):
        ya_p, q16, *kv, kwin, vwin = _in_proj(xp, pw, l, tab_p, gate_w_p, gate_b_p, rows=rows_p,
                                              batch=batch, seq=seq, prev_win=win_p)
        win_p = (kwin, vwin)
        yb_p, w_l = _attn_prompt(q16, (kv[0:3], kv[3:6]), bias, pw["g_ob"], l, batch, seq,
                                 (w_out, w_up, w_down))
        ya_s, q, kt, vt, va = _in_proj(xs, pw, l, tab_s, gate_w_s, gate_b_s, rows=CHUNK)
        cv_list.append(va.reshape(db, t_new, D_A))
        q8 = jnp.tile(q.reshape(db, t_new, per_seq, hps, HEAD_DIM).transpose(0, 2, 3, 1, 4), (1, 1, 1, 2, 1))
        q_blk = (q8[:, :, :, :, None, :] * eye_h.astype(BF16)).reshape(db, per_seq, 8 * hps, hps * HEAD_DIM)
        heads = lambda t: t.reshape(N_HEADS, HEAD_DIM, n_s)

        xp, okt, ovt, o = _ffn_window(xp, ya_p, yb_p, pw, l, w_l, cache_kt, cache_vt, heads(kt),
                                      heads(vt), q_blk, mult, win_s, rows=rows_ffn)
        win_s = (okt, ovt)
        o = (o.reshape(db, per_seq, hps, 8, hps, HEAD_DIM) * eye_h).sum(axis=4)
        yb_s = o[:, :, :, :t_new].transpose(0, 3, 1, 2, 4).reshape(n_s, D_B)
        xs = _out_ffn(xs, ya_s, yb_s, pw, l, w_l, rows=CHUNK, norm_b=True)

    win_p = [w.reshape(depth, batch, N_HEADS, HEAD_DIM, MAX_WINDOW) for w in win_p]
    unt = lambda w: w.transpose(0, 1, 4, 2, 3)
    return (xp.reshape(batch, seq, D_MODEL), xs.reshape(db, t_new, D_MODEL),
            unt(win_p[0]), unt(win_p[1]), unt(win_s[0]), unt(win_s[1]),
            jnp.stack(cv_list))
```

```python
import functools

import numpy as np
import jax
import jax.numpy as jnp
from jax import lax
from jax.experimental import pallas as pl
from jax.experimental.pallas import tpu as pltpu

D_MODEL = 1024
D_A = 512
D_B = 512
HEAD_DIM = 64
N_HEADS = 8
CHUNK = 128
D_FF = 4 * D_MODEL
ROT_DIM = HEAD_DIM // 4
ROPE_THETA = 500000.0
EPS = 1e-6
ATTN_SCALE = HEAD_DIM ** -0.5
LOG2_E = float(np.log2(np.e))
BAND = 128
TILE = 16 * BAND
SUB = 4
MAX_WINDOW = 2048
PAST_LEN = 16384
T_NEW = 4
Q_ROWS = 8

ROWS_IN_PROJ = 512
ROWS_FFN = 256
N_IN_PROJ_INPUTS = 13

LANES = 128
N_SLABS = D_B // LANES
VMEM_LIMIT = 56 * 1024 * 1024

F32 = jnp.float32
BF16 = jnp.bfloat16


def _rms(x, g):
    return x * lax.rsqrt(jnp.mean(x * x, axis=-1, keepdims=True) + EPS) * g


def _gelu(x):
    c = np.float32(np.sqrt(2.0 / np.pi))
    return x * (0.5 * (1.0 + jnp.tanh(c * (x + 0.044715 * (x * x * x)))))


def _dot(a, b):
    return jnp.dot(a, b, preferred_element_type=F32)


def _dot_nt(a, b):
    return lax.dot_general(a, b, (((1,), (1,)), ((), ())), preferred_element_type=F32)


def _params(n_axes):
    return pltpu.CompilerParams(dimension_semantics=("arbitrary",) * n_axes,
                                vmem_limit_bytes=VMEM_LIMIT)


def _const_spec(shape):
    return pl.BlockSpec(shape, lambda *_: (0,) * len(shape), pipeline_mode=pl.Buffered(1))


def _layer_spec(shape, layer):
    return pl.BlockSpec((None,) + shape, lambda *_: (layer,) + (0,) * len(shape),
                        pipeline_mode=pl.Buffered(1))


def _slab(p):
    return slice(p * LANES, (p + 1) * LANES)


def _in_proj_body(x_ref, gattn_ref, win_ref, gv_ref, gq_ref, gk_ref, hsum_ref, tbase_ref, toff_ref,
                  tsign_ref, wg_ref, bg_ref, goa_ref, *outs, rows, tiles_per_seq, n_in):
    dilated = tiles_per_seq is not None
    outs = outs[n_in - N_IN_PROJ_INPUTS:]
    hb = _rms(x_ref[...], gattn_ref[...]).astype(BF16)

    cb, sb = tbase_ref[0:1, :], tbase_ref[1:2, :]
    co, so = toff_ref[0], toff_ref[1]
    cos = cb * co - sb * so
    sin = sb * co + cb * so
    sin_up, sin_dn = sin * tsign_ref[0:1, :], sin * tsign_ref[1:2, :]

    def proj(i):
        return _dot(hb, win_ref[:, i * 512:(i + 1) * 512])

    def head_norm_rope(z, g_ref):
        zz = (z * z).astype(BF16)
        half = D_B // 2
        ms = jnp.concatenate([_dot(zz[:, :half], hsum_ref[...]), _dot(zz[:, half:], hsum_ref[...])],
                             axis=1)
        zn = z * lax.rsqrt(ms + EPS) * g_ref[...]
        slabs = []
        for p in range(N_SLABS):
            s = zn[:, _slab(p)]
            slabs.append(s * cos + pltpu.roll(s, LANES - ROT_DIM // 2, 1) * sin_up
                         + pltpu.roll(s, ROT_DIM // 2, 1) * sin_dn)
        return jnp.concatenate(slabs, axis=1)

    if dilated:
        ya_ref, q16, k1, k4, k16, v1, v4, v16, kwin_ref, vwin_ref, scr1, scr4 = outs

        def split(val, r4_ref, r16_ref):
            for p in range(N_SLABS):
                scr1[p] = val[:, _slab(p)]
            n4, n16 = rows // 4, rows // 16
            for p in range(N_SLABS):
                for r in range(4):
                    part = scr1[p, pl.ds(r, n4, stride=4), :]
                    scr4[p, r * n4:(r + 1) * n4, :] = part
                    if r4_ref is not None:
                        r4_ref[r, :, _slab(p)] = part.astype(BF16)
                for r in range(4):
                    for c in range(4):
                        part = scr4[p, pl.ds(r * n4 + c, n16, stride=4), :]
                        r16_ref[r + 4 * c, :, _slab(p)] = part.astype(r16_ref.dtype)
    else:
        ya_ref, q1, kt_ref, vt_ref, va_ref = outs

    zu = proj(0)
    zq = proj(2)
    u = _gelu(zu)
    q = head_norm_rope(zq, gq_ref) * ATTN_SCALE
    zva = proj(1)
    if dilated:
        split(q, None, q16)
    else:
        q1[...] = q.astype(BF16)
    zk = proj(3)
    va = _rms(_gelu(zva), gv_ref[...])
    k = head_norm_rope(zk, gk_ref)
    v = proj(4)
    if dilated:
        k1[...] = k.astype(BF16)
        split(k, k4, k16)
        v1[...] = v.astype(BF16)
        split(v, v4, v16)
    else:
        va_ref[...] = va
        kt_ref[...] = k.T
        vt_ref[...] = v.T

    lane_lo = lax.broadcasted_iota(jnp.int32, (CHUNK, LANES), 1) < HEAD_DIM
    n_chunks = rows // CHUNK
    side = min(2, n_chunks)
    ys = [[None] * N_SLABS for _ in range(n_chunks)]
    for c0 in range(0, n_chunks, side):
        for p in range(N_SLABS):
            parts = [va[c * CHUNK:(c + 1) * CHUNK, _slab(p)] for c in range(c0, c0 + side)]
            lo = jnp.concatenate([jnp.where(lane_lo, t, 0.0) for t in parts], axis=1)
            hi = jnp.concatenate([jnp.where(lane_lo, 0.0, t) for t in parts], axis=1)
            s = _dot(wg_ref[p], jnp.concatenate([lo, hi], axis=0).astype(BF16))
            for i in range(side):
                c = c0 + i
                ys[c][p] = u[c * CHUNK:(c + 1) * CHUNK, _slab(p)] * (s[:, _slab(i)] + bg_ref[:, _slab(p)])
    for c in range(n_chunks):
        ya_ref[c * CHUNK:(c + 1) * CHUNK, :] = _rms(jnp.concatenate(ys[c], axis=1), goa_ref[...]).astype(BF16)

    if dilated:
        @pl.when(pl.program_id(0) % tiles_per_seq >= tiles_per_seq - MAX_WINDOW // rows)
        def _():
            kwin_ref[...] = k.T
            vwin_ref[...] = v.T


def _in_proj(x, pw, layer, tables, gate_w, gate_b, *, rows, batch=None, seq=None, prev_win=None):
    n = x.shape[0]
    depth = pw["w_in"].shape[0]
    dilated = batch is not None
    tbase, toff, tsign = tables
    n_base = tbase.shape[0]
    row_spec = lambda w: pl.BlockSpec((rows, w), lambda i: (i, 0))
    lspec = lambda *shape: _layer_spec(shape, layer)
    flat = lambda dt: jax.ShapeDtypeStruct((n, 512), dt)
    scratch = []
    per_b = None
    in_specs = [row_spec(D_MODEL), lspec(1, D_MODEL), lspec(D_MODEL, 5 * 512),
                lspec(1, D_A), lspec(1, D_B), lspec(1, D_B),
                _const_spec((D_B // 2, D_B // 2)),
                pl.BlockSpec((None, 2, LANES), lambda i: (i % n_base, 0, 0)),
                _const_spec((2, rows, LANES)), _const_spec((2, LANES)),
                lspec(N_SLABS, CHUNK, 2 * CHUNK), lspec(CHUNK, D_A), lspec(1, D_A)]
    args = [x, pw["g_attn"], pw["w_in"], pw["g_v"], pw["g_q"], pw["g_k"], pw["hsum"], tbase, toff,
            tsign, gate_w, gate_b, pw["g_oa"]]
    aliases = {}
    if dilated:
        per_b = seq // rows
        split_shape = lambda d, dt: jax.ShapeDtypeStruct((batch, d, seq // d, D_B), dt)
        split_spec = lambda d: pl.BlockSpec((None, d, rows // d, D_B),
                                            lambda i: (i // per_b, 0, i % per_b, 0))
        kv_shape = [flat(BF16), split_shape(4, BF16), split_shape(16, BF16)]
        kv_spec = [row_spec(512), split_spec(4), split_spec(16)]
        first_win = per_b - MAX_WINDOW // rows
        win_shape = jax.ShapeDtypeStruct((depth, batch, D_B, MAX_WINDOW), F32)
        win_spec = pl.BlockSpec((None, None, D_B, rows),
                                lambda i: (layer, i // per_b, 0, jnp.maximum(i % per_b - first_win, 0)))
        out_shape = [flat(BF16), split_shape(16, F32)] + kv_shape * 2 + [win_shape] * 2
        out_specs = [row_spec(512), split_spec(16)] + kv_spec * 2 + [win_spec] * 2
        scratch = [pltpu.VMEM((N_SLABS, rows, LANES), F32)] * 2
        if prev_win is not None:
            in_specs += [pl.BlockSpec(memory_space=pl.ANY)] * 2
            aliases = {len(args): len(out_shape) - 2, len(args) + 1: len(out_shape) - 1}
            args += list(prev_win)
    else:
        t_shape = jax.ShapeDtypeStruct((512, n), F32)
        t_spec = pl.BlockSpec((512, rows), lambda i: (0, i))
        out_shape = [flat(BF16), flat(BF16), t_shape, t_shape, flat(F32)]
        out_specs = [row_spec(512), row_spec(512), t_spec, t_spec, row_spec(512)]
    return pl.pallas_call(
        functools.partial(_in_proj_body, rows=rows, tiles_per_seq=per_b, n_in=len(args)),
        grid=(n // rows,),
        in_specs=in_specs,
        out_specs=out_specs,
        out_shape=out_shape,
        scratch_shapes=scratch,
        input_output_aliases=aliases,
        compiler_params=_params(1),
        name="in_proj",
    )(*args)


def _band_block(q, kp, kc, vp, vc, bias):
    lane_lo = lax.broadcasted_iota(jnp.int32, (BAND, LANES), 1) < HEAD_DIM
    head_mask = [(lax.broadcasted_iota(jnp.int32, (1, LANES), 1) // HEAD_DIM == hh)
                 .astype(F32).astype(BF16) for hh in range(2)]
    ones = jnp.ones((2 * BAND, LANES), BF16)
    pick = lambda t: jnp.where(lane_lo, t[:BAND], t[BAND:])
    out = []
    for p in range(N_SLABS):
        qp = q[:, _slab(p)]
        kk = jnp.concatenate([kp[:, _slab(p)], kc[:, _slab(p)]], axis=0)
        vv = jnp.concatenate([vp[:, _slab(p)], vc[:, _slab(p)]], axis=0)
        qq = jnp.concatenate([qp * head_mask[0], qp * head_mask[1]], axis=0)
        s = _dot_nt(qq, kk) + bias
        m = jnp.max(s, axis=-1, keepdims=True)
        pr = jnp.exp(s - m).astype(BF16)
        a = _dot(pr, jnp.concatenate([vv, ones], axis=1))
        l = pick(a[:, LANES:])
        out.append((pick(a[:, :LANES]) * (1.0 / l), pick(m) * LOG2_E + jnp.log2(l)))
    return out


def _attn_prompt_body(q1, k1c, k1p, v1c, v1p, q4, k4c, k4p, v4c, v4p, q16, k16c, k16p, v16c, v16p,
                      bias_ref, gob_ref, *rest):
    n_w = (len(rest) - 7) // 2
    w_f32, out_ref, w_bf16 = rest[:n_w], rest[n_w], rest[n_w + 1:2 * n_w + 1]
    o1, s1, o4, s4, o16, s16 = rest[2 * n_w + 1:]
    for src, dst in zip(w_f32, w_bf16):
        dst[...] = src[...].astype(BF16)
    j = pl.program_id(1)
    g = pl.program_id(2)
    n_groups = pl.num_programs(2)

    def run(branch, q, kc, kp, vc, vp, first, refs, runs):
        bias = bias_ref[branch, first] if isinstance(first, int) else \
            bias_ref[branch, first.astype(jnp.int32)]
        res = _band_block(q.reshape(BAND, D_B).astype(BF16), kp, kc, vp, vc, bias)
        n = BAND // len(runs)
        for p in range(N_SLABS):
            for ref, val in zip(refs, res[p]):
                for i, row0 in enumerate(runs):
                    ref[p, pl.ds(pl.multiple_of(row0, n), n), :] = val[i * n:(i + 1) * n]

    for i in range(SUB):
        c = SUB * g + i
        blk = slice(i * BAND, (i + 1) * BAND)
        if i == 0:
            kp, vp, first = k1p[...], v1p[...], (n_groups * j + g) == 0
        else:
            before = slice((i - 1) * BAND, i * BAND)
            kp, vp, first = k1c[before, :], v1c[before, :], 0
        run(0, q1[:, 8 * i:8 * (i + 1), :], k1c[blk, :], kp, v1c[blk, :], vp, first, (o1, s1),
            [BAND * r + 8 * c for r in range(16)])
        run(1, q4[:, i], k4c[i], k4p[i], v4c[i], v4p[i], (n_groups * j + g) == 0, (o4, s4),
            [BAND * (4 * cc + i) + 32 * g for cc in range(4)])
        run(2, q16[i], k16c[i], k16p[i], v16c[i], v16p[i], j == 0, (o16, s16), [BAND * c])

    @pl.when(g == n_groups - 1)
    def _finalize():
        parts = [(o1, s1), (o4, s4), (o16, s16)]
        for r in range(16):
            rows = pl.ds(r * BAND, BAND)
            slabs = []
            for p in range(N_SLABS):
                lse = [s_ref[p, rows, :] for (_, s_ref) in parts]
                top = jnp.maximum(jnp.maximum(lse[0], lse[1]), lse[2])
                num = den = None
                for e, (o_ref, _) in zip(lse, parts):
                    w = jnp.exp2(e - top)
                    t = o_ref[p, rows, :] * w
                    num, den = (t, w) if num is None else (num + t, den + w)
                slabs.append(num / den)
            y = _rms(jnp.concatenate(slabs, axis=1), gob_ref[...])
            for p in range(N_SLABS):
                o16[p, rows, :] = y[:, _slab(p)]
        for p in range(N_SLABS):
            for r in range(16):
                o1[p, pl.ds(r, BAND, stride=16), :] = o16[p, r * BAND:(r + 1) * BAND, :]
            out_ref[:, _slab(p)] = o1[p].astype(BF16)


def _attn_prompt(q16, kv, bias, g_ob, layer, batch, seq, weights):
    assert SUB == 4
    n_tiles = seq // TILE
    n_groups = 16 // SUB
    span = SUB * BAND
    prev = lambda i: jnp.maximum(i - 1, 0)
    grp = lambda j, g: n_groups * j + g
    branches = [
        (pl.BlockSpec((None, 16, span // 16, D_B), lambda b, j, g: (b, 0, grp(j, g), 0)), q16,
         ((None, span, D_B), lambda b, j, g: (b, grp(j, g), 0)),
         ((None, BAND, D_B), lambda b, j, g: (b, prev(SUB * grp(j, g)), 0))),
        (pl.BlockSpec((None, 4, SUB, span // 16, D_B), lambda b, j, g: (b, 0, 0, grp(j, g), 0)),
         q16.reshape(batch, 4, 4, seq // 16, D_B),
         ((None, SUB, BAND, D_B), lambda b, j, g: (b, 0, grp(j, g), 0)),
         ((None, SUB, BAND, D_B), lambda b, j, g: (b, 0, prev(grp(j, g)), 0))),
        (pl.BlockSpec((None, SUB, BAND, D_B), lambda b, j, g: (b, g, j, 0)), q16,
         ((None, SUB, BAND, D_B), lambda b, j, g: (b, g, j, 0)),
         ((None, SUB, BAND, D_B), lambda b, j, g: (b, g, prev(j), 0))),
    ]
    in_specs, args = [], []
    for di, (q_spec, q_arr, cur, prv) in enumerate(branches):
        cs, ps = pl.BlockSpec(*cur), pl.BlockSpec(*prv)
        in_specs += [q_spec, cs, ps, cs, ps]
        kd, vd = (t[di] if di else t[di].reshape(batch, seq, D_B) for t in kv)
        args += [q_arr, kd, kd, vd, vd]
    in_specs += [_const_spec((3, 2, 2 * BAND, 2 * BAND)), _layer_spec((1, D_B), layer)]
    args += [bias, g_ob]
    n_steps = batch * n_tiles * n_groups
    step = lambda b, j, g: (b * n_tiles + j) * n_groups + g
    w_out_specs, w_out_shapes = [], []
    for w in weights:
        rows, cols = w.shape[1] // n_steps, w.shape[2]
        in_specs.append(pl.BlockSpec((None, rows, cols), lambda b, j, g: (layer, step(b, j, g), 0)))
        w_out_specs.append(pl.BlockSpec((rows, cols), lambda b, j, g: (step(b, j, g), 0)))
        w_out_shapes.append(jax.ShapeDtypeStruct(w.shape[1:], BF16))
    args += list(weights)
    out, *w_bf16 = pl.pallas_call(
        _attn_prompt_body,
        grid=(batch, n_tiles, n_groups),
        in_specs=in_specs,
        out_specs=[pl.BlockSpec((None, TILE, D_B), lambda b, j, g: (b, j, 0))] + w_out_specs,
        out_shape=[jax.ShapeDtypeStruct((batch, seq, D_B), BF16)] + w_out_shapes,
        scratch_shapes=[pltpu.VMEM((N_SLABS, TILE, LANES), F32)] * 6,
        compiler_params=_params(3),
        name="attn_prompt",
    )(*args)
    return out.reshape(batch * seq, D_B), w_bf16


def _band_bias():
    kj = np.arange(2 * BAND)[None, :]
    rho = np.arange(BAND)
    out = []
    for runs in (16, 4, 1):
        per = BAND // runs
        qi = (runs * (rho % per) + rho // per)[:, None]
        ok = (kj >= qi) & (kj <= qi + BAND)
        out.append(np.stack([np.where(ok, 0.0, -np.inf), np.where(ok & (kj >= BAND), 0.0, -np.inf)]))
    return np.tile(np.stack(out), (1, 1, 2, 1)).astype(np.float32)


def _window_attention(kt_ref, vt_ref, kn_ref, vn_ref, q_ref, mult_ref, okt_ref, ovt_ref, o_ref,
                      seq_idx):
    n_heads = kt_ref.shape[0]
    n_feat = n_heads * HEAD_DIM
    stack = lambda w_ref, n_ref: jnp.concatenate(
        [w_ref[...].reshape(n_feat, MAX_WINDOW), n_ref[...].reshape(n_feat, LANES)], axis=1).astype(BF16)
    state = {}

    def scores():
        own = lax.broadcasted_iota(jnp.int32, (Q_ROWS * n_heads, LANES), 1) // T_NEW == seq_idx
        mult = jnp.concatenate([mult_ref[:, 0:MAX_WINDOW],
                                jnp.where(own, mult_ref[:, MAX_WINDOW:], 0.0)], axis=1)
        s = jnp.where(mult > 0, _dot(q_ref[...], stack(kt_ref, kn_ref)), -jnp.inf)
        state["p"] = mult * jnp.exp(s - jnp.max(s, axis=-1, keepdims=True))

    def output():
        p = state["p"]
        o_ref[...] = _dot_nt(p.astype(BF16), stack(vt_ref, vn_ref)) / jnp.sum(p, axis=-1, keepdims=True)

    def shift():
        body = MAX_WINDOW - LANES
        new_shift = (LANES - T_NEW * (seq_idx + 1)) % LANES
        is_new = lax.broadcasted_iota(jnp.int32, (HEAD_DIM, LANES), 1) >= LANES - T_NEW
        for src_ref, new_ref, dst_ref in ((kt_ref, kn_ref, okt_ref), (vt_ref, vn_ref, ovt_ref)):
            for h in range(n_heads):
                y = pltpu.roll(src_ref[h], MAX_WINDOW - T_NEW, 1)
                dst_ref[h, :, 0:body] = y[:, 0:body]
                dst_ref[h, :, body:MAX_WINDOW] = jnp.where(
                    is_new, pltpu.roll(new_ref[h], new_shift, 1), y[:, body:])

    return scores, output, shift


def _sample_multiplicity(n_rows):
    def mult(d):
        ok = d >= 0
        return (ok & (d <= 128)).astype(np.int32) + (ok & (d % 4 == 0) & (d <= 512)) \
            + (ok & (d % 16 == 0) & (d <= 2048))
    t = (np.arange(n_rows) % Q_ROWS % T_NEW)[:, None]
    old = mult(MAX_WINDOW + t - np.arange(MAX_WINDOW)[None, :])
    new = mult(t - (np.arange(LANES) % T_NEW)[None, :])
    return np.concatenate([old, new], axis=1).astype(np.float32)


def _ffn_compute(x_ref, ya_ref, yb_ref, gob_ref, wout_ref, gffn_ref, wup_ref, wdown_ref, o_ref,
                 norm_b, between=lambda stage: None):
    yb = yb_ref[...]
    if norm_b:
        yb = _rms(yb, gob_ref[...])
    x1 = x_ref[...] + _dot(ya_ref[...], wout_ref[0:D_A, :]) \
        + _dot(yb.astype(BF16), wout_ref[D_A:D_A + D_B, :])
    hb = _rms(x1, gffn_ref[...]).astype(BF16)
    between(0)
    step = D_FF // 4
    mlp = None
    for c in range(4):
        f = _dot(hb, wup_ref[:, c * step:(c + 1) * step])
        f = jnp.square(jnp.maximum(f, 0.0)).astype(BF16)
        d = _dot(f, wdown_ref[c * step:(c + 1) * step, :])
        mlp = d if mlp is None else mlp + d
        between(c + 1)
    o_ref[...] = x1 + mlp


def _out_ffn_body(*refs, norm_b):
    _ffn_compute(*refs, norm_b)


def _ffn_window_body(*refs, steps_per_seq):
    n_ffn = 8
    ffn_in, (kt, vt, kn, vn, q, mult) = refs[:n_ffn], refs[n_ffn:n_ffn + 6]
    x_out, okt, ovt, o_att = refs[-4:]
    seq_idx = pl.program_id(0) // steps_per_seq
    scores, output, shift = _window_attention(kt, vt, kn, vn, q, mult, okt, ovt, o_att, seq_idx)
    stages = {0: scores, 2: output, 4: shift}
    _ffn_compute(*ffn_in, x_out, False, between=lambda stage: stages.get(stage, lambda: None)())


def _ffn_specs(rows, layer):
    row_spec = lambda w: pl.BlockSpec((rows, w), lambda i: (i, 0))
    lspec = lambda *shape: _layer_spec(shape, layer)
    return [row_spec(D_MODEL), row_spec(D_A), row_spec(D_B), lspec(1, D_B),
            _const_spec((D_A + D_B, D_MODEL)), lspec(1, D_MODEL),
            _const_spec((D_MODEL, D_FF)), _const_spec((D_FF, D_MODEL))], row_spec(D_MODEL)


def _out_ffn(x, ya, yb, pw, layer, weights, *, rows, norm_b):
    n = x.shape[0]
    in_specs, out_spec = _ffn_specs(rows, layer)
    return pl.pallas_call(
        functools.partial(_out_ffn_body, norm_b=norm_b),
        grid=(n // rows,),
        in_specs=in_specs,
        out_specs=out_spec,
        out_shape=jax.ShapeDtypeStruct((n, D_MODEL), F32),
        compiler_params=_params(1),
        name="out_ffn",
    )(x, ya, yb, pw["g_ob"], weights[0], pw["g_ffn"], weights[1], weights[2])


def _ffn_window(x, ya, yb, pw, layer, weights, cache_kt, cache_vt, k_new, v_new, q, mult, prev_out,
                *, rows):
    n = x.shape[0]
    steps = n // rows
    depth, db = cache_kt.shape[:2]
    per_seq = steps // db
    hps = N_HEADS // per_seq
    assert per_seq * db == steps and hps * per_seq == N_HEADS
    assert q.shape == (db, per_seq, Q_ROWS * hps, HEAD_DIM * hps)
    in_specs, out_spec = _ffn_specs(rows, layer)
    cache_spec = pl.BlockSpec((None, None, hps, HEAD_DIM, MAX_WINDOW),
                              lambda i: (layer, i // per_seq, i % per_seq, 0, 0))
    new_spec = pl.BlockSpec((hps, HEAD_DIM, LANES), lambda i: (i % per_seq, 0, 0))
    q_spec = pl.BlockSpec((None, None) + q.shape[2:], lambda i: (i // per_seq, i % per_seq, 0, 0))
    in_specs += [cache_spec, cache_spec, new_spec, new_spec, q_spec,
                 _const_spec((Q_ROWS * hps, MAX_WINDOW + LANES))]
    args = [x, ya, yb, pw["g_ob"], weights[0], pw["g_ffn"], weights[1], weights[2],
            cache_kt, cache_vt, k_new, v_new, q, mult]
    aliases = {}
    if prev_out is not None:
        in_specs += [pl.BlockSpec(memory_space=pl.ANY)] * 2
        aliases = {len(args): 1, len(args) + 1: 2}
        args += list(prev_out)
    win_shape = jax.ShapeDtypeStruct(cache_kt.shape, F32)
    return pl.pallas_call(
        functools.partial(_ffn_window_body, steps_per_seq=per_seq),
        grid=(steps,),
        in_specs=in_specs,
        out_specs=[out_spec, cache_spec, cache_spec, q_spec],
        out_shape=[jax.ShapeDtypeStruct((n, D_MODEL), F32), win_shape, win_shape,
                   jax.ShapeDtypeStruct(q.shape, F32)],
        input_output_aliases=aliases,
        compiler_params=_params(1),
        name="ffn_window",
    )(*args)


def _rope_tables(base_pos, offsets):
    half = ROT_DIM // 2
    inv = jnp.power(ROPE_THETA, -jnp.arange(half, dtype=F32) * (2.0 / ROT_DIM))
    lane = np.arange(LANES) % HEAD_DIM
    freq = jnp.where(lane < ROT_DIM, inv[lane % half], 0.0)
    ang = lambda pos: pos.astype(F32)[:, None] * freq[None, :]
    ab, ao = ang(base_pos), ang(offsets)
    tbase = jnp.stack([jnp.cos(ab), jnp.sin(ab)], axis=1)
    toff = jnp.stack([jnp.cos(ao), jnp.sin(ao)], axis=0)
    tsign = jnp.asarray(np.stack([-1.0 * (lane < half), 1.0 * ((lane >= half) & (lane < ROT_DIM))]), F32)
    return tbase, toff, tsign


def kernel(x_prompt, x_sample, cache_win_k, cache_win_v, attn_norm_g, w_in, v_norm_g, w_spatial, b_spatial, q_norm_g, k_norm_g, out_norm_a_g, out_norm_b_g, w_out, ffn_norm_g, w_up, w_down):
    depth = w_in.shape[0]
    batch, seq, _ = x_prompt.shape
    db, t_new, _ = x_sample.shape
    n_s = db * t_new
    assert n_s == CHUNK and t_new == T_NEW and cache_win_k.shape[2] == MAX_WINDOW
    assert seq % TILE == 0

    tab_p = _rope_tables(jnp.arange(seq // ROWS_IN_PROJ) * ROWS_IN_PROJ, jnp.arange(ROWS_IN_PROJ))
    tab_s = _rope_tables(jnp.full((1,), PAST_LEN), jnp.arange(n_s) % t_new)

    tril = jnp.tril(jnp.ones((CHUNK, CHUNK), bool))
    hsum = jnp.kron(jnp.eye(N_HEADS // 2, dtype=F32), jnp.full((HEAD_DIM, HEAD_DIM), 1.0 / HEAD_DIM, F32)).astype(BF16)
    row = lambda g: g[:, None, :]
    per_head = lambda g: jnp.tile(g, (1, N_HEADS))
    pw = dict(g_attn=row(attn_norm_g), w_in=w_in.astype(BF16), g_v=row(v_norm_g),
              g_q=row(per_head(q_norm_g)), g_k=row(per_head(k_norm_g)), hsum=hsum,
              g_oa=row(out_norm_a_g), g_ob=row(out_norm_b_g), g_ffn=row(ffn_norm_g))
    pair = lambda w: w.reshape(depth, N_SLABS, 2, CHUNK, CHUNK).transpose(0, 1, 3, 2, 4) \
        .reshape(depth, N_SLABS, CHUNK, 2 * CHUNK).astype(BF16)
    gate_w_p = pair(jnp.where(tril, w_spatial, 0))
    gate_b_p = jnp.repeat(b_spatial.transpose(0, 2, 1), HEAD_DIM, axis=2)
    t_of = np.arange(CHUNK) % t_new
    same_seq = (np.arange(CHUNK)[:, None] // t_new) == (np.arange(CHUNK)[None, :] // t_new)
    causal_s = jnp.asarray(same_seq & (t_of[:, None] >= t_of[None, :]))
    onehot = jnp.asarray(t_of[:, None] == np.arange(t_new)[None, :], F32)
    w_rep = jnp.einsum("ia,lhab,jb->lhij", onehot, w_spatial[:, :, :t_new, :t_new], onehot,
                       precision=lax.Precision.HIGHEST)
    gate_w_s = pair(jnp.where(causal_s, w_rep, 0))
    b_rep = jnp.einsum("ia,lha->lih", onehot, b_spatial[:, :, :t_new], precision=lax.Precision.HIGHEST)
    gate_b_s = jnp.repeat(b_rep, HEAD_DIM, axis=2)

    bias = jnp.asarray(_band_bias())
    per_seq = batch * seq // ROWS_FFN // db
    hps = N_HEADS // per_seq
    mult = jnp.asarray(_sample_multiplicity(Q_ROWS * hps))
    eye_h = jnp.eye(hps, dtype=F32)[None, None, :, None, :, None]

    cache_kt = cache_win_k.transpose(0, 1, 3, 4, 2)
    cache_vt = cache_win_v.transpose(0, 1, 3, 4, 2)

    xp = x_prompt.reshape(batch * seq, D_MODEL)
    xs = x_sample.reshape(n_s, D_MODEL)
    cv_list = []
    win_p = win_s = None
    for l in range(depth):
        ya_p, q16, *kv, kwin, vwin = _in_proj(xp, pw, l, tab_p, gate_w_p, gate_b_p, rows=ROWS_IN_PROJ,
                                              batch=batch, seq=seq, prev_win=win_p)
        win_p = (kwin, vwin)
        yb_p, w_l = _attn_prompt(q16, (kv[0:3], kv[3:6]), bias, pw["g_ob"], l, batch, seq,
                                 (w_out, w_up, w_down))
        ya_s, q, kt, vt, va = _in_proj(xs, pw, l, tab_s, gate_w_s, gate_b_s, rows=CHUNK)
        cv_list.append(va.reshape(db, t_new, D_A))
        q8 = jnp.tile(q.reshape(db, t_new, per_seq, hps, HEAD_DIM).transpose(0, 2, 3, 1, 4), (1, 1, 1, 2, 1))
        q_blk = (q8[:, :, :, :, None, :] * eye_h.astype(BF16)).reshape(db, per_seq, Q_ROWS * hps, hps * HEAD_DIM)
        heads = lambda t: t.reshape(N_HEADS, HEAD_DIM, n_s)

        xp, okt, ovt, o = _ffn_window(xp, ya_p, yb_p, pw, l, w_l, cache_kt, cache_vt, heads(kt),
                                      heads(vt), q_blk, mult, win_s, rows=ROWS_FFN)
        win_s = (okt, ovt)
        o = (o.reshape(db, per_seq, hps, Q_ROWS, hps, HEAD_DIM) * eye_h).sum(axis=4)
        yb_s = o[:, :, :, :t_new].transpose(0, 3, 1, 2, 4).reshape(n_s, D_B)
        xs = _out_ffn(xs, ya_s, yb_s, pw, l, w_l, rows=CHUNK, norm_b=True)

    win_p = [w.reshape(depth, batch, N_HEADS, HEAD_DIM, MAX_WINDOW) for w in win_p]
    unt = lambda w: w.transpose(0, 1, 4, 2, 3)
    return (xp.reshape(batch, seq, D_MODEL), xs.reshape(db, t_new, D_MODEL),
            unt(win_p[0]), unt(win_p[1]), unt(win_s[0]), unt(win_s[1]),
            jnp.stack(cv_list))
```

```python
import functools

import numpy as np
import jax
import jax.numpy as jnp
from jax import lax
from jax.experimental import pallas as pl
from jax.experimental.pallas import tpu as pltpu

D_MODEL = 1024
D_A = 512
D_B = 512
HEAD_DIM = 64
N_HEADS = 8
CHUNK = 128
D_FF = 4 * D_MODEL
ROT_DIM = HEAD_DIM // 4
ROPE_THETA = 500000.0
EPS = 1e-6
ATTN_SCALE = HEAD_DIM ** -0.5
LOG2_E = float(np.log2(np.e))
BAND = 128
TILE = 16 * BAND
SUB = 4
MAX_WINDOW = 2048
PAST_LEN = 16384
T_NEW = 4
Q_ROWS = 8

ROWS_IN_PROJ = 512
ROWS_FFN = 256
N_IN_PROJ_INPUTS = 13

LANES = 128
N_SLABS = D_B // LANES
VMEM_LIMIT = 56 * 1024 * 1024

F32 = jnp.float32
BF16 = jnp.bfloat16


def _rms(x, g):
    return x * lax.rsqrt(jnp.mean(x * x, axis=-1, keepdims=True) + EPS) * g


def _gelu(x):
    c = np.float32(np.sqrt(2.0 / np.pi))
    return x * (0.5 * (1.0 + jnp.tanh(c * (x + 0.044715 * (x * x * x)))))


def _dot(a, b):
    return jnp.dot(a, b, preferred_element_type=F32)


def _dot_nt(a, b):
    return lax.dot_general(a, b, (((1,), (1,)), ((), ())), preferred_element_type=F32)


def _params(n_axes):
    return pltpu.CompilerParams(dimension_semantics=("arbitrary",) * n_axes,
                                vmem_limit_bytes=VMEM_LIMIT)


def _const_spec(shape):
    return pl.BlockSpec(shape, lambda *_: (0,) * len(shape), pipeline_mode=pl.Buffered(1))


def _layer_spec(shape, layer):
    return pl.BlockSpec((None,) + shape, lambda *_: (layer,) + (0,) * len(shape),
                        pipeline_mode=pl.Buffered(1))


def _slab(p):
    return slice(p * LANES, (p + 1) * LANES)


def _in_proj_body(x_ref, gattn_ref, win_ref, gv_ref, gq_ref, gk_ref, hsum_ref, tbase_ref, toff_ref,
                  tsign_ref, wg_ref, bg_ref, goa_ref, *outs, rows, tiles_per_seq, n_in):
    dilated = tiles_per_seq is not None
    outs = outs[n_in - N_IN_PROJ_INPUTS:]
    hb = _rms(x_ref[...], gattn_ref[...]).astype(BF16)

    cb, sb = tbase_ref[0:1, :], tbase_ref[1:2, :]
    co, so = toff_ref[0], toff_ref[1]
    cos = cb * co - sb * so
    sin = sb * co + cb * so
    sin_up, sin_dn = sin * tsign_ref[0:1, :], sin * tsign_ref[1:2, :]

    def proj(i):
        return _dot(hb, win_ref[:, i * 512:(i + 1) * 512])

    def head_norm_rope(z, g_ref, scale=1.0):
        zz = (z * z).astype(BF16)
        half = D_B // 2
        ms = jnp.concatenate([_dot(zz[:, :half], hsum_ref[...]), _dot(zz[:, half:], hsum_ref[...])],
                             axis=1)
        zn = z * lax.rsqrt(ms + EPS) * g_ref[...]
        slabs = []
        for p in range(N_SLABS):
            s = zn[:, _slab(p)]
            slabs.append((s * cos + pltpu.roll(s, LANES - ROT_DIM // 2, 1) * sin_up
                          + pltpu.roll(s, ROT_DIM // 2, 1) * sin_dn) * scale)
        return slabs

    if dilated:
        ya_ref, q16, k1, k4, k16, v1, v4, v16, kwin_ref, vwin_ref, scr1, scrk, scr4 = outs

        def split(slabs, src, r4_ref, r16_ref):
            for p in range(N_SLABS):
                src[p] = slabs[p]
            n4, n16 = rows // 4, rows // 16
            for p in range(N_SLABS):
                for r in range(4):
                    part = src[p, pl.ds(r, n4, stride=4), :]
                    scr4[p, r * n4:(r + 1) * n4, :] = part
                    if r4_ref is not None:
                        r4_ref[r, :, _slab(p)] = part.astype(BF16)
                for r in range(4):
                    for c in range(4):
                        part = scr4[p, pl.ds(r * n4 + c, n16, stride=4), :]
                        r16_ref[r + 4 * c, :, _slab(p)] = part.astype(r16_ref.dtype)
    else:
        ya_ref, q1, kt_ref, vt_ref, va_ref = outs

    zu = proj(0)
    zq = proj(2)
    u = _gelu(zu)
    q = head_norm_rope(zq, gq_ref, ATTN_SCALE * LOG2_E if dilated else ATTN_SCALE)
    zva = proj(1)
    if dilated:
        split(q, scr1, None, q16)
    else:
        q1[...] = jnp.concatenate(q, axis=1).astype(BF16)
    zk = proj(3)
    va = _rms(_gelu(zva), gv_ref[...])
    k = head_norm_rope(zk, gk_ref)
    v = proj(4)
    if dilated:
        for p in range(N_SLABS):
            k1[:, _slab(p)] = k[p].astype(BF16)
        split(k, scrk, k4, k16)
        v1[...] = v.astype(BF16)
        split([v[:, _slab(p)] for p in range(N_SLABS)], scr1, v4, v16)
    else:
        va_ref[...] = va
        kt_ref[...] = jnp.concatenate(k, axis=1).T
        vt_ref[...] = v.T

    lane_lo = lax.broadcasted_iota(jnp.int32, (CHUNK, LANES), 1) < HEAD_DIM
    n_chunks = rows // CHUNK
    side = min(2, n_chunks)
    ys = [[None] * N_SLABS for _ in range(n_chunks)]
    for c0 in range(0, n_chunks, side):
        for p in range(N_SLABS):
            parts = [va[c * CHUNK:(c + 1) * CHUNK, _slab(p)] for c in range(c0, c0 + side)]
            lo = jnp.concatenate([jnp.where(lane_lo, t, 0.0) for t in parts], axis=1)
            hi = jnp.concatenate([jnp.where(lane_lo, 0.0, t) for t in parts], axis=1)
            s = _dot(wg_ref[p], jnp.concatenate([lo, hi], axis=0).astype(BF16))
            for i in range(side):
                c = c0 + i
                ys[c][p] = u[c * CHUNK:(c + 1) * CHUNK, _slab(p)] * (s[:, _slab(i)] + bg_ref[:, _slab(p)])
    for c in range(n_chunks):
        ya_ref[c * CHUNK:(c + 1) * CHUNK, :] = _rms(jnp.concatenate(ys[c], axis=1), goa_ref[...]).astype(BF16)

    if dilated:
        @pl.when(pl.program_id(0) % tiles_per_seq >= tiles_per_seq - MAX_WINDOW // rows)
        def _():
            for p in range(N_SLABS):
                kwin_ref[_slab(p), :] = scrk[p].T
                vwin_ref[_slab(p), :] = scr1[p].T


def _in_proj(x, pw, layer, tables, gate_w, gate_b, *, rows, batch=None, seq=None, prev_win=None):
    n = x.shape[0]
    depth = pw["w_in"].shape[0]
    dilated = batch is not None
    tbase, toff, tsign = tables
    n_base = tbase.shape[0]
    row_spec = lambda w: pl.BlockSpec((rows, w), lambda i: (i, 0))
    lspec = lambda *shape: _layer_spec(shape, layer)
    flat = lambda dt: jax.ShapeDtypeStruct((n, 512), dt)
    scratch = []
    per_b = None
    in_specs = [row_spec(D_MODEL), lspec(1, D_MODEL), lspec(D_MODEL, 5 * 512),
                lspec(1, D_A), lspec(1, D_B), lspec(1, D_B),
                _const_spec((D_B // 2, D_B // 2)),
                pl.BlockSpec((None, 2, LANES), lambda i: (i % n_base, 0, 0)),
                _const_spec((2, rows, LANES)), _const_spec((2, LANES)),
                lspec(N_SLABS, CHUNK, 2 * CHUNK), lspec(CHUNK, D_A), lspec(1, D_A)]
    args = [x, pw["g_attn"], pw["w_in"], pw["g_v"], pw["g_q"], pw["g_k"], pw["hsum"], tbase, toff,
            tsign, gate_w, gate_b, pw["g_oa"]]
    aliases = {}
    if dilated:
        per_b = seq // rows
        split_shape = lambda d, dt: jax.ShapeDtypeStruct((batch, d, seq // d, D_B), dt)
        split_spec = lambda d: pl.BlockSpec((None, d, rows // d, D_B),
                                            lambda i: (i // per_b, 0, i % per_b, 0))
        kv_shape = [flat(BF16), split_shape(4, BF16), split_shape(16, BF16)]
        kv_spec = [row_spec(512), split_spec(4), split_spec(16)]
        first_win = per_b - MAX_WINDOW // rows
        win_shape = jax.ShapeDtypeStruct((depth, batch, D_B, MAX_WINDOW), F32)
        win_spec = pl.BlockSpec((None, None, D_B, rows),
                                lambda i: (layer, i // per_b, 0, jnp.maximum(i % per_b - first_win, 0)))
        out_shape = [flat(BF16), split_shape(16, F32)] + kv_shape * 2 + [win_shape] * 2
        out_specs = [row_spec(512), split_spec(16)] + kv_spec * 2 + [win_spec] * 2
        scratch = [pltpu.VMEM((N_SLABS, rows, LANES), F32)] * 3
        if prev_win is not None:
            in_specs += [pl.BlockSpec(memory_space=pl.ANY)] * 2
            aliases = {len(args): len(out_shape) - 2, len(args) + 1: len(out_shape) - 1}
            args += list(prev_win)
    else:
        t_shape = jax.ShapeDtypeStruct((512, n), F32)
        t_spec = pl.BlockSpec((512, rows), lambda i: (0, i))
        out_shape = [flat(BF16), flat(BF16), t_shape, t_shape, flat(F32)]
        out_specs = [row_spec(512), row_spec(512), t_spec, t_spec, row_spec(512)]
    return pl.pallas_call(
        functools.partial(_in_proj_body, rows=rows, tiles_per_seq=per_b, n_in=len(args)),
        grid=(n // rows,),
        in_specs=in_specs,
        out_specs=out_specs,
        out_shape=out_shape,
        scratch_shapes=scratch,
        input_output_aliases=aliases,
        compiler_params=_params(1),
        name="in_proj",
    )(*args)


def _band_block(q, kp, kc, vp, vc, bias):
    lane_lo = lax.broadcasted_iota(jnp.int32, (BAND, LANES), 1) < HEAD_DIM
    head_mask = [(lax.broadcasted_iota(jnp.int32, (1, LANES), 1) // HEAD_DIM == hh)
                 .astype(F32).astype(BF16) for hh in range(2)]
    ones = jnp.ones((2 * BAND, LANES), BF16)
    pick = lambda t: jnp.where(lane_lo, t[:BAND], t[BAND:])
    out = []
    for p in range(N_SLABS):
        qp = q[:, _slab(p)]
        kk = jnp.concatenate([kp[:, _slab(p)], kc[:, _slab(p)]], axis=0)
        vv = jnp.concatenate([vp[:, _slab(p)], vc[:, _slab(p)]], axis=0)
        qq = jnp.concatenate([qp * head_mask[0], qp * head_mask[1]], axis=0)
        s = _dot_nt(qq, kk) + bias
        m = jnp.max(s, axis=-1, keepdims=True)
        pr = jnp.exp2(s - m).astype(BF16)
        a = _dot(pr, jnp.concatenate([vv, ones], axis=1))
        l = pick(a[:, LANES:])
        out.append((pick(a[:, :LANES]) * (1.0 / l), pick(m) + jnp.log2(l)))
    return out


def _attn_prompt_body(q1, k1c, k1p, v1c, v1p, q4, k4c, k4p, v4c, v4p, q16, k16c, k16p, v16c, v16p,
                      bias_ref, gob_ref, *rest):
    n_w = (len(rest) - 7) // 2
    w_f32, out_ref, w_bf16 = rest[:n_w], rest[n_w], rest[n_w + 1:2 * n_w + 1]
    o1, s1, o4, s4, o16, s16 = rest[2 * n_w + 1:]
    for src, dst in zip(w_f32, w_bf16):
        dst[...] = src[...].astype(BF16)
    j = pl.program_id(1)
    g = pl.program_id(2)
    n_groups = pl.num_programs(2)

    def run(branch, q, kc, kp, vc, vp, first, refs, runs):
        bias = bias_ref[branch, first] if isinstance(first, int) else \
            bias_ref[branch, first.astype(jnp.int32)]
        res = _band_block(q.reshape(BAND, D_B).astype(BF16), kp, kc, vp, vc, bias)
        n = BAND // len(runs)
        for p in range(N_SLABS):
            for ref, val in zip(refs, res[p]):
                for i, row0 in enumerate(runs):
                    ref[p, pl.ds(pl.multiple_of(row0, n), n), :] = val[i * n:(i + 1) * n]

    for i in range(SUB):
        c = SUB * g + i
        blk = slice(i * BAND, (i + 1) * BAND)
        if i == 0:
            kp, vp, first = k1p[...], v1p[...], (n_groups * j + g) == 0
        else:
            before = slice((i - 1) * BAND, i * BAND)
            kp, vp, first = k1c[before, :], v1c[before, :], 0
        run(0, q1[:, 8 * i:8 * (i + 1), :], k1c[blk, :], kp, v1c[blk, :], vp, first, (o1, s1),
            [BAND * r + 8 * c for r in range(16)])
        run(1, q4[:, i], k4c[i], k4p[i], v4c[i], v4p[i], (n_groups * j + g) == 0, (o4, s4),
            [BAND * (4 * cc + i) + 32 * g for cc in range(4)])
        run(2, q16[i], k16c[i], k16p[i], v16c[i], v16p[i], j == 0, (o16, s16), [BAND * c])

    @pl.when(g == n_groups - 1)
    def _finalize():
        parts = [(o1, s1), (o4, s4), (o16, s16)]
        for r in range(16):
            rows = pl.ds(r * BAND, BAND)
            slabs = []
            for p in range(N_SLABS):
                lse = [s_ref[p, rows, :] for (_, s_ref) in parts]
                top = jnp.maximum(jnp.maximum(lse[0], lse[1]), lse[2])
                num = den = None
                for e, (o_ref, _) in zip(lse, parts):
                    w = jnp.exp2(e - top)
                    t = o_ref[p, rows, :] * w
                    num, den = (t, w) if num is None else (num + t, den + w)
                slabs.append(num / den)
            y = _rms(jnp.concatenate(slabs, axis=1), gob_ref[...])
            for p in range(N_SLABS):
                o16[p, rows, :] = y[:, _slab(p)]
        for p in range(N_SLABS):
            for r in range(16):
                o1[p, pl.ds(r, BAND, stride=16), :] = o16[p, r * BAND:(r + 1) * BAND, :]
            out_ref[:, _slab(p)] = o1[p].astype(BF16)


def _attn_prompt(q16, kv, bias, g_ob, layer, batch, seq, weights):
    assert SUB == 4
    n_tiles = seq // TILE
    n_groups = 16 // SUB
    span = SUB * BAND
    prev = lambda i: jnp.maximum(i - 1, 0)
    grp = lambda j, g: n_groups * j + g
    branches = [
        (pl.BlockSpec((None, 16, span // 16, D_B), lambda b, j, g: (b, 0, grp(j, g), 0)), q16,
         ((None, span, D_B), lambda b, j, g: (b, grp(j, g), 0)),
         ((None, BAND, D_B), lambda b, j, g: (b, prev(SUB * grp(j, g)), 0))),
        (pl.BlockSpec((None, 4, SUB, span // 16, D_B), lambda b, j, g: (b, 0, 0, grp(j, g), 0)),
         q16.reshape(batch, 4, 4, seq // 16, D_B),
         ((None, SUB, BAND, D_B), lambda b, j, g: (b, 0, grp(j, g), 0)),
         ((None, SUB, BAND, D_B), lambda b, j, g: (b, 0, prev(grp(j, g)), 0))),
        (pl.BlockSpec((None, SUB, BAND, D_B), lambda b, j, g: (b, g, j, 0)), q16,
         ((None, SUB, BAND, D_B), lambda b, j, g: (b, g, j, 0)),
         ((None, SUB, BAND, D_B), lambda b, j, g: (b, g, prev(j), 0))),
    ]
    in_specs, args = [], []
    for di, (q_spec, q_arr, cur, prv) in enumerate(branches):
        cs, ps = pl.BlockSpec(*cur), pl.BlockSpec(*prv)
        in_specs += [q_spec, cs, ps, cs, ps]
        kd, vd = (t[di] if di else t[di].reshape(batch, seq, D_B) for t in kv)
        args += [q_arr, kd, kd, vd, vd]
    in_specs += [_const_spec((3, 2, 2 * BAND, 2 * BAND)), _layer_spec((1, D_B), layer)]
    args += [bias, g_ob]
    n_steps = batch * n_tiles * n_groups
    step = lambda b, j, g: (b * n_tiles + j) * n_groups + g
    w_out_specs, w_out_shapes = [], []
    for w in weights:
        rows, cols = w.shape[1] // n_steps, w.shape[2]
        in_specs.append(pl.BlockSpec((None, rows, cols), lambda b, j, g: (layer, step(b, j, g), 0)))
        w_out_specs.append(pl.BlockSpec((rows, cols), lambda b, j, g: (step(b, j, g), 0)))
        w_out_shapes.append(jax.ShapeDtypeStruct(w.shape[1:], BF16))
    args += list(weights)
    out, *w_bf16 = pl.pallas_call(
        _attn_prompt_body,
        grid=(batch, n_tiles, n_groups),
        in_specs=in_specs,
        out_specs=[pl.BlockSpec((None, TILE, D_B), lambda b, j, g: (b, j, 0))] + w_out_specs,
        out_shape=[jax.ShapeDtypeStruct((batch, seq, D_B), BF16)] + w_out_shapes,
        scratch_shapes=[pltpu.VMEM((N_SLABS, TILE, LANES), F32)] * 6,
        compiler_params=_params(3),
        name="attn_prompt",
    )(*args)
    return out.reshape(batch * seq, D_B), w_bf16


def _band_bias():
    kj = np.arange(2 * BAND)[None, :]
    rho = np.arange(BAND)
    out = []
    for runs in (16, 4, 1):
        per = BAND // runs
        qi = (runs * (rho % per) + rho // per)[:, None]
        ok = (kj >= qi) & (kj <= qi + BAND)
        out.append(np.stack([np.where(ok, 0.0, -np.inf), np.where(ok & (kj >= BAND), 0.0, -np.inf)]))
    return np.tile(np.stack(out), (1, 1, 2, 1)).astype(np.float32)


def _window_attention(kt_ref, vt_ref, kn_ref, vn_ref, q_ref, mult_ref, okt_ref, ovt_ref, o_ref,
                      seq_idx):
    n_heads = kt_ref.shape[0]
    n_feat = n_heads * HEAD_DIM
    stack = lambda w_ref, n_ref: jnp.concatenate(
        [w_ref[...].reshape(n_feat, MAX_WINDOW), n_ref[...].reshape(n_feat, LANES)], axis=1).astype(BF16)
    state = {}

    def scores():
        own = lax.broadcasted_iota(jnp.int32, (Q_ROWS * n_heads, LANES), 1) // T_NEW == seq_idx
        mult = jnp.concatenate([mult_ref[:, 0:MAX_WINDOW],
                                jnp.where(own, mult_ref[:, MAX_WINDOW:], 0.0)], axis=1)
        s = jnp.where(mult > 0, _dot(q_ref[...], stack(kt_ref, kn_ref)), -jnp.inf)
        state["p"] = mult * jnp.exp(s - jnp.max(s, axis=-1, keepdims=True))

    def output():
        p = state["p"]
        o_ref[...] = _dot_nt(p.astype(BF16), stack(vt_ref, vn_ref)) / jnp.sum(p, axis=-1, keepdims=True)

    def shift():
        body = MAX_WINDOW - LANES
        new_shift = (LANES - T_NEW * (seq_idx + 1)) % LANES
        is_new = lax.broadcasted_iota(jnp.int32, (HEAD_DIM, LANES), 1) >= LANES - T_NEW
        for src_ref, new_ref, dst_ref in ((kt_ref, kn_ref, okt_ref), (vt_ref, vn_ref, ovt_ref)):
            for h in range(n_heads):
                y = pltpu.roll(src_ref[h], MAX_WINDOW - T_NEW, 1)
                dst_ref[h, :, 0:body] = y[:, 0:body]
                dst_ref[h, :, body:MAX_WINDOW] = jnp.where(
                    is_new, pltpu.roll(new_ref[h], new_shift, 1), y[:, body:])

    return scores, output, shift


def _sample_multiplicity(n_rows):
    def mult(d):
        ok = d >= 0
        return (ok & (d <= 128)).astype(np.int32) + (ok & (d % 4 == 0) & (d <= 512)) \
            + (ok & (d % 16 == 0) & (d <= 2048))
    t = (np.arange(n_rows) % Q_ROWS % T_NEW)[:, None]
    old = mult(MAX_WINDOW + t - np.arange(MAX_WINDOW)[None, :])
    new = mult(t - (np.arange(LANES) % T_NEW)[None, :])
    return np.concatenate([old, new], axis=1).astype(np.float32)


def _ffn_compute(x_ref, ya_ref, yb_ref, gob_ref, wout_ref, gffn_ref, wup_ref, wdown_ref, o_ref,
                 norm_b, between=lambda stage: None):
    yb = yb_ref[...]
    if norm_b:
        yb = _rms(yb, gob_ref[...])
    x1 = x_ref[...] + _dot(ya_ref[...], wout_ref[0:D_A, :]) \
        + _dot(yb.astype(BF16), wout_ref[D_A:D_A + D_B, :])
    hb = _rms(x1, gffn_ref[...]).astype(BF16)
    between(0)
    step = D_FF // 4
    mlp = None
    for c in range(4):
        f = _dot(hb, wup_ref[:, c * step:(c + 1) * step])
        f = jnp.square(jnp.maximum(f, 0.0)).astype(BF16)
        d = _dot(f, wdown_ref[c * step:(c + 1) * step, :])
        mlp = d if mlp is None else mlp + d
        between(c + 1)
    o_ref[...] = x1 + mlp


def _out_ffn_body(*refs, norm_b):
    _ffn_compute(*refs, norm_b)


def _ffn_window_body(*refs, steps_per_seq):
    n_ffn = 8
    ffn_in, (kt, vt, kn, vn, q, mult) = refs[:n_ffn], refs[n_ffn:n_ffn + 6]
    x_out, okt, ovt, o_att = refs[-4:]
    seq_idx = pl.program_id(0) // steps_per_seq
    scores, output, shift = _window_attention(kt, vt, kn, vn, q, mult, okt, ovt, o_att, seq_idx)
    stages = {0: scores, 2: output, 4: shift}
    _ffn_compute(*ffn_in, x_out, False, between=lambda stage: stages.get(stage, lambda: None)())


def _ffn_specs(rows, layer):
    row_spec = lambda w: pl.BlockSpec((rows, w), lambda i: (i, 0))
    lspec = lambda *shape: _layer_spec(shape, layer)
    return [row_spec(D_MODEL), row_spec(D_A), row_spec(D_B), lspec(1, D_B),
            _const_spec((D_A + D_B, D_MODEL)), lspec(1, D_MODEL),
            _const_spec((D_MODEL, D_FF)), _const_spec((D_FF, D_MODEL))], row_spec(D_MODEL)


def _out_ffn(x, ya, yb, pw, layer, weights, *, rows, norm_b):
    n = x.shape[0]
    in_specs, out_spec = _ffn_specs(rows, layer)
    return pl.pallas_call(
        functools.partial(_out_ffn_body, norm_b=norm_b),
        grid=(n // rows,),
        in_specs=in_specs,
        out_specs=out_spec,
        out_shape=jax.ShapeDtypeStruct((n, D_MODEL), F32),
        compiler_params=_params(1),
        name="out_ffn",
    )(x, ya, yb, pw["g_ob"], weights[0], pw["g_ffn"], weights[1], weights[2])


def _ffn_window(x, ya, yb, pw, layer, weights, cache_kt, cache_vt, k_new, v_new, q, mult, prev_out,
                *, rows):
    n = x.shape[0]
    steps = n // rows
    depth, db = cache_kt.shape[:2]
    per_seq = steps // db
    hps = N_HEADS // per_seq
    assert per_seq * db == steps and hps * per_seq == N_HEADS
    assert q.shape == (db, per_seq, Q_ROWS * hps, HEAD_DIM * hps)
    in_specs, out_spec = _ffn_specs(rows, layer)
    cache_spec = pl.BlockSpec((None, None, hps, HEAD_DIM, MAX_WINDOW),
                              lambda i: (layer, i // per_seq, i % per_seq, 0, 0))
    new_spec = pl.BlockSpec((hps, HEAD_DIM, LANES), lambda i: (i % per_seq, 0, 0))
    q_spec = pl.BlockSpec((None, None) + q.shape[2:], lambda i: (i // per_seq, i % per_seq, 0, 0))
    in_specs += [cache_spec, cache_spec, new_spec, new_spec, q_spec,
                 _const_spec((Q_ROWS * hps, MAX_WINDOW + LANES))]
    args = [x, ya, yb, pw["g_ob"], weights[0], pw["g_ffn"], weights[1], weights[2],
            cache_kt, cache_vt, k_new, v_new, q, mult]
    aliases = {}
    if prev_out is not None:
        in_specs += [pl.BlockSpec(memory_space=pl.ANY)] * 2
        aliases = {len(args): 1, len(args) + 1: 2}
        args += list(prev_out)
    win_shape = jax.ShapeDtypeStruct(cache_kt.shape, F32)
    return pl.pallas_call(
        functools.partial(_ffn_window_body, steps_per_seq=per_seq),
        grid=(steps,),
        in_specs=in_specs,
        out_specs=[out_spec, cache_spec, cache_spec, q_spec],
        out_shape=[jax.ShapeDtypeStruct((n, D_MODEL), F32), win_shape, win_shape,
                   jax.ShapeDtypeStruct(q.shape, F32)],
        input_output_aliases=aliases,
        compiler_params=_params(1),
        name="ffn_window",
    )(*args)


def _rope_tables(base_pos, offsets):
    half = ROT_DIM // 2
    inv = jnp.power(ROPE_THETA, -jnp.arange(half, dtype=F32) * (2.0 / ROT_DIM))
    lane = np.arange(LANES) % HEAD_DIM
    freq = jnp.where(lane < ROT_DIM, inv[lane % half], 0.0)
    ang = lambda pos: pos.astype(F32)[:, None] * freq[None, :]
    ab, ao = ang(base_pos), ang(offsets)
    tbase = jnp.stack([jnp.cos(ab), jnp.sin(ab)], axis=1)
    toff = jnp.stack([jnp.cos(ao), jnp.sin(ao)], axis=0)
    tsign = jnp.asarray(np.stack([-1.0 * (lane < half), 1.0 * ((lane >= half) & (lane < ROT_DIM))]), F32)
    return tbase, toff, tsign


def kernel(x_prompt, x_sample, cache_win_k, cache_win_v, attn_norm_g, w_in, v_norm_g, w_spatial, b_spatial, q_norm_g, k_norm_g, out_norm_a_g, out_norm_b_g, w_out, ffn_norm_g, w_up, w_down):
    depth = w_in.shape[0]
    batch, seq, _ = x_prompt.shape
    db, t_new, _ = x_sample.shape
    n_s = db * t_new
    assert n_s == CHUNK and t_new == T_NEW and cache_win_k.shape[2] == MAX_WINDOW
    assert seq % TILE == 0

    tab_p = _rope_tables(jnp.arange(seq // ROWS_IN_PROJ) * ROWS_IN_PROJ, jnp.arange(ROWS_IN_PROJ))
    tab_s = _rope_tables(jnp.full((1,), PAST_LEN), jnp.arange(n_s) % t_new)

    tril = jnp.tril(jnp.ones((CHUNK, CHUNK), bool))
    hsum = jnp.kron(jnp.eye(N_HEADS // 2, dtype=F32), jnp.full((HEAD_DIM, HEAD_DIM), 1.0 / HEAD_DIM, F32)).astype(BF16)
    row = lambda g: g[:, None, :]
    per_head = lambda g: jnp.tile(g, (1, N_HEADS))
    pw = dict(g_attn=row(attn_norm_g), w_in=w_in.astype(BF16), g_v=row(v_norm_g),
              g_q=row(per_head(q_norm_g)), g_k=row(per_head(k_norm_g)), hsum=hsum,
              g_oa=row(out_norm_a_g), g_ob=row(out_norm_b_g), g_ffn=row(ffn_norm_g))
    pair = lambda w: w.reshape(depth, N_SLABS, 2, CHUNK, CHUNK).transpose(0, 1, 3, 2, 4) \
        .reshape(depth, N_SLABS, CHUNK, 2 * CHUNK).astype(BF16)
    gate_w_p = pair(jnp.where(tril, w_spatial, 0))
    gate_b_p = jnp.repeat(b_spatial.transpose(0, 2, 1), HEAD_DIM, axis=2)
    t_of = np.arange(CHUNK) % t_new
    same_seq = (np.arange(CHUNK)[:, None] // t_new) == (np.arange(CHUNK)[None, :] // t_new)
    causal_s = jnp.asarray(same_seq & (t_of[:, None] >= t_of[None, :]))
    onehot = jnp.asarray(t_of[:, None] == np.arange(t_new)[None, :], F32)
    w_rep = jnp.einsum("ia,lhab,jb->lhij", onehot, w_spatial[:, :, :t_new, :t_new], onehot,
                       precision=lax.Precision.HIGHEST)
    gate_w_s = pair(jnp.where(causal_s, w_rep, 0))
    b_rep = jnp.einsum("ia,lha->lih", onehot, b_spatial[:, :, :t_new], precision=lax.Precision.HIGHEST)
    gate_b_s = jnp.repeat(b_rep, HEAD_DIM, axis=2)

    bias = jnp.asarray(_band_bias())
    per_seq = batch * seq // ROWS_FFN // db
    hps = N_HEADS // per_seq
    mult = jnp.asarray(_sample_multiplicity(Q_ROWS * hps))
    eye_h = jnp.eye(hps, dtype=F32)[None, None, :, None, :, None]

    cache_kt = cache_win_k.transpose(0, 1, 3, 4, 2)
    cache_vt = cache_win_v.transpose(0, 1, 3, 4, 2)

    xp = x_prompt.reshape(batch * seq, D_MODEL)
    xs = x_sample.reshape(n_s, D_MODEL)
    cv_list = []
    win_p = win_s = None
    for l in range(depth):
        ya_p, q16, *kv, kwin, vwin = _in_proj(xp, pw, l, tab_p, gate_w_p, gate_b_p, rows=ROWS_IN_PROJ,
                                              batch=batch, seq=seq, prev_win=win_p)
        win_p = (kwin, vwin)
        yb_p, w_l = _attn_prompt(q16, (kv[0:3], kv[3:6]), bias, pw["g_ob"], l, batch, seq,
                                 (w_out, w_up, w_down))
        ya_s, q, kt, vt, va = _in_proj(xs, pw, l, tab_s, gate_w_s, gate_b_s, rows=CHUNK)
        cv_list.append(va.reshape(db, t_new, D_A))
        q8 = jnp.tile(q.reshape(db, t_new, per_seq, hps, HEAD_DIM).transpose(0, 2, 3, 1, 4), (1, 1, 1, 2, 1))
        q_blk = (q8[:, :, :, :, None, :] * eye_h.astype(BF16)).reshape(db, per_seq, Q_ROWS * hps, hps * HEAD_DIM)
        heads = lambda t: t.reshape(N_HEADS, HEAD_DIM, n_s)

        xp, okt, ovt, o = _ffn_window(xp, ya_p, yb_p, pw, l, w_l, cache_kt, cache_vt, heads(kt),
                                      heads(vt), q_blk, mult, win_s, rows=ROWS_FFN)
        win_s = (okt, ovt)
        o = (o.reshape(db, per_seq, hps, Q_ROWS, hps, HEAD_DIM) * eye_h).sum(axis=4)
        yb_s = o[:, :, :, :t_new].transpose(0, 3, 1, 2, 4).reshape(n_s, D_B)
        xs = _out_ffn(xs, ya_s, yb_s, pw, l, w_l, rows=CHUNK, norm_b=True)

    win_p = [w.reshape(depth, batch, N_HEADS, HEAD_DIM, MAX_WINDOW) for w in win_p]
    unt = lambda w: w.transpose(0, 1, 4, 2, 3)
    return (xp.reshape(batch, seq, D_MODEL), xs.reshape(db, t_new, D_MODEL),
            unt(win_p[0]), unt(win_p[1]), unt(win_s[0]), unt(win_s[1]),
            jnp.stack(cv_list))
```

```python
import functools

import numpy as np
import jax
import jax.numpy as jnp
from jax import lax
from jax.experimental import pallas as pl
from jax.experimental.pallas import tpu as pltpu

D_MODEL = 1024
D_A = 512
D_B = 512
HEAD_DIM = 64
N_HEADS = 8
CHUNK = 128
D_FF = 4 * D_MODEL
ROT_DIM = HEAD_DIM // 4
ROPE_THETA = 500000.0
EPS = 1e-6
ATTN_SCALE = HEAD_DIM ** -0.5
LOG2_E = float(np.log2(np.e))
BAND = 128
TILE = 16 * BAND
SUB = 4
MAX_WINDOW = 2048
PAST_LEN = 16384
T_NEW = 4
Q_ROWS = 8

ROWS_IN_PROJ = 512
ROWS_FFN = 256
N_IN_PROJ_INPUTS = 13

LANES = 128
N_SLABS = D_B // LANES
VMEM_LIMIT = 56 * 1024 * 1024

F32 = jnp.float32
BF16 = jnp.bfloat16


def _rms(x, g):
    return x * lax.rsqrt(jnp.mean(x * x, axis=-1, keepdims=True) + EPS) * g


def _gelu(x):
    c = np.float32(np.sqrt(2.0 / np.pi))
    return x * (0.5 * (1.0 + jnp.tanh(c * (x + 0.044715 * (x * x * x)))))


def _dot(a, b):
    return jnp.dot(a, b, preferred_element_type=F32)


def _dot_nt(a, b):
    return lax.dot_general(a, b, (((1,), (1,)), ((), ())), preferred_element_type=F32)


def _params(n_axes):
    return pltpu.CompilerParams(dimension_semantics=("arbitrary",) * n_axes,
                                vmem_limit_bytes=VMEM_LIMIT)


def _const_spec(shape):
    return pl.BlockSpec(shape, lambda *_: (0,) * len(shape), pipeline_mode=pl.Buffered(1))


def _layer_spec(shape, layer):
    return pl.BlockSpec((None,) + shape, lambda *_: (layer,) + (0,) * len(shape),
                        pipeline_mode=pl.Buffered(1))


def _slab(p):
    return slice(p * LANES, (p + 1) * LANES)


def _in_proj_body(x_ref, gattn_ref, win_ref, gv_ref, gq_ref, gk_ref, hsum_ref, tbase_ref, toff_ref,
                  tsign_ref, wg_ref, bg_ref, goa_ref, *outs, rows, tiles_per_seq, n_in):
    dilated = tiles_per_seq is not None
    outs = outs[n_in - N_IN_PROJ_INPUTS:]
    hb = _rms(x_ref[...], gattn_ref[...]).astype(BF16)

    cb, sb = tbase_ref[0:1, :], tbase_ref[1:2, :]
    co, so = toff_ref[0], toff_ref[1]
    cos = cb * co - sb * so
    sin = sb * co + cb * so
    sin_up, sin_dn = sin * tsign_ref[0:1, :], sin * tsign_ref[1:2, :]

    def proj(i):
        return _dot(hb, win_ref[:, i * 512:(i + 1) * 512])

    def head_norm_rope(z, g_ref, scale=1.0):
        zz = (z * z).astype(BF16)
        half = D_B // 2
        ms = jnp.concatenate([_dot(zz[:, :half], hsum_ref[...]), _dot(zz[:, half:], hsum_ref[...])],
                             axis=1)
        zn = z * lax.rsqrt(ms + EPS) * g_ref[...]
        slabs = []
        for p in range(N_SLABS):
            s = zn[:, _slab(p)]
            slabs.append((s * cos + pltpu.roll(s, LANES - ROT_DIM // 2, 1) * sin_up
                          + pltpu.roll(s, ROT_DIM // 2, 1) * sin_dn) * scale)
        return slabs

    if dilated:
        ya_ref, q16, k1, k4, k16, v1, v4, v16, kwin_ref, vwin_ref, scr1, scrk, scr4 = outs

        def split(slabs, src, r4_ref, r16_ref):
            for p in range(N_SLABS):
                src[p] = slabs[p]
            n4, n16 = rows // 4, rows // 16
            for p in range(N_SLABS):
                for r in range(4):
                    part = src[p, pl.ds(r, n4, stride=4), :]
                    scr4[p, r * n4:(r + 1) * n4, :] = part
                    if r4_ref is not None:
                        r4_ref[r, :, _slab(p)] = part.astype(BF16)
                for r in range(4):
                    for c in range(4):
                        part = scr4[p, pl.ds(r * n4 + c, n16, stride=4), :]
                        r16_ref[r + 4 * c, :, _slab(p)] = part.astype(r16_ref.dtype)
    else:
        ya_ref, q1, kt_ref, vt_ref, va_ref = outs

    zu = proj(0)
    zq = proj(2)
    u = _gelu(zu)
    q = head_norm_rope(zq, gq_ref, ATTN_SCALE * LOG2_E if dilated else ATTN_SCALE)
    zva = proj(1)
    if dilated:
        split(q, scr1, None, q16)
    else:
        q1[...] = jnp.concatenate(q, axis=1).astype(BF16)
    zk = proj(3)
    va = _rms(_gelu(zva), gv_ref[...])
    k = head_norm_rope(zk, gk_ref)
    v = proj(4)
    if dilated:
        for p in range(N_SLABS):
            k1[:, _slab(p)] = k[p].astype(BF16)
        split(k, scrk, k4, k16)
        v1[...] = v.astype(BF16)
        split([v[:, _slab(p)] for p in range(N_SLABS)], scr1, v4, v16)
    else:
        va_ref[...] = va
        kt_ref[...] = jnp.concatenate(k, axis=1).T
        vt_ref[...] = v.T

    lane_lo = lax.broadcasted_iota(jnp.int32, (CHUNK, LANES), 1) < HEAD_DIM
    n_chunks = rows // CHUNK
    side = min(2, n_chunks)
    ys = [[None] * N_SLABS for _ in range(n_chunks)]
    for c0 in range(0, n_chunks, side):
        for p in range(N_SLABS):
            parts = [va[c * CHUNK:(c + 1) * CHUNK, _slab(p)] for c in range(c0, c0 + side)]
            lo = jnp.concatenate([jnp.where(lane_lo, t, 0.0) for t in parts], axis=1)
            hi = jnp.concatenate([jnp.where(lane_lo, 0.0, t) for t in parts], axis=1)
            s = _dot(wg_ref[p], jnp.concatenate([lo, hi], axis=0).astype(BF16))
            for i in range(side):
                c = c0 + i
                ys[c][p] = u[c * CHUNK:(c + 1) * CHUNK, _slab(p)] * (s[:, _slab(i)] + bg_ref[:, _slab(p)])
    for c in range(n_chunks):
        ya_ref[c * CHUNK:(c + 1) * CHUNK, :] = _rms(jnp.concatenate(ys[c], axis=1), goa_ref[...]).astype(BF16)

    if dilated:
        @pl.when(pl.program_id(0) % tiles_per_seq >= tiles_per_seq - MAX_WINDOW // rows)
        def _():
            for p in range(N_SLABS):
                kwin_ref[_slab(p), :] = scrk[p].T
                vwin_ref[_slab(p), :] = scr1[p].T


def _in_proj(x, pw, layer, tables, gate_w, gate_b, *, rows, batch=None, seq=None, prev_win=None):
    n = x.shape[0]
    depth = pw["w_in"].shape[0]
    dilated = batch is not None
    tbase, toff, tsign = tables
    n_base = tbase.shape[0]
    row_spec = lambda w: pl.BlockSpec((rows, w), lambda i: (i, 0))
    lspec = lambda *shape: _layer_spec(shape, layer)
    flat = lambda dt: jax.ShapeDtypeStruct((n, 512), dt)
    scratch = []
    per_b = None
    in_specs = [row_spec(D_MODEL), lspec(1, D_MODEL), lspec(D_MODEL, 5 * 512),
                lspec(1, D_A), lspec(1, D_B), lspec(1, D_B),
                _const_spec((D_B // 2, D_B // 2)),
                pl.BlockSpec((None, 2, LANES), lambda i: (i % n_base, 0, 0)),
                _const_spec((2, rows, LANES)), _const_spec((2, LANES)),
                lspec(N_SLABS, CHUNK, 2 * CHUNK), lspec(CHUNK, D_A), lspec(1, D_A)]
    args = [x, pw["g_attn"], pw["w_in"], pw["g_v"], pw["g_q"], pw["g_k"], pw["hsum"], tbase, toff,
            tsign, gate_w, gate_b, pw["g_oa"]]
    aliases = {}
    if dilated:
        per_b = seq // rows
        split_shape = lambda d, dt: jax.ShapeDtypeStruct((batch, d, seq // d, D_B), dt)
        split_spec = lambda d: pl.BlockSpec((None, d, rows // d, D_B),
                                            lambda i: (i // per_b, 0, i % per_b, 0))
        kv_shape = [flat(BF16), split_shape(4, BF16), split_shape(16, BF16)]
        kv_spec = [row_spec(512), split_spec(4), split_spec(16)]
        first_win = per_b - MAX_WINDOW // rows
        win_shape = jax.ShapeDtypeStruct((depth, batch, D_B, MAX_WINDOW), F32)
        win_spec = pl.BlockSpec((None, None, D_B, rows),
                                lambda i: (layer, i // per_b, 0, jnp.maximum(i % per_b - first_win, 0)))
        out_shape = [flat(BF16), split_shape(16, F32)] + kv_shape * 2 + [win_shape] * 2
        out_specs = [row_spec(512), split_spec(16)] + kv_spec * 2 + [win_spec] * 2
        scratch = [pltpu.VMEM((N_SLABS, rows, LANES), F32)] * 3
        if prev_win is not None:
            in_specs += [pl.BlockSpec(memory_space=pl.ANY)] * 2
            aliases = {len(args): len(out_shape) - 2, len(args) + 1: len(out_shape) - 1}
            args += list(prev_win)
    else:
        t_shape = jax.ShapeDtypeStruct((512, n), F32)
        t_spec = pl.BlockSpec((512, rows), lambda i: (0, i))
        out_shape = [flat(BF16), flat(BF16), t_shape, t_shape, flat(F32)]
        out_specs = [row_spec(512), row_spec(512), t_spec, t_spec, row_spec(512)]
    return pl.pallas_call(
        functools.partial(_in_proj_body, rows=rows, tiles_per_seq=per_b, n_in=len(args)),
        grid=(n // rows,),
        in_specs=in_specs,
        out_specs=out_specs,
        out_shape=out_shape,
        scratch_shapes=scratch,
        input_output_aliases=aliases,
        compiler_params=_params(1),
        name="in_proj",
    )(*args)


def _band_block(q, kp, kc, vp, vc, bias):
    lane_lo = lax.broadcasted_iota(jnp.int32, (BAND, LANES), 1) < HEAD_DIM
    head_mask = [(lax.broadcasted_iota(jnp.int32, (1, LANES), 1) // HEAD_DIM == hh)
                 .astype(F32).astype(BF16) for hh in range(2)]
    ones = jnp.ones((2 * BAND, LANES), BF16)
    pick = lambda t: jnp.where(lane_lo, t[:BAND], t[BAND:])
    out = []
    for p in range(N_SLABS):
        qp = q[:, _slab(p)]
        kk = jnp.concatenate([kp[:, _slab(p)], kc[:, _slab(p)]], axis=0)
        vv = jnp.concatenate([vp[:, _slab(p)], vc[:, _slab(p)]], axis=0)
        qq = jnp.concatenate([qp * head_mask[0], qp * head_mask[1]], axis=0)
        s = _dot_nt(qq, kk) + bias
        m = jnp.max(s, axis=-1, keepdims=True)
        pr = jnp.exp2(s - m).astype(BF16)
        a = _dot(pr, jnp.concatenate([vv, ones], axis=1))
        l = pick(a[:, LANES:])
        out.append((pick(a[:, :LANES]) * (1.0 / l), pick(m) + jnp.log2(l)))
    return out


def _attn_prompt_body(q1, k1c, k1p, v1c, v1p, q4, k4c, k4p, v4c, v4p, q16, k16c, k16p, v16c, v16p,
                      bias_ref, gob_ref, *rest):
    n_w = (len(rest) - 7) // 2
    w_f32, out_ref, w_bf16 = rest[:n_w], rest[n_w], rest[n_w + 1:2 * n_w + 1]
    o1, s1, o4, s4, o16, s16 = rest[2 * n_w + 1:]
    for src, dst in zip(w_f32, w_bf16):
        dst[...] = src[...].astype(BF16)
    j = pl.program_id(1)
    g = pl.program_id(2)
    n_groups = pl.num_programs(2)

    def run(branch, q, kc, kp, vc, vp, first, refs, runs):
        bias = bias_ref[branch, first] if isinstance(first, int) else \
            bias_ref[branch, first.astype(jnp.int32)]
        res = _band_block(q.reshape(BAND, D_B).astype(BF16), kp, kc, vp, vc, bias)
        n = BAND // len(runs)
        for p in range(N_SLABS):
            for ref, val in zip(refs, res[p]):
                for i, row0 in enumerate(runs):
                    ref[p, pl.ds(pl.multiple_of(row0, n), n), :] = val[i * n:(i + 1) * n]

    for i in range(SUB):
        c = SUB * g + i
        blk = slice(i * BAND, (i + 1) * BAND)
        if i == 0:
            kp, vp, first = k1p[...], v1p[...], (n_groups * j + g) == 0
        else:
            before = slice((i - 1) * BAND, i * BAND)
            kp, vp, first = k1c[before, :], v1c[before, :], 0
        run(0, q1[:, 8 * i:8 * (i + 1), :], k1c[blk, :], kp, v1c[blk, :], vp, first, (o1, s1),
            [BAND * r + 8 * c for r in range(16)])
        run(1, q4[:, i], k4c[i], k4p[i], v4c[i], v4p[i], (n_groups * j + g) == 0, (o4, s4),
            [BAND * (4 * cc + i) + 32 * g for cc in range(4)])
        run(2, q16[i], k16c[i], k16p[i], v16c[i], v16p[i], j == 0, (o16, s16), [BAND * c])

    @pl.when(g == n_groups - 1)
    def _finalize():
        parts = [(o1, s1), (o4, s4), (o16, s16)]
        for r in range(16):
            rows = pl.ds(r * BAND, BAND)
            slabs = []
            for p in range(N_SLABS):
                lse = [s_ref[p, rows, :] for (_, s_ref) in parts]
                top = jnp.maximum(jnp.maximum(lse[0], lse[1]), lse[2])
                num = den = None
                for e, (o_ref, _) in zip(lse, parts):
                    w = jnp.exp2(e - top)
                    t = o_ref[p, rows, :] * w
                    num, den = (t, w) if num is None else (num + t, den + w)
                slabs.append(num / den)
            y = _rms(jnp.concatenate(slabs, axis=1), gob_ref[...])
            for p in range(N_SLABS):
                o16[p, rows, :] = y[:, _slab(p)]
        for p in range(N_SLABS):
            for r in range(16):
                o1[p, pl.ds(r, BAND, stride=16), :] = o16[p, r * BAND:(r + 1) * BAND, :]
            out_ref[:, _slab(p)] = o1[p].astype(BF16)


def _attn_prompt(q16, kv, bias, g_ob, layer, batch, seq, weights):
    assert SUB == 4
    n_tiles = seq // TILE
    n_groups = 16 // SUB
    span = SUB * BAND
    prev = lambda i: jnp.maximum(i - 1, 0)
    grp = lambda j, g: n_groups * j + g
    branches = [
        (pl.BlockSpec((None, 16, span // 16, D_B), lambda b, j, g: (b, 0, grp(j, g), 0)), q16,
         ((None, span, D_B), lambda b, j, g: (b, grp(j, g), 0)),
         ((None, BAND, D_B), lambda b, j, g: (b, prev(SUB * grp(j, g)), 0))),
        (pl.BlockSpec((None, 4, SUB, span // 16, D_B), lambda b, j, g: (b, 0, 0, grp(j, g), 0)),
         q16.reshape(batch, 4, 4, seq // 16, D_B),
         ((None, SUB, BAND, D_B), lambda b, j, g: (b, 0, grp(j, g), 0)),
         ((None, SUB, BAND, D_B), lambda b, j, g: (b, 0, prev(grp(j, g)), 0))),
        (pl.BlockSpec((None, SUB, BAND, D_B), lambda b, j, g: (b, g, j, 0)), q16,
         ((None, SUB, BAND, D_B), lambda b, j, g: (b, g, j, 0)),
         ((None, SUB, BAND, D_B), lambda b, j, g: (b, g, prev(j), 0))),
    ]
    in_specs, args = [], []
    for di, (q_spec, q_arr, cur, prv) in enumerate(branches):
        cs, ps = pl.BlockSpec(*cur), pl.BlockSpec(*prv)
        in_specs += [q_spec, cs, ps, cs, ps]
        kd, vd = (t[di] if di else t[di].reshape(batch, seq, D_B) for t in kv)
        args += [q_arr, kd, kd, vd, vd]
    in_specs += [_const_spec((3, 2, 2 * BAND, 2 * BAND)), _layer_spec((1, D_B), layer)]
    args += [bias, g_ob]
    n_steps = batch * n_tiles * n_groups
    step = lambda b, j, g: (b * n_tiles + j) * n_groups + g
    w_out_specs, w_out_shapes = [], []
    for w in weights:
        rows, cols = w.shape[1] // n_steps, w.shape[2]
        in_specs.append(pl.BlockSpec((None, rows, cols), lambda b, j, g: (layer, step(b, j, g), 0)))
        w_out_specs.append(pl.BlockSpec((rows, cols), lambda b, j, g: (step(b, j, g), 0)))
        w_out_shapes.append(jax.ShapeDtypeStruct(w.shape[1:], BF16))
    args += list(weights)
    out, *w_bf16 = pl.pallas_call(
        _attn_prompt_body,
        grid=(batch, n_tiles, n_groups),
        in_specs=in_specs,
        out_specs=[pl.BlockSpec((None, TILE, D_B), lambda b, j, g: (b, j, 0))] + w_out_specs,
        out_shape=[jax.ShapeDtypeStruct((batch, seq, D_B), BF16)] + w_out_shapes,
        scratch_shapes=[pltpu.VMEM((N_SLABS, TILE, LANES), F32)] * 6,
        compiler_params=_params(3),
        name="attn_prompt",
    )(*args)
    return out.reshape(batch * seq, D_B), w_bf16


def _band_bias():
    kj = np.arange(2 * BAND)[None, :]
    rho = np.arange(BAND)
    out = []
    for runs in (16, 4, 1):
        per = BAND // runs
        qi = (runs * (rho % per) + rho // per)[:, None]
        ok = (kj >= qi) & (kj <= qi + BAND)
        out.append(np.stack([np.where(ok, 0.0, -np.inf), np.where(ok & (kj >= BAND), 0.0, -np.inf)]))
    return np.tile(np.stack(out), (1, 1, 2, 1)).astype(np.float32)


def _window_attention(kt_ref, vt_ref, kn_ref, vn_ref, q_ref, mult_ref, okt_ref, ovt_ref, o_ref,
                      seq_idx):
    n_heads = kt_ref.shape[0]
    n_feat = n_heads * HEAD_DIM
    stack = lambda w_ref, n_ref: jnp.concatenate(
        [w_ref[...].reshape(n_feat, MAX_WINDOW), n_ref[...].reshape(n_feat, LANES)], axis=1).astype(BF16)
    state = {}

    def scores():
        own = lax.broadcasted_iota(jnp.int32, (Q_ROWS * n_heads, LANES), 1) // T_NEW == seq_idx
        mult = jnp.concatenate([mult_ref[:, 0:MAX_WINDOW],
                                jnp.where(own, mult_ref[:, MAX_WINDOW:], 0.0)], axis=1)
        s = jnp.where(mult > 0, _dot(q_ref[...], stack(kt_ref, kn_ref)), -jnp.inf)
        state["p"] = mult * jnp.exp(s - jnp.max(s, axis=-1, keepdims=True))

    def output():
        p = state["p"]
        o = _dot_nt(p.astype(BF16), stack(vt_ref, vn_ref)) / jnp.sum(p, axis=-1, keepdims=True)
        for h in range(n_heads):
            o_ref[h] = o[Q_ROWS * h:Q_ROWS * (h + 1), HEAD_DIM * h:HEAD_DIM * (h + 1)]

    def shift():
        body = MAX_WINDOW - LANES
        new_shift = (LANES - T_NEW * (seq_idx + 1)) % LANES
        is_new = lax.broadcasted_iota(jnp.int32, (HEAD_DIM, LANES), 1) >= LANES - T_NEW
        for src_ref, new_ref, dst_ref in ((kt_ref, kn_ref, okt_ref), (vt_ref, vn_ref, ovt_ref)):
            for h in range(n_heads):
                y = pltpu.roll(src_ref[h], MAX_WINDOW - T_NEW, 1)
                dst_ref[h, :, 0:body] = y[:, 0:body]
                dst_ref[h, :, body:MAX_WINDOW] = jnp.where(
                    is_new, pltpu.roll(new_ref[h], new_shift, 1), y[:, body:])

    return scores, output, shift


def _sample_multiplicity(n_rows):
    def mult(d):
        ok = d >= 0
        return (ok & (d <= 128)).astype(np.int32) + (ok & (d % 4 == 0) & (d <= 512)) \
            + (ok & (d % 16 == 0) & (d <= 2048))
    t = (np.arange(n_rows) % Q_ROWS % T_NEW)[:, None]
    old = mult(MAX_WINDOW + t - np.arange(MAX_WINDOW)[None, :])
    new = mult(t - (np.arange(LANES) % T_NEW)[None, :])
    return np.concatenate([old, new], axis=1).astype(np.float32)


def _ffn_compute(x_ref, ya_ref, yb_ref, gob_ref, wout_ref, gffn_ref, wup_ref, wdown_ref, o_ref,
                 norm_b, between=lambda stage: None):
    yb = yb_ref[...]
    if norm_b:
        yb = _rms(yb, gob_ref[...])
    x1 = x_ref[...] + _dot(ya_ref[...], wout_ref[0:D_A, :]) \
        + _dot(yb.astype(BF16), wout_ref[D_A:D_A + D_B, :])
    hb = _rms(x1, gffn_ref[...]).astype(BF16)
    between(0)
    step = D_FF // 4
    mlp = None
    for c in range(4):
        f = _dot(hb, wup_ref[:, c * step:(c + 1) * step])
        f = jnp.square(jnp.maximum(f, 0.0)).astype(BF16)
        d = _dot(f, wdown_ref[c * step:(c + 1) * step, :])
        mlp = d if mlp is None else mlp + d
        between(c + 1)
    o_ref[...] = x1 + mlp


def _out_ffn_body(*refs, norm_b):
    _ffn_compute(*refs, norm_b)


def _ffn_window_body(*refs, steps_per_seq):
    n_ffn = 8
    ffn_in, (kt, vt, kn, vn, q, mult) = refs[:n_ffn], refs[n_ffn:n_ffn + 6]
    x_out, okt, ovt, o_att = refs[-4:]
    seq_idx = pl.program_id(0) // steps_per_seq
    scores, output, shift = _window_attention(kt, vt, kn, vn, q, mult, okt, ovt, o_att, seq_idx)
    stages = {0: scores, 2: output, 4: shift}
    _ffn_compute(*ffn_in, x_out, False, between=lambda stage: stages.get(stage, lambda: None)())


def _ffn_specs(rows, layer):
    row_spec = lambda w: pl.BlockSpec((rows, w), lambda i: (i, 0))
    lspec = lambda *shape: _layer_spec(shape, layer)
    return [row_spec(D_MODEL), row_spec(D_A), row_spec(D_B), lspec(1, D_B),
            _const_spec((D_A + D_B, D_MODEL)), lspec(1, D_MODEL),
            _const_spec((D_MODEL, D_FF)), _const_spec((D_FF, D_MODEL))], row_spec(D_MODEL)


def _out_ffn(x, ya, yb, pw, layer, weights, *, rows, norm_b):
    n = x.shape[0]
    in_specs, out_spec = _ffn_specs(rows, layer)
    return pl.pallas_call(
        functools.partial(_out_ffn_body, norm_b=norm_b),
        grid=(n // rows,),
        in_specs=in_specs,
        out_specs=out_spec,
        out_shape=jax.ShapeDtypeStruct((n, D_MODEL), F32),
        compiler_params=_params(1),
        name="out_ffn",
    )(x, ya, yb, pw["g_ob"], weights[0], pw["g_ffn"], weights[1], weights[2])


def _ffn_window(x, ya, yb, pw, layer, weights, cache_kt, cache_vt, k_new, v_new, q, mult, prev_out,
                *, rows):
    n = x.shape[0]
    steps = n // rows
    depth, db = cache_kt.shape[:2]
    per_seq = steps // db
    hps = N_HEADS // per_seq
    assert per_seq * db == steps and hps * per_seq == N_HEADS
    assert q.shape == (db, per_seq, Q_ROWS * hps, HEAD_DIM * hps)
    in_specs, out_spec = _ffn_specs(rows, layer)
    cache_spec = pl.BlockSpec((None, None, hps, HEAD_DIM, MAX_WINDOW),
                              lambda i: (layer, i // per_seq, i % per_seq, 0, 0))
    new_spec = pl.BlockSpec((hps, HEAD_DIM, LANES), lambda i: (i % per_seq, 0, 0))
    q_spec = pl.BlockSpec((None, None) + q.shape[2:], lambda i: (i // per_seq, i % per_seq, 0, 0))
    in_specs += [cache_spec, cache_spec, new_spec, new_spec, q_spec,
                 _const_spec((Q_ROWS * hps, MAX_WINDOW + LANES))]
    args = [x, ya, yb, pw["g_ob"], weights[0], pw["g_ffn"], weights[1], weights[2],
            cache_kt, cache_vt, k_new, v_new, q, mult]
    aliases = {}
    if prev_out is not None:
        in_specs += [pl.BlockSpec(memory_space=pl.ANY)] * 2
        aliases = {len(args): 1, len(args) + 1: 2}
        args += list(prev_out)
    win_shape = jax.ShapeDtypeStruct(cache_kt.shape, F32)
    return pl.pallas_call(
        functools.partial(_ffn_window_body, steps_per_seq=per_seq),
        grid=(steps,),
        in_specs=in_specs,
        out_specs=[out_spec, cache_spec, cache_spec,
                   pl.BlockSpec((None, hps, Q_ROWS, HEAD_DIM), lambda i: (i // per_seq, i % per_seq, 0, 0))],
        out_shape=[jax.ShapeDtypeStruct((n, D_MODEL), F32), win_shape, win_shape,
                   jax.ShapeDtypeStruct((db, N_HEADS, Q_ROWS, HEAD_DIM), F32)],
        input_output_aliases=aliases,
        compiler_params=_params(1),
        name="ffn_window",
    )(*args)


def _rope_tables(base_pos, offsets):
    half = ROT_DIM // 2
    inv = np.power(ROPE_THETA, -np.arange(half) * (2.0 / ROT_DIM))
    lane = np.arange(LANES) % HEAD_DIM
    freq = np.where(lane < ROT_DIM, inv[lane % half], 0.0)
    ang = lambda pos: np.asarray(pos, np.float64)[:, None] * freq[None, :]
    ab, ao = ang(base_pos), ang(offsets)
    tbase = np.stack([np.cos(ab), np.sin(ab)], axis=1)
    toff = np.stack([np.cos(ao), np.sin(ao)], axis=0)
    tsign = np.stack([-1.0 * (lane < half), 1.0 * ((lane >= half) & (lane < ROT_DIM))])
    return tuple(jnp.asarray(t, F32) for t in (tbase, toff, tsign))


def kernel(x_prompt, x_sample, cache_win_k, cache_win_v, attn_norm_g, w_in, v_norm_g, w_spatial, b_spatial, q_norm_g, k_norm_g, out_norm_a_g, out_norm_b_g, w_out, ffn_norm_g, w_up, w_down):
    depth = w_in.shape[0]
    batch, seq, _ = x_prompt.shape
    db, t_new, _ = x_sample.shape
    n_s = db * t_new
    assert n_s == CHUNK and t_new == T_NEW and cache_win_k.shape[2] == MAX_WINDOW
    assert seq % TILE == 0

    tab_p = _rope_tables(np.arange(seq // ROWS_IN_PROJ) * ROWS_IN_PROJ, np.arange(ROWS_IN_PROJ))
    tab_s = _rope_tables(np.full((1,), PAST_LEN), np.arange(n_s) % t_new)

    tril = jnp.tril(jnp.ones((CHUNK, CHUNK), bool))
    hsum = jnp.kron(jnp.eye(N_HEADS // 2, dtype=F32), jnp.full((HEAD_DIM, HEAD_DIM), 1.0 / HEAD_DIM, F32)).astype(BF16)
    row = lambda g: g[:, None, :]
    per_head = lambda g: jnp.tile(g, (1, N_HEADS))
    pw = dict(g_attn=row(attn_norm_g), w_in=w_in.astype(BF16), g_v=row(v_norm_g),
              g_q=row(per_head(q_norm_g)), g_k=row(per_head(k_norm_g)), hsum=hsum,
              g_oa=row(out_norm_a_g), g_ob=row(out_norm_b_g), g_ffn=row(ffn_norm_g))
    pair = lambda w: w.reshape(depth, N_SLABS, 2, CHUNK, CHUNK).transpose(0, 1, 3, 2, 4) \
        .reshape(depth, N_SLABS, CHUNK, 2 * CHUNK).astype(BF16)
    gate_w_p = pair(jnp.where(tril, w_spatial, 0))
    gate_b_p = jnp.repeat(b_spatial.transpose(0, 2, 1), HEAD_DIM, axis=2)
    t_of = np.arange(CHUNK) % t_new
    same_seq = (np.arange(CHUNK)[:, None] // t_new) == (np.arange(CHUNK)[None, :] // t_new)
    causal_s = jnp.asarray(same_seq & (t_of[:, None] >= t_of[None, :]))
    onehot = jnp.asarray(t_of[:, None] == np.arange(t_new)[None, :], F32)
    w_rep = jnp.einsum("ia,lhab,jb->lhij", onehot, w_spatial[:, :, :t_new, :t_new], onehot,
                       precision=lax.Precision.HIGHEST)
    gate_w_s = pair(jnp.where(causal_s, w_rep, 0))
    b_rep = jnp.einsum("ia,lha->lih", onehot, b_spatial[:, :, :t_new], precision=lax.Precision.HIGHEST)
    gate_b_s = jnp.repeat(b_rep, HEAD_DIM, axis=2)

    bias = jnp.asarray(_band_bias())
    per_seq = batch * seq // ROWS_FFN // db
    hps = N_HEADS // per_seq
    mult = jnp.asarray(_sample_multiplicity(Q_ROWS * hps))
    eye_h = jnp.eye(hps, dtype=F32)[None, None, :, None, :, None]

    cache_kt = cache_win_k.transpose(0, 1, 3, 4, 2)
    cache_vt = cache_win_v.transpose(0, 1, 3, 4, 2)

    xp = x_prompt.reshape(batch * seq, D_MODEL)
    xs = x_sample.reshape(n_s, D_MODEL)
    cv_list = []
    win_p = win_s = None
    for l in range(depth):
        ya_p, q16, *kv, kwin, vwin = _in_proj(xp, pw, l, tab_p, gate_w_p, gate_b_p, rows=ROWS_IN_PROJ,
                                              batch=batch, seq=seq, prev_win=win_p)
        win_p = (kwin, vwin)
        yb_p, w_l = _attn_prompt(q16, (kv[0:3], kv[3:6]), bias, pw["g_ob"], l, batch, seq,
                                 (w_out, w_up, w_down))
        ya_s, q, kt, vt, va = _in_proj(xs, pw, l, tab_s, gate_w_s, gate_b_s, rows=CHUNK)
        cv_list.append(va.reshape(db, t_new, D_A))
        q8 = jnp.tile(q.reshape(db, t_new, per_seq, hps, HEAD_DIM).transpose(0, 2, 3, 1, 4), (1, 1, 1, 2, 1))
        q_blk = (q8[:, :, :, :, None, :] * eye_h.astype(BF16)).reshape(db, per_seq, Q_ROWS * hps, hps * HEAD_DIM)
        heads = lambda t: t.reshape(N_HEADS, HEAD_DIM, n_s)

        xp, okt, ovt, o = _ffn_window(xp, ya_p, yb_p, pw, l, w_l, cache_kt, cache_vt, heads(kt),
                                      heads(vt), q_blk, mult, win_s, rows=ROWS_FFN)
        win_s = (okt, ovt)
        yb_s = o[:, :, :t_new].transpose(0, 2, 1, 3).reshape(n_s, D_B)
        xs = _out_ffn(xs, ya_s, yb_s, pw, l, w_l, rows=CHUNK, norm_b=True)

    win_p = [w.reshape(depth, batch, N_HEADS, HEAD_DIM, MAX_WINDOW) for w in win_p]
    unt = lambda w: w.transpose(0, 1, 4, 2, 3)
    return (xp.reshape(batch, seq, D_MODEL), xs.reshape(db, t_new, D_MODEL),
            unt(win_p[0]), unt(win_p[1]), unt(win_s[0]), unt(win_s[1]),
            jnp.stack(cv_list))
```

```python
import functools

import numpy as np
import jax
import jax.numpy as jnp
from jax import lax
from jax.experimental import pallas as pl
from jax.experimental.pallas import tpu as pltpu

D_MODEL = 1024
D_A = 512
D_B = 512
HEAD_DIM = 64
N_HEADS = 8
CHUNK = 128
D_FF = 4 * D_MODEL
ROT_DIM = HEAD_DIM // 4
ROPE_THETA = 500000.0
EPS = 1e-6
ATTN_SCALE = HEAD_DIM ** -0.5
LOG2_E = float(np.log2(np.e))
BAND = 128
TILE = 16 * BAND
SUB = 4
MAX_WINDOW = 2048
PAST_LEN = 16384
T_NEW = 4
Q_ROWS = 8

ROWS_IN_PROJ = 1024
ROWS_FFN = 256
N_IN_PROJ_INPUTS = 13

LANES = 128
N_SLABS = D_B // LANES
VMEM_LIMIT = 58 * 1024 * 1024

F32 = jnp.float32
BF16 = jnp.bfloat16


def _rms(x, g):
    return x * lax.rsqrt(jnp.mean(x * x, axis=-1, keepdims=True) + EPS) * g


def _gelu(x):
    c = np.float32(np.sqrt(2.0 / np.pi))
    return x * (0.5 * (1.0 + jnp.tanh(c * (x + 0.044715 * (x * x * x)))))


def _dot(a, b):
    return jnp.dot(a, b, preferred_element_type=F32)


def _dot_nt(a, b):
    return lax.dot_general(a, b, (((1,), (1,)), ((), ())), preferred_element_type=F32)


def _params(n_axes):
    return pltpu.CompilerParams(dimension_semantics=("arbitrary",) * n_axes,
                                vmem_limit_bytes=VMEM_LIMIT)


def _const_spec(shape):
    return pl.BlockSpec(shape, lambda *_: (0,) * len(shape), pipeline_mode=pl.Buffered(1))


def _layer_spec(shape, layer):
    return pl.BlockSpec((None,) + shape, lambda *_: (layer,) + (0,) * len(shape),
                        pipeline_mode=pl.Buffered(1))


def _slab(p):
    return slice(p * LANES, (p + 1) * LANES)


def _in_proj_body(x_ref, gattn_ref, win_ref, gv_ref, gq_ref, gk_ref, hsum_ref, tbase_ref, toff_ref,
                  tsign_ref, wg_ref, bg_ref, goa_ref, *outs, rows, tiles_per_seq, n_in):
    dilated = tiles_per_seq is not None
    outs = outs[n_in - N_IN_PROJ_INPUTS:]
    hb = _rms(x_ref[...], gattn_ref[...]).astype(BF16)

    cb, sb = tbase_ref[0:1, :], tbase_ref[1:2, :]
    co, so = toff_ref[0], toff_ref[1]
    cos = cb * co - sb * so
    sin = sb * co + cb * so
    sin_up, sin_dn = sin * tsign_ref[0:1, :], sin * tsign_ref[1:2, :]

    def proj(i):
        return _dot(hb, win_ref[:, i * 512:(i + 1) * 512])

    def head_norm_rope(z, g_ref, scale=1.0):
        zz = (z * z).astype(BF16)
        half = D_B // 2
        ms = jnp.concatenate([_dot(zz[:, :half], hsum_ref[...]), _dot(zz[:, half:], hsum_ref[...])],
                             axis=1)
        zn = z * lax.rsqrt(ms + EPS) * g_ref[...]
        slabs = []
        for p in range(N_SLABS):
            s = zn[:, _slab(p)]
            slabs.append((s * cos + pltpu.roll(s, LANES - ROT_DIM // 2, 1) * sin_up
                          + pltpu.roll(s, ROT_DIM // 2, 1) * sin_dn) * scale)
        return slabs

    if dilated:
        ya_ref, q16, k1, k4, k16, v1, v4, v16, kwin_ref, vwin_ref, scr1, scrk, scr4 = outs

        def split(slabs, src, r4_ref, r16_ref):
            for p in range(N_SLABS):
                src[p] = slabs[p]
            n4, n16 = rows // 4, rows // 16
            for p in range(N_SLABS):
                for r in range(4):
                    part = src[p, pl.ds(r, n4, stride=4), :]
                    scr4[p, r * n4:(r + 1) * n4, :] = part
                    if r4_ref is not None:
                        r4_ref[r, :, _slab(p)] = part.astype(BF16)
                for r in range(4):
                    for c in range(4):
                        part = scr4[p, pl.ds(r * n4 + c, n16, stride=4), :]
                        r16_ref[r + 4 * c, :, _slab(p)] = part.astype(r16_ref.dtype)
    else:
        ya_ref, q1, kt_ref, vt_ref, va_ref = outs

    zu = proj(0)
    zq = proj(2)
    u = _gelu(zu)
    q = head_norm_rope(zq, gq_ref, ATTN_SCALE * LOG2_E if dilated else ATTN_SCALE)
    zva = proj(1)
    if dilated:
        split(q, scr1, None, q16)
    else:
        q1[...] = jnp.concatenate(q, axis=1).astype(BF16)
    zk = proj(3)
    va = _rms(_gelu(zva), gv_ref[...])
    k = head_norm_rope(zk, gk_ref)
    v = proj(4)
    if dilated:
        for p in range(N_SLABS):
            k1[:, _slab(p)] = k[p].astype(BF16)
        split(k, scrk, k4, k16)
        v1[...] = v.astype(BF16)
        split([v[:, _slab(p)] for p in range(N_SLABS)], scr1, v4, v16)
    else:
        va_ref[...] = va
        kt_ref[...] = jnp.concatenate(k, axis=1).T
        vt_ref[...] = v.T

    lane_lo = lax.broadcasted_iota(jnp.int32, (CHUNK, LANES), 1) < HEAD_DIM
    n_chunks = rows // CHUNK
    side = min(2, n_chunks)
    ys = [[None] * N_SLABS for _ in range(n_chunks)]
    for c0 in range(0, n_chunks, side):
        for p in range(N_SLABS):
            parts = [va[c * CHUNK:(c + 1) * CHUNK, _slab(p)] for c in range(c0, c0 + side)]
            lo = jnp.concatenate([jnp.where(lane_lo, t, 0.0) for t in parts], axis=1)
            hi = jnp.concatenate([jnp.where(lane_lo, 0.0, t) for t in parts], axis=1)
            s = _dot(wg_ref[p], jnp.concatenate([lo, hi], axis=0).astype(BF16))
            for i in range(side):
                c = c0 + i
                ys[c][p] = u[c * CHUNK:(c + 1) * CHUNK, _slab(p)] * (s[:, _slab(i)] + bg_ref[:, _slab(p)])
    for c in range(n_chunks):
        ya_ref[c * CHUNK:(c + 1) * CHUNK, :] = _rms(jnp.concatenate(ys[c], axis=1), goa_ref[...]).astype(BF16)

    if dilated:
        @pl.when(pl.program_id(0) % tiles_per_seq >= tiles_per_seq - MAX_WINDOW // rows)
        def _():
            for p in range(N_SLABS):
                kwin_ref[_slab(p), :] = scrk[p].T
                vwin_ref[_slab(p), :] = scr1[p].T


def _in_proj(x, pw, layer, tables, gate_w, gate_b, *, rows, batch=None, seq=None, prev_win=None):
    n = x.shape[0]
    depth = pw["w_in"].shape[0]
    dilated = batch is not None
    tbase, toff, tsign = tables
    n_base = tbase.shape[0]
    row_spec = lambda w: pl.BlockSpec((rows, w), lambda i: (i, 0))
    lspec = lambda *shape: _layer_spec(shape, layer)
    flat = lambda dt: jax.ShapeDtypeStruct((n, 512), dt)
    scratch = []
    per_b = None
    in_specs = [row_spec(D_MODEL), lspec(1, D_MODEL), lspec(D_MODEL, 5 * 512),
                lspec(1, D_A), lspec(1, D_B), lspec(1, D_B),
                _const_spec((D_B // 2, D_B // 2)),
                pl.BlockSpec((None, 2, LANES), lambda i: (i % n_base, 0, 0)),
                _const_spec((2, rows, LANES)), _const_spec((2, LANES)),
                lspec(N_SLABS, CHUNK, 2 * CHUNK), lspec(CHUNK, D_A), lspec(1, D_A)]
    args = [x, pw["g_attn"], pw["w_in"], pw["g_v"], pw["g_q"], pw["g_k"], pw["hsum"], tbase, toff,
            tsign, gate_w, gate_b, pw["g_oa"]]
    aliases = {}
    if dilated:
        per_b = seq // rows
        split_shape = lambda d, dt: jax.ShapeDtypeStruct((batch, d, seq // d, D_B), dt)
        split_spec = lambda d: pl.BlockSpec((None, d, rows // d, D_B),
                                            lambda i: (i // per_b, 0, i % per_b, 0))
        kv_shape = [flat(BF16), split_shape(4, BF16), split_shape(16, BF16)]
        kv_spec = [row_spec(512), split_spec(4), split_spec(16)]
        first_win = per_b - MAX_WINDOW // rows
        win_shape = jax.ShapeDtypeStruct((depth, batch, D_B, MAX_WINDOW), F32)
        win_spec = pl.BlockSpec((None, None, D_B, rows),
                                lambda i: (layer, i // per_b, 0, jnp.maximum(i % per_b - first_win, 0)))
        out_shape = [flat(BF16), split_shape(16, F32)] + kv_shape * 2 + [win_shape] * 2
        out_specs = [row_spec(512), split_spec(16)] + kv_spec * 2 + [win_spec] * 2
        scratch = [pltpu.VMEM((N_SLABS, rows, LANES), F32)] * 3
        if prev_win is not None:
            in_specs += [pl.BlockSpec(memory_space=pl.ANY)] * 2
            aliases = {len(args): len(out_shape) - 2, len(args) + 1: len(out_shape) - 1}
            args += list(prev_win)
    else:
        t_shape = jax.ShapeDtypeStruct((512, n), F32)
        t_spec = pl.BlockSpec((512, rows), lambda i: (0, i))
        out_shape = [flat(BF16), flat(BF16), t_shape, t_shape, flat(F32)]
        out_specs = [row_spec(512), row_spec(512), t_spec, t_spec, row_spec(512)]
    return pl.pallas_call(
        functools.partial(_in_proj_body, rows=rows, tiles_per_seq=per_b, n_in=len(args)),
        grid=(n // rows,),
        in_specs=in_specs,
        out_specs=out_specs,
        out_shape=out_shape,
        scratch_shapes=scratch,
        input_output_aliases=aliases,
        compiler_params=_params(1),
        name="in_proj",
    )(*args)


def _band_block(q, kp, kc, vp, vc, bias):
    lane_lo = lax.broadcasted_iota(jnp.int32, (BAND, LANES), 1) < HEAD_DIM
    head_mask = [(lax.broadcasted_iota(jnp.int32, (1, LANES), 1) // HEAD_DIM == hh)
                 .astype(F32).astype(BF16) for hh in range(2)]
    ones = jnp.ones((2 * BAND, LANES), BF16)
    pick = lambda t: jnp.where(lane_lo, t[:BAND], t[BAND:])
    out = []
    for p in range(N_SLABS):
        qp = q[:, _slab(p)]
        kk = jnp.concatenate([kp[:, _slab(p)], kc[:, _slab(p)]], axis=0)
        vv = jnp.concatenate([vp[:, _slab(p)], vc[:, _slab(p)]], axis=0)
        qq = jnp.concatenate([qp * head_mask[0], qp * head_mask[1]], axis=0)
        s = _dot_nt(qq, kk) + bias
        m = jnp.max(s, axis=-1, keepdims=True)
        pr = jnp.exp2(s - m).astype(BF16)
        a = _dot(pr, jnp.concatenate([vv, ones], axis=1))
        l = pick(a[:, LANES:])
        out.append((pick(a[:, :LANES]) * (1.0 / l), pick(m) + jnp.log2(l)))
    return out


def _attn_prompt_body(q1, k1c, k1p, v1c, v1p, q4, k4c, k4p, v4c, v4p, q16, k16c, k16p, v16c, v16p,
                      bias_ref, gob_ref, *rest):
    n_w = (len(rest) - 7) // 2
    w_f32, out_ref, w_bf16 = rest[:n_w], rest[n_w], rest[n_w + 1:2 * n_w + 1]
    o1, s1, o4, s4, o16, s16 = rest[2 * n_w + 1:]
    for src, dst in zip(w_f32, w_bf16):
        dst[...] = src[...].astype(BF16)
    j = pl.program_id(1)
    g = pl.program_id(2)
    n_groups = pl.num_programs(2)

    def run(branch, q, kc, kp, vc, vp, first, refs, runs):
        bias = bias_ref[branch, first] if isinstance(first, int) else \
            bias_ref[branch, first.astype(jnp.int32)]
        res = _band_block(q.reshape(BAND, D_B).astype(BF16), kp, kc, vp, vc, bias)
        n = BAND // len(runs)
        for p in range(N_SLABS):
            for ref, val in zip(refs, res[p]):
                for i, row0 in enumerate(runs):
                    ref[p, pl.ds(pl.multiple_of(row0, n), n), :] = val[i * n:(i + 1) * n]

    for i in range(SUB):
        c = SUB * g + i
        blk = slice(i * BAND, (i + 1) * BAND)
        if i == 0:
            kp, vp, first = k1p[...], v1p[...], (n_groups * j + g) == 0
        else:
            before = slice((i - 1) * BAND, i * BAND)
            kp, vp, first = k1c[before, :], v1c[before, :], 0
        run(0, q1[:, 8 * i:8 * (i + 1), :], k1c[blk, :], kp, v1c[blk, :], vp, first, (o1, s1),
            [BAND * r + 8 * c for r in range(16)])
        run(1, q4[:, i], k4c[i], k4p[i], v4c[i], v4p[i], (n_groups * j + g) == 0, (o4, s4),
            [BAND * (4 * cc + i) + 32 * g for cc in range(4)])
        run(2, q16[i], k16c[i], k16p[i], v16c[i], v16p[i], j == 0, (o16, s16), [BAND * c])

    @pl.when(g == n_groups - 1)
    def _finalize():
        parts = [(o1, s1), (o4, s4), (o16, s16)]
        for r in range(16):
            rows = pl.ds(r * BAND, BAND)
            slabs = []
            for p in range(N_SLABS):
                lse = [s_ref[p, rows, :] for (_, s_ref) in parts]
                top = jnp.maximum(jnp.maximum(lse[0], lse[1]), lse[2])
                num = den = None
                for e, (o_ref, _) in zip(lse, parts):
                    w = jnp.exp2(e - top)
                    t = o_ref[p, rows, :] * w
                    num, den = (t, w) if num is None else (num + t, den + w)
                slabs.append(num / den)
            y = _rms(jnp.concatenate(slabs, axis=1), gob_ref[...])
            for p in range(N_SLABS):
                o16[p, rows, :] = y[:, _slab(p)]
        for p in range(N_SLABS):
            for r in range(16):
                o1[p, pl.ds(r, BAND, stride=16), :] = o16[p, r * BAND:(r + 1) * BAND, :]
            out_ref[:, _slab(p)] = o1[p].astype(BF16)


def _attn_prompt(q16, kv, bias, g_ob, layer, batch, seq, weights):
    assert SUB == 4
    n_tiles = seq // TILE
    n_groups = 16 // SUB
    span = SUB * BAND
    prev = lambda i: jnp.maximum(i - 1, 0)
    grp = lambda j, g: n_groups * j + g
    branches = [
        (pl.BlockSpec((None, 16, span // 16, D_B), lambda b, j, g: (b, 0, grp(j, g), 0)), q16,
         ((None, span, D_B), lambda b, j, g: (b, grp(j, g), 0)),
         ((None, BAND, D_B), lambda b, j, g: (b, prev(SUB * grp(j, g)), 0))),
        (pl.BlockSpec((None, 4, SUB, span // 16, D_B), lambda b, j, g: (b, 0, 0, grp(j, g), 0)),
         q16.reshape(batch, 4, 4, seq // 16, D_B),
         ((None, SUB, BAND, D_B), lambda b, j, g: (b, 0, grp(j, g), 0)),
         ((None, SUB, BAND, D_B), lambda b, j, g: (b, 0, prev(grp(j, g)), 0))),
        (pl.BlockSpec((None, SUB, BAND, D_B), lambda b, j, g: (b, g, j, 0)), q16,
         ((None, SUB, BAND, D_B), lambda b, j, g: (b, g, j, 0)),
         ((None, SUB, BAND, D_B), lambda b, j, g: (b, g, prev(j), 0))),
    ]
    in_specs, args = [], []
    for di, (q_spec, q_arr, cur, prv) in enumerate(branches):
        cs, ps = pl.BlockSpec(*cur), pl.BlockSpec(*prv)
        in_specs += [q_spec, cs, ps, cs, ps]
        kd, vd = (t[di] if di else t[di].reshape(batch, seq, D_B) for t in kv)
        args += [q_arr, kd, kd, vd, vd]
    in_specs += [_const_spec((3, 2, 2 * BAND, 2 * BAND)), _layer_spec((1, D_B), layer)]
    args += [bias, g_ob]
    n_steps = batch * n_tiles * n_groups
    step = lambda b, j, g: (b * n_tiles + j) * n_groups + g
    w_out_specs, w_out_shapes = [], []
    for w in weights:
        rows, cols = w.shape[1] // n_steps, w.shape[2]
        in_specs.append(pl.BlockSpec((None, rows, cols), lambda b, j, g: (layer, step(b, j, g), 0)))
        w_out_specs.append(pl.BlockSpec((rows, cols), lambda b, j, g: (step(b, j, g), 0)))
        w_out_shapes.append(jax.ShapeDtypeStruct(w.shape[1:], BF16))
    args += list(weights)
    out, *w_bf16 = pl.pallas_call(
        _attn_prompt_body,
        grid=(batch, n_tiles, n_groups),
        in_specs=in_specs,
        out_specs=[pl.BlockSpec((None, TILE, D_B), lambda b, j, g: (b, j, 0))] + w_out_specs,
        out_shape=[jax.ShapeDtypeStruct((batch, seq, D_B), BF16)] + w_out_shapes,
        scratch_shapes=[pltpu.VMEM((N_SLABS, TILE, LANES), F32)] * 6,
        compiler_params=_params(3),
        name="attn_prompt",
    )(*args)
    return out.reshape(batch * seq, D_B), w_bf16


def _band_bias():
    kj = np.arange(2 * BAND)[None, :]
    rho = np.arange(BAND)
    out = []
    for runs in (16, 4, 1):
        per = BAND // runs
        qi = (runs * (rho % per) + rho // per)[:, None]
        ok = (kj >= qi) & (kj <= qi + BAND)
        out.append(np.stack([np.where(ok, 0.0, -np.inf), np.where(ok & (kj >= BAND), 0.0, -np.inf)]))
    return np.tile(np.stack(out), (1, 1, 2, 1)).astype(np.float32)


def _window_attention(kt_ref, vt_ref, kn_ref, vn_ref, q_ref, mult_ref, okt_ref, ovt_ref, o_ref,
                      seq_idx):
    n_heads = kt_ref.shape[0]
    n_feat = n_heads * HEAD_DIM
    stack = lambda w_ref, n_ref: jnp.concatenate(
        [w_ref[...].reshape(n_feat, MAX_WINDOW), n_ref[...].reshape(n_feat, LANES)], axis=1).astype(BF16)
    state = {}

    def scores():
        own = lax.broadcasted_iota(jnp.int32, (Q_ROWS * n_heads, LANES), 1) // T_NEW == seq_idx
        mult = jnp.concatenate([mult_ref[:, 0:MAX_WINDOW],
                                jnp.where(own, mult_ref[:, MAX_WINDOW:], 0.0)], axis=1)
        s = jnp.where(mult > 0, _dot(q_ref[...], stack(kt_ref, kn_ref)), -jnp.inf)
        state["p"] = mult * jnp.exp(s - jnp.max(s, axis=-1, keepdims=True))

    def output():
        p = state["p"]
        o = _dot_nt(p.astype(BF16), stack(vt_ref, vn_ref)) / jnp.sum(p, axis=-1, keepdims=True)
        for h in range(n_heads):
            o_ref[h] = o[Q_ROWS * h:Q_ROWS * (h + 1), HEAD_DIM * h:HEAD_DIM * (h + 1)]

    def shift():
        body = MAX_WINDOW - LANES
        new_shift = (LANES - T_NEW * (seq_idx + 1)) % LANES
        is_new = lax.broadcasted_iota(jnp.int32, (HEAD_DIM, LANES), 1) >= LANES - T_NEW
        for src_ref, new_ref, dst_ref in ((kt_ref, kn_ref, okt_ref), (vt_ref, vn_ref, ovt_ref)):
            for h in range(n_heads):
                y = pltpu.roll(src_ref[h], MAX_WINDOW - T_NEW, 1)
                dst_ref[h, :, 0:body] = y[:, 0:body]
                dst_ref[h, :, body:MAX_WINDOW] = jnp.where(
                    is_new, pltpu.roll(new_ref[h], new_shift, 1), y[:, body:])

    return scores, output, shift


def _sample_multiplicity(n_rows):
    def mult(d):
        ok = d >= 0
        return (ok & (d <= 128)).astype(np.int32) + (ok & (d % 4 == 0) & (d <= 512)) \
            + (ok & (d % 16 == 0) & (d <= 2048))
    t = (np.arange(n_rows) % Q_ROWS % T_NEW)[:, None]
    old = mult(MAX_WINDOW + t - np.arange(MAX_WINDOW)[None, :])
    new = mult(t - (np.arange(LANES) % T_NEW)[None, :])
    return np.concatenate([old, new], axis=1).astype(np.float32)


def _ffn_compute(x_ref, ya_ref, yb_ref, gob_ref, wout_ref, gffn_ref, wup_ref, wdown_ref, o_ref,
                 norm_b, between=lambda stage: None):
    yb = yb_ref[...]
    if norm_b:
        yb = _rms(yb, gob_ref[...])
    x1 = x_ref[...] + _dot(ya_ref[...], wout_ref[0:D_A, :]) \
        + _dot(yb.astype(BF16), wout_ref[D_A:D_A + D_B, :])
    hb = _rms(x1, gffn_ref[...]).astype(BF16)
    between(0)
    step = D_FF // 4
    mlp = None
    for c in range(4):
        f = _dot(hb, wup_ref[:, c * step:(c + 1) * step])
        f = jnp.square(jnp.maximum(f, 0.0)).astype(BF16)
        d = _dot(f, wdown_ref[c * step:(c + 1) * step, :])
        mlp = d if mlp is None else mlp + d
        between(c + 1)
    o_ref[...] = x1 + mlp


def _out_ffn_body(*refs, norm_b):
    _ffn_compute(*refs, norm_b)


def _ffn_window_body(*refs, steps_per_seq):
    n_ffn = 8
    ffn_in, (kt, vt, kn, vn, q, mult) = refs[:n_ffn], refs[n_ffn:n_ffn + 6]
    x_out, okt, ovt, o_att = refs[-4:]
    seq_idx = pl.program_id(0) // steps_per_seq
    scores, output, shift = _window_attention(kt, vt, kn, vn, q, mult, okt, ovt, o_att, seq_idx)
    stages = {0: scores, 2: output, 4: shift}
    _ffn_compute(*ffn_in, x_out, False, between=lambda stage: stages.get(stage, lambda: None)())


def _ffn_specs(rows, layer):
    row_spec = lambda w: pl.BlockSpec((rows, w), lambda i: (i, 0))
    lspec = lambda *shape: _layer_spec(shape, layer)
    return [row_spec(D_MODEL), row_spec(D_A), row_spec(D_B), lspec(1, D_B),
            _const_spec((D_A + D_B, D_MODEL)), lspec(1, D_MODEL),
            _const_spec((D_MODEL, D_FF)), _const_spec((D_FF, D_MODEL))], row_spec(D_MODEL)


def _out_ffn(x, ya, yb, pw, layer, weights, *, rows, norm_b):
    n = x.shape[0]
    in_specs, out_spec = _ffn_specs(rows, layer)
    return pl.pallas_call(
        functools.partial(_out_ffn_body, norm_b=norm_b),
        grid=(n // rows,),
        in_specs=in_specs,
        out_specs=out_spec,
        out_shape=jax.ShapeDtypeStruct((n, D_MODEL), F32),
        compiler_params=_params(1),
        name="out_ffn",
    )(x, ya, yb, pw["g_ob"], weights[0], pw["g_ffn"], weights[1], weights[2])


def _ffn_window(x, ya, yb, pw, layer, weights, cache_kt, cache_vt, k_new, v_new, q, mult, prev_out,
                *, rows):
    n = x.shape[0]
    steps = n // rows
    depth, db = cache_kt.shape[:2]
    per_seq = steps // db
    hps = N_HEADS // per_seq
    assert per_seq * db == steps and hps * per_seq == N_HEADS
    assert q.shape == (db, per_seq, Q_ROWS * hps, HEAD_DIM * hps)
    in_specs, out_spec = _ffn_specs(rows, layer)
    cache_spec = pl.BlockSpec((None, None, hps, HEAD_DIM, MAX_WINDOW),
                              lambda i: (layer, i // per_seq, i % per_seq, 0, 0))
    new_spec = pl.BlockSpec((hps, HEAD_DIM, LANES), lambda i: (i % per_seq, 0, 0))
    q_spec = pl.BlockSpec((None, None) + q.shape[2:], lambda i: (i // per_seq, i % per_seq, 0, 0))
    in_specs += [cache_spec, cache_spec, new_spec, new_spec, q_spec,
                 _const_spec((Q_ROWS * hps, MAX_WINDOW + LANES))]
    args = [x, ya, yb, pw["g_ob"], weights[0], pw["g_ffn"], weights[1], weights[2],
            cache_kt, cache_vt, k_new, v_new, q, mult]
    aliases = {}
    if prev_out is not None:
        in_specs += [pl.BlockSpec(memory_space=pl.ANY)] * 2
        aliases = {len(args): 1, len(args) + 1: 2}
        args += list(prev_out)
    win_shape = jax.ShapeDtypeStruct(cache_kt.shape, F32)
    return pl.pallas_call(
        functools.partial(_ffn_window_body, steps_per_seq=per_seq),
        grid=(steps,),
        in_specs=in_specs,
        out_specs=[out_spec, cache_spec, cache_spec,
                   pl.BlockSpec((None, hps, Q_ROWS, HEAD_DIM), lambda i: (i // per_seq, i % per_seq, 0, 0))],
        out_shape=[jax.ShapeDtypeStruct((n, D_MODEL), F32), win_shape, win_shape,
                   jax.ShapeDtypeStruct((db, N_HEADS, Q_ROWS, HEAD_DIM), F32)],
        input_output_aliases=aliases,
        compiler_params=_params(1),
        name="ffn_window",
    )(*args)


def _rope_tables(base_pos, offsets):
    half = ROT_DIM // 2
    inv = np.power(ROPE_THETA, -np.arange(half) * (2.0 / ROT_DIM))
    lane = np.arange(LANES) % HEAD_DIM
    freq = np.where(lane < ROT_DIM, inv[lane % half], 0.0)
    ang = lambda pos: np.asarray(pos, np.float64)[:, None] * freq[None, :]
    ab, ao = ang(base_pos), ang(offsets)
    tbase = np.stack([np.cos(ab), np.sin(ab)], axis=1)
    toff = np.stack([np.cos(ao), np.sin(ao)], axis=0)
    tsign = np.stack([-1.0 * (lane < half), 1.0 * ((lane >= half) & (lane < ROT_DIM))])
    return tuple(jnp.asarray(t, F32) for t in (tbase, toff, tsign))


def kernel(x_prompt, x_sample, cache_win_k, cache_win_v, attn_norm_g, w_in, v_norm_g, w_spatial, b_spatial, q_norm_g, k_norm_g, out_norm_a_g, out_norm_b_g, w_out, ffn_norm_g, w_up, w_down):
    depth = w_in.shape[0]
    batch, seq, _ = x_prompt.shape
    db, t_new, _ = x_sample.shape
    n_s = db * t_new
    assert n_s == CHUNK and t_new == T_NEW and cache_win_k.shape[2] == MAX_WINDOW
    assert seq % TILE == 0

    tab_p = _rope_tables(np.arange(seq // ROWS_IN_PROJ) * ROWS_IN_PROJ, np.arange(ROWS_IN_PROJ))
    tab_s = _rope_tables(np.full((1,), PAST_LEN), np.arange(n_s) % t_new)

    tril = jnp.tril(jnp.ones((CHUNK, CHUNK), bool))
    hsum = jnp.kron(jnp.eye(N_HEADS // 2, dtype=F32), jnp.full((HEAD_DIM, HEAD_DIM), 1.0 / HEAD_DIM, F32)).astype(BF16)
    row = lambda g: g[:, None, :]
    per_head = lambda g: jnp.tile(g, (1, N_HEADS))
    pw = dict(g_attn=row(attn_norm_g), w_in=w_in.astype(BF16), g_v=row(v_norm_g),
              g_q=row(per_head(q_norm_g)), g_k=row(per_head(k_norm_g)), hsum=hsum,
              g_oa=row(out_norm_a_g), g_ob=row(out_norm_b_g), g_ffn=row(ffn_norm_g))
    pair = lambda w: w.reshape(depth, N_SLABS, 2, CHUNK, CHUNK).transpose(0, 1, 3, 2, 4) \
        .reshape(depth, N_SLABS, CHUNK, 2 * CHUNK).astype(BF16)
    gate_w_p = pair(jnp.where(tril, w_spatial, 0))
    gate_b_p = jnp.repeat(b_spatial.transpose(0, 2, 1), HEAD_DIM, axis=2)
    t_of = np.arange(CHUNK) % t_new
    same_seq = (np.arange(CHUNK)[:, None] // t_new) == (np.arange(CHUNK)[None, :] // t_new)
    causal_s = jnp.asarray(same_seq & (t_of[:, None] >= t_of[None, :]))
    onehot = jnp.asarray(t_of[:, None] == np.arange(t_new)[None, :], F32)
    w_rep = jnp.einsum("ia,lhab,jb->lhij", onehot, w_spatial[:, :, :t_new, :t_new], onehot,
                       precision=lax.Precision.HIGHEST)
    gate_w_s = pair(jnp.where(causal_s, w_rep, 0))
    b_rep = jnp.einsum("ia,lha->lih", onehot, b_spatial[:, :, :t_new], precision=lax.Precision.HIGHEST)
    gate_b_s = jnp.repeat(b_rep, HEAD_DIM, axis=2)

    bias = jnp.asarray(_band_bias())
    per_seq = batch * seq // ROWS_FFN // db
    hps = N_HEADS // per_seq
    mult = jnp.asarray(_sample_multiplicity(Q_ROWS * hps))
    eye_h = jnp.eye(hps, dtype=F32)[None, None, :, None, :, None]

    cache_kt = cache_win_k.transpose(0, 1, 3, 4, 2)
    cache_vt = cache_win_v.transpose(0, 1, 3, 4, 2)

    xp = x_prompt.reshape(batch * seq, D_MODEL)
    xs = x_sample.reshape(n_s, D_MODEL)
    cv_list = []
    win_p = win_s = None
    for l in range(depth):
        ya_p, q16, *kv, kwin, vwin = _in_proj(xp, pw, l, tab_p, gate_w_p, gate_b_p, rows=ROWS_IN_PROJ,
                                              batch=batch, seq=seq, prev_win=win_p)
        win_p = (kwin, vwin)
        yb_p, w_l = _attn_prompt(q16, (kv[0:3], kv[3:6]), bias, pw["g_ob"], l, batch, seq,
                                 (w_out, w_up, w_down))
        ya_s, q, kt, vt, va = _in_proj(xs, pw, l, tab_s, gate_w_s, gate_b_s, rows=CHUNK)
        cv_list.append(va.reshape(db, t_new, D_A))
        q8 = jnp.tile(q.reshape(db, t_new, per_seq, hps, HEAD_DIM).transpose(0, 2, 3, 1, 4), (1, 1, 1, 2, 1))
        q_blk = (q8[:, :, :, :, None, :] * eye_h.astype(BF16)).reshape(db, per_seq, Q_ROWS * hps, hps * HEAD_DIM)
        heads = lambda t: t.reshape(N_HEADS, HEAD_DIM, n_s)

        xp, okt, ovt, o = _ffn_window(xp, ya_p, yb_p, pw, l, w_l, cache_kt, cache_vt, heads(kt),
                                      heads(vt), q_blk, mult, win_s, rows=ROWS_FFN)
        win_s = (okt, ovt)
        yb_s = o[:, :, :t_new].transpose(0, 2, 1, 3).reshape(n_s, D_B)
        xs = _out_ffn(xs, ya_s, yb_s, pw, l, w_l, rows=CHUNK, norm_b=True)

    win_p = [w.reshape(depth, batch, N_HEADS, HEAD_DIM, MAX_WINDOW) for w in win_p]
    unt = lambda w: w.transpose(0, 1, 4, 2, 3)
    return (xp.reshape(batch, seq, D_MODEL), xs.reshape(db, t_new, D_MODEL),
            unt(win_p[0]), unt(win_p[1]), unt(win_s[0]), unt(win_s[1]),
            jnp.stack(cv_list))
```

```python
import functools

import numpy as np
import jax
import jax.numpy as jnp
from jax import lax
from jax.experimental import pallas as pl
from jax.experimental.pallas import tpu as pltpu

D_MODEL = 1024
D_A = 512
D_B = 512
HEAD_DIM = 64
N_HEADS = 8
CHUNK = 128
D_FF = 4 * D_MODEL
ROT_DIM = HEAD_DIM // 4
ROPE_THETA = 500000.0
EPS = 1e-6
ATTN_SCALE = HEAD_DIM ** -0.5
LOG2_E = float(np.log2(np.e))
BAND = 128
TILE = 16 * BAND
SUB = 4
MAX_WINDOW = 2048
PAST_LEN = 16384
T_NEW = 4
Q_ROWS = 8

ROWS_IN_PROJ = 1024
ROWS_FFN = 256
N_IN_PROJ_INPUTS = 13

LANES = 128
N_SLABS = D_B // LANES
WINDOW_SLOTS = 3
VMEM_LIMIT = 58 * 1024 * 1024

F32 = jnp.float32
BF16 = jnp.bfloat16


def _rms(x, g):
    return x * lax.rsqrt(jnp.mean(x * x, axis=-1, keepdims=True) + EPS) * g


def _gelu(x):
    c = np.float32(np.sqrt(2.0 / np.pi))
    return x * (0.5 * (1.0 + jnp.tanh(c * (x + 0.044715 * (x * x * x)))))


def _dot(a, b):
    return jnp.dot(a, b, preferred_element_type=F32)


def _dot_nt(a, b):
    return lax.dot_general(a, b, (((1,), (1,)), ((), ())), preferred_element_type=F32)


def _params(n_axes):
    return pltpu.CompilerParams(dimension_semantics=("arbitrary",) * n_axes,
                                vmem_limit_bytes=VMEM_LIMIT)


def _const_spec(shape):
    return pl.BlockSpec(shape, lambda *_: (0,) * len(shape), pipeline_mode=pl.Buffered(1))


def _layer_spec(shape, layer):
    return pl.BlockSpec((None,) + shape, lambda *_: (layer,) + (0,) * len(shape),
                        pipeline_mode=pl.Buffered(1))


def _slab(p):
    return slice(p * LANES, (p + 1) * LANES)


def _in_proj_body(x_ref, gattn_ref, win_ref, gv_ref, gq_ref, gk_ref, hsum_ref, tbase_ref, toff_ref,
                  tsign_ref, wg_ref, bg_ref, goa_ref, *outs, rows, tiles_per_seq, n_in):
    dilated = tiles_per_seq is not None
    outs = outs[n_in - N_IN_PROJ_INPUTS:]
    hb = _rms(x_ref[...], gattn_ref[...]).astype(BF16)

    cb, sb = tbase_ref[0:1, :], tbase_ref[1:2, :]
    co, so = toff_ref[0], toff_ref[1]
    cos = cb * co - sb * so
    sin = sb * co + cb * so
    sin_up, sin_dn = sin * tsign_ref[0:1, :], sin * tsign_ref[1:2, :]

    def proj(i):
        return _dot(hb, win_ref[:, i * 512:(i + 1) * 512])

    def head_norm_rope(z, g_ref, scale=1.0):
        zz = (z * z).astype(BF16)
        half = D_B // 2
        ms = jnp.concatenate([_dot(zz[:, :half], hsum_ref[...]), _dot(zz[:, half:], hsum_ref[...])],
                             axis=1)
        zn = z * lax.rsqrt(ms + EPS) * g_ref[...]
        slabs = []
        for p in range(N_SLABS):
            s = zn[:, _slab(p)]
            slabs.append((s * cos + pltpu.roll(s, LANES - ROT_DIM // 2, 1) * sin_up
                          + pltpu.roll(s, ROT_DIM // 2, 1) * sin_dn) * scale)
        return slabs

    if dilated:
        ya_ref, q16, k1, k4, k16, v1, v4, v16, kwin_ref, vwin_ref, scr1, scrk, scr4 = outs

        def split(slabs, src, r4_ref, r16_ref):
            for p in range(N_SLABS):
                src[p] = slabs[p]
            n4, n16 = rows // 4, rows // 16
            for p in range(N_SLABS):
                for r in range(4):
                    part = src[p, pl.ds(r, n4, stride=4), :]
                    scr4[p, r * n4:(r + 1) * n4, :] = part
                    if r4_ref is not None:
                        r4_ref[r, :, _slab(p)] = part.astype(BF16)
                for r in range(4):
                    for c in range(4):
                        part = scr4[p, pl.ds(r * n4 + c, n16, stride=4), :]
                        r16_ref[r + 4 * c, :, _slab(p)] = part.astype(r16_ref.dtype)
    else:
        ya_ref, q1, kt_ref, vt_ref, va_ref = outs

    zu = proj(0)
    zq = proj(2)
    u = _gelu(zu)
    q = head_norm_rope(zq, gq_ref, ATTN_SCALE * LOG2_E if dilated else ATTN_SCALE)
    zva = proj(1)
    if dilated:
        split(q, scr1, None, q16)
    else:
        q1[...] = jnp.concatenate(q, axis=1).astype(BF16)
    zk = proj(3)
    va = _rms(_gelu(zva), gv_ref[...])
    k = head_norm_rope(zk, gk_ref)
    v = proj(4)
    if dilated:
        for p in range(N_SLABS):
            k1[:, _slab(p)] = k[p].astype(BF16)
        split(k, scrk, k4, k16)
        v1[...] = v.astype(BF16)
        split([v[:, _slab(p)] for p in range(N_SLABS)], scr1, v4, v16)
    else:
        va_ref[...] = va
        kt_ref[...] = jnp.concatenate(k, axis=1).T
        vt_ref[...] = v.T

    lane_lo = lax.broadcasted_iota(jnp.int32, (CHUNK, LANES), 1) < HEAD_DIM
    n_chunks = rows // CHUNK
    side = min(2, n_chunks)
    ys = [[None] * N_SLABS for _ in range(n_chunks)]
    for c0 in range(0, n_chunks, side):
        for p in range(N_SLABS):
            parts = [va[c * CHUNK:(c + 1) * CHUNK, _slab(p)] for c in range(c0, c0 + side)]
            lo = jnp.concatenate([jnp.where(lane_lo, t, 0.0) for t in parts], axis=1)
            hi = jnp.concatenate([jnp.where(lane_lo, 0.0, t) for t in parts], axis=1)
            s = _dot(wg_ref[p], jnp.concatenate([lo, hi], axis=0).astype(BF16))
            for i in range(side):
                c = c0 + i
                ys[c][p] = u[c * CHUNK:(c + 1) * CHUNK, _slab(p)] * (s[:, _slab(i)] + bg_ref[:, _slab(p)])
    for c in range(n_chunks):
        ya_ref[c * CHUNK:(c + 1) * CHUNK, :] = _rms(jnp.concatenate(ys[c], axis=1), goa_ref[...]).astype(BF16)

    if dilated:
        @pl.when(pl.program_id(0) % tiles_per_seq >= tiles_per_seq - MAX_WINDOW // rows)
        def _():
            for p in range(N_SLABS):
                kwin_ref[_slab(p), :] = scrk[p].T
                vwin_ref[_slab(p), :] = scr1[p].T


def _in_proj(x, pw, layer, tables, gate_w, gate_b, *, rows, batch=None, seq=None, prev_win=None):
    n = x.shape[0]
    depth = pw["w_in"].shape[0]
    dilated = batch is not None
    tbase, toff, tsign = tables
    n_base = tbase.shape[0]
    row_spec = lambda w: pl.BlockSpec((rows, w), lambda i: (i, 0))
    lspec = lambda *shape: _layer_spec(shape, layer)
    flat = lambda dt: jax.ShapeDtypeStruct((n, 512), dt)
    scratch = []
    per_b = None
    in_specs = [row_spec(D_MODEL), lspec(1, D_MODEL), lspec(D_MODEL, 5 * 512),
                lspec(1, D_A), lspec(1, D_B), lspec(1, D_B),
                _const_spec((D_B // 2, D_B // 2)),
                pl.BlockSpec((None, 2, LANES), lambda i: (i % n_base, 0, 0)),
                _const_spec((2, rows, LANES)), _const_spec((2, LANES)),
                lspec(N_SLABS, CHUNK, 2 * CHUNK), lspec(CHUNK, D_A), lspec(1, D_A)]
    args = [x, pw["g_attn"], pw["w_in"], pw["g_v"], pw["g_q"], pw["g_k"], pw["hsum"], tbase, toff,
            tsign, gate_w, gate_b, pw["g_oa"]]
    aliases = {}
    if dilated:
        per_b = seq // rows
        split_shape = lambda d, dt: jax.ShapeDtypeStruct((batch, d, seq // d, D_B), dt)
        split_spec = lambda d: pl.BlockSpec((None, d, rows // d, D_B),
                                            lambda i: (i // per_b, 0, i % per_b, 0))
        kv_shape = [flat(BF16), split_shape(4, BF16), split_shape(16, BF16)]
        kv_spec = [row_spec(512), split_spec(4), split_spec(16)]
        first_win = per_b - MAX_WINDOW // rows
        win_shape = jax.ShapeDtypeStruct((depth, batch, D_B, MAX_WINDOW), F32)
        win_spec = pl.BlockSpec((None, None, D_B, rows),
                                lambda i: (layer, i // per_b, 0, jnp.maximum(i % per_b - first_win, 0)))
        out_shape = [flat(BF16), split_shape(16, F32)] + kv_shape * 2 + [win_shape] * 2
        out_specs = [row_spec(512), split_spec(16)] + kv_spec * 2 + [win_spec] * 2
        scratch = [pltpu.VMEM((N_SLABS, rows, LANES), F32)] * 3
        if prev_win is not None:
            in_specs += [pl.BlockSpec(memory_space=pl.ANY)] * 2
            aliases = {len(args): len(out_shape) - 2, len(args) + 1: len(out_shape) - 1}
            args += list(prev_win)
    else:
        t_shape = jax.ShapeDtypeStruct((512, n), F32)
        t_spec = pl.BlockSpec((512, rows), lambda i: (0, i))
        out_shape = [flat(BF16), flat(BF16), t_shape, t_shape, flat(F32)]
        out_specs = [row_spec(512), row_spec(512), t_spec, t_spec, row_spec(512)]
    return pl.pallas_call(
        functools.partial(_in_proj_body, rows=rows, tiles_per_seq=per_b, n_in=len(args)),
        grid=(n // rows,),
        in_specs=in_specs,
        out_specs=out_specs,
        out_shape=out_shape,
        scratch_shapes=scratch,
        input_output_aliases=aliases,
        compiler_params=_params(1),
        name="in_proj",
    )(*args)


def _band_block(q, kp, kc, vp, vc, bias):
    lane_lo = lax.broadcasted_iota(jnp.int32, (BAND, LANES), 1) < HEAD_DIM
    head_mask = [(lax.broadcasted_iota(jnp.int32, (1, LANES), 1) // HEAD_DIM == hh)
                 .astype(F32).astype(BF16) for hh in range(2)]
    ones = jnp.ones((2 * BAND, LANES), BF16)
    pick = lambda t: jnp.where(lane_lo, t[:BAND], t[BAND:])
    out = []
    for p in range(N_SLABS):
        qp = q[:, _slab(p)]
        kk = jnp.concatenate([kp[:, _slab(p)], kc[:, _slab(p)]], axis=0)
        vv = jnp.concatenate([vp[:, _slab(p)], vc[:, _slab(p)]], axis=0)
        qq = jnp.concatenate([qp * head_mask[0], qp * head_mask[1]], axis=0)
        s = _dot_nt(qq, kk) + bias
        m = jnp.max(s, axis=-1, keepdims=True)
        pr = jnp.exp2(s - m).astype(BF16)
        a = _dot(pr, jnp.concatenate([vv, ones], axis=1))
        l = pick(a[:, LANES:])
        out.append((pick(a[:, :LANES]) * (1.0 / l), pick(m) + jnp.log2(l)))
    return out


def _attn_prompt_body(q1, k1c, k1p, v1c, v1p, q4, k4c, k4p, v4c, v4p, q16, k16c, k16p, v16c, v16p,
                      bias_ref, gob_ref, *rest):
    n_w = (len(rest) - 7) // 2
    w_f32, out_ref, w_bf16 = rest[:n_w], rest[n_w], rest[n_w + 1:2 * n_w + 1]
    o1, s1, o4, s4, o16, s16 = rest[2 * n_w + 1:]
    for src, dst in zip(w_f32, w_bf16):
        dst[...] = src[...].astype(BF16)
    j = pl.program_id(1)
    g = pl.program_id(2)
    n_groups = pl.num_programs(2)

    def run(branch, q, kc, kp, vc, vp, first, refs, runs):
        bias = bias_ref[branch, first] if isinstance(first, int) else \
            bias_ref[branch, first.astype(jnp.int32)]
        res = _band_block(q.reshape(BAND, D_B).astype(BF16), kp, kc, vp, vc, bias)
        n = BAND // len(runs)
        for p in range(N_SLABS):
            for ref, val in zip(refs, res[p]):
                for i, row0 in enumerate(runs):
                    ref[p, pl.ds(pl.multiple_of(row0, n), n), :] = val[i * n:(i + 1) * n]

    for i in range(SUB):
        c = SUB * g + i
        blk = slice(i * BAND, (i + 1) * BAND)
        if i == 0:
            kp, vp, first = k1p[...], v1p[...], (n_groups * j + g) == 0
        else:
            before = slice((i - 1) * BAND, i * BAND)
            kp, vp, first = k1c[before, :], v1c[before, :], 0
        run(0, q1[:, 8 * i:8 * (i + 1), :], k1c[blk, :], kp, v1c[blk, :], vp, first, (o1, s1),
            [BAND * r + 8 * c for r in range(16)])
        run(1, q4[:, i], k4c[i], k4p[i], v4c[i], v4p[i], (n_groups * j + g) == 0, (o4, s4),
            [BAND * (4 * cc + i) + 32 * g for cc in range(4)])
        run(2, q16[i], k16c[i], k16p[i], v16c[i], v16p[i], j == 0, (o16, s16), [BAND * c])

    @pl.when(g == n_groups - 1)
    def _finalize():
        parts = [(o1, s1), (o4, s4), (o16, s16)]
        for r in range(16):
            rows = pl.ds(r * BAND, BAND)
            slabs = []
            for p in range(N_SLABS):
                lse = [s_ref[p, rows, :] for (_, s_ref) in parts]
                top = jnp.maximum(jnp.maximum(lse[0], lse[1]), lse[2])
                num = den = None
                for e, (o_ref, _) in zip(lse, parts):
                    w = jnp.exp2(e - top)
                    t = o_ref[p, rows, :] * w
                    num, den = (t, w) if num is None else (num + t, den + w)
                slabs.append(num / den)
            y = _rms(jnp.concatenate(slabs, axis=1), gob_ref[...])
            for p in range(N_SLABS):
                o16[p, rows, :] = y[:, _slab(p)]
        for p in range(N_SLABS):
            for r in range(16):
                o1[p, pl.ds(r, BAND, stride=16), :] = o16[p, r * BAND:(r + 1) * BAND, :]
            out_ref[:, _slab(p)] = o1[p].astype(BF16)


def _attn_prompt(q16, kv, bias, g_ob, layer, batch, seq, weights):
    assert SUB == 4
    n_tiles = seq // TILE
    n_groups = 16 // SUB
    span = SUB * BAND
    prev = lambda i: jnp.maximum(i - 1, 0)
    grp = lambda j, g: n_groups * j + g
    branches = [
        (pl.BlockSpec((None, 16, span // 16, D_B), lambda b, j, g: (b, 0, grp(j, g), 0)), q16,
         ((None, span, D_B), lambda b, j, g: (b, grp(j, g), 0)),
         ((None, BAND, D_B), lambda b, j, g: (b, prev(SUB * grp(j, g)), 0))),
        (pl.BlockSpec((None, 4, SUB, span // 16, D_B), lambda b, j, g: (b, 0, 0, grp(j, g), 0)),
         q16.reshape(batch, 4, 4, seq // 16, D_B),
         ((None, SUB, BAND, D_B), lambda b, j, g: (b, 0, grp(j, g), 0)),
         ((None, SUB, BAND, D_B), lambda b, j, g: (b, 0, prev(grp(j, g)), 0))),
        (pl.BlockSpec((None, SUB, BAND, D_B), lambda b, j, g: (b, g, j, 0)), q16,
         ((None, SUB, BAND, D_B), lambda b, j, g: (b, g, j, 0)),
         ((None, SUB, BAND, D_B), lambda b, j, g: (b, g, prev(j), 0))),
    ]
    in_specs, args = [], []
    for di, (q_spec, q_arr, cur, prv) in enumerate(branches):
        cs, ps = pl.BlockSpec(*cur), pl.BlockSpec(*prv)
        in_specs += [q_spec, cs, ps, cs, ps]
        kd, vd = (t[di] if di else t[di].reshape(batch, seq, D_B) for t in kv)
        args += [q_arr, kd, kd, vd, vd]
    in_specs += [_const_spec((3, 2, 2 * BAND, 2 * BAND)), _layer_spec((1, D_B), layer)]
    args += [bias, g_ob]
    n_steps = batch * n_tiles * n_groups
    step = lambda b, j, g: (b * n_tiles + j) * n_groups + g
    w_out_specs, w_out_shapes = [], []
    for w in weights:
        rows, cols = w.shape[1] // n_steps, w.shape[2]
        in_specs.append(pl.BlockSpec((None, rows, cols), lambda b, j, g: (layer, step(b, j, g), 0)))
        w_out_specs.append(pl.BlockSpec((rows, cols), lambda b, j, g: (step(b, j, g), 0)))
        w_out_shapes.append(jax.ShapeDtypeStruct(w.shape[1:], BF16))
    args += list(weights)
    out, *w_bf16 = pl.pallas_call(
        _attn_prompt_body,
        grid=(batch, n_tiles, n_groups),
        in_specs=in_specs,
        out_specs=[pl.BlockSpec((None, TILE, D_B), lambda b, j, g: (b, j, 0))] + w_out_specs,
        out_shape=[jax.ShapeDtypeStruct((batch, seq, D_B), BF16)] + w_out_shapes,
        scratch_shapes=[pltpu.VMEM((N_SLABS, TILE, LANES), F32)] * 6,
        compiler_params=_params(3),
        name="attn_prompt",
    )(*args)
    return out.reshape(batch * seq, D_B), w_bf16


def _band_bias():
    kj = np.arange(2 * BAND)[None, :]
    rho = np.arange(BAND)
    out = []
    for runs in (16, 4, 1):
        per = BAND // runs
        qi = (runs * (rho % per) + rho // per)[:, None]
        ok = (kj >= qi) & (kj <= qi + BAND)
        out.append(np.stack([np.where(ok, 0.0, -np.inf), np.where(ok & (kj >= BAND), 0.0, -np.inf)]))
    return np.tile(np.stack(out), (1, 1, 2, 1)).astype(np.float32)


def _window_attention(kt_ref, vt_ref, kn_ref, vn_ref, q_ref, mult_ref, okt_ref, ovt_ref, o_ref,
                      seq_idx):
    n_heads = kt_ref.shape[0]
    n_feat = n_heads * HEAD_DIM
    stack = lambda w_ref, n_ref: jnp.concatenate(
        [w_ref[...].reshape(n_feat, MAX_WINDOW), n_ref[...].reshape(n_feat, LANES)], axis=1).astype(BF16)
    state = {}

    def scores():
        own = lax.broadcasted_iota(jnp.int32, (Q_ROWS * n_heads, LANES), 1) // T_NEW == seq_idx
        mult = jnp.concatenate([mult_ref[:, 0:MAX_WINDOW],
                                jnp.where(own, mult_ref[:, MAX_WINDOW:], 0.0)], axis=1)
        s = jnp.where(mult > 0, _dot(q_ref[...], stack(kt_ref, kn_ref)), -jnp.inf)
        state["p"] = mult * jnp.exp(s - jnp.max(s, axis=-1, keepdims=True))

    def output():
        p = state["p"]
        o = _dot_nt(p.astype(BF16), stack(vt_ref, vn_ref)) / jnp.sum(p, axis=-1, keepdims=True)
        for h in range(n_heads):
            o_ref[h] = o[Q_ROWS * h:Q_ROWS * (h + 1), HEAD_DIM * h:HEAD_DIM * (h + 1)]

    def shift():
        body = MAX_WINDOW - LANES
        new_shift = (LANES - T_NEW * (seq_idx + 1)) % LANES
        is_new = lax.broadcasted_iota(jnp.int32, (HEAD_DIM, LANES), 1) >= LANES - T_NEW
        for src_ref, new_ref, dst_ref in ((kt_ref, kn_ref, okt_ref), (vt_ref, vn_ref, ovt_ref)):
            for h in range(n_heads):
                y = pltpu.roll(src_ref[h], MAX_WINDOW - T_NEW, 1)
                dst_ref[h, :, 0:body] = y[:, 0:body]
                dst_ref[h, :, body:MAX_WINDOW] = jnp.where(
                    is_new, pltpu.roll(new_ref[h], new_shift, 1), y[:, body:])

    return scores, output, shift


def _sample_multiplicity(n_rows):
    def mult(d):
        ok = d >= 0
        return (ok & (d <= 128)).astype(np.int32) + (ok & (d % 4 == 0) & (d <= 512)) \
            + (ok & (d % 16 == 0) & (d <= 2048))
    t = (np.arange(n_rows) % Q_ROWS % T_NEW)[:, None]
    old = mult(MAX_WINDOW + t - np.arange(MAX_WINDOW)[None, :])
    new = mult(t - (np.arange(LANES) % T_NEW)[None, :])
    return np.concatenate([old, new], axis=1).astype(np.float32)


def _ffn_compute(x_ref, ya_ref, yb_ref, gob_ref, wout_ref, gffn_ref, wup_ref, wdown_ref, o_ref,
                 norm_b, between=lambda stage: None):
    yb = yb_ref[...]
    if norm_b:
        yb = _rms(yb, gob_ref[...])
    x1 = x_ref[...] + _dot(ya_ref[...], wout_ref[0:D_A, :]) \
        + _dot(yb.astype(BF16), wout_ref[D_A:D_A + D_B, :])
    hb = _rms(x1, gffn_ref[...]).astype(BF16)
    between(0)
    step = D_FF // 4
    mlp = None
    for c in range(4):
        f = _dot(hb, wup_ref[:, c * step:(c + 1) * step])
        f = jnp.square(jnp.maximum(f, 0.0)).astype(BF16)
        d = _dot(f, wdown_ref[c * step:(c + 1) * step, :])
        mlp = d if mlp is None else mlp + d
        between(c + 1)
    o_ref[...] = x1 + mlp


def _out_ffn_body(*refs, norm_b):
    _ffn_compute(*refs, norm_b)


def _ffn_window_body(*refs, steps_per_seq, layer, n_steps):
    n_ffn = 8
    ffn_in, (kt_hbm, vt_hbm, kn, vn, q, mult) = refs[:n_ffn], refs[n_ffn:n_ffn + 6]
    x_out, okt, ovt, o_att, kbuf, vbuf, sems = refs[-7:]
    step = pl.program_id(0)
    seq_idx = step // steps_per_seq
    hps = kbuf.shape[1]

    def fetch(s):
        slot = s % WINDOW_SLOTS
        heads = pl.ds((s % steps_per_seq) * hps, hps)
        return [pltpu.make_async_copy(hbm.at[layer, s // steps_per_seq, heads], buf.at[slot],
                                      sems.at[which, slot])
                for which, (hbm, buf) in enumerate(((kt_hbm, kbuf), (vt_hbm, vbuf)))]

    @pl.when(step == 0)
    def _():
        for s in range(min(WINDOW_SLOTS - 1, n_steps)):
            for c in fetch(s):
                c.start()

    @pl.when(step + WINDOW_SLOTS - 1 < n_steps)
    def _():
        for c in fetch(step + WINDOW_SLOTS - 1):
            c.start()

    for c in fetch(step):
        c.wait()
    kt, vt = kbuf.at[step % WINDOW_SLOTS], vbuf.at[step % WINDOW_SLOTS]
    scores, output, shift = _window_attention(kt, vt, kn, vn, q, mult, okt, ovt, o_att, seq_idx)
    stages = {0: scores, 2: output, 4: shift}
    _ffn_compute(*ffn_in, x_out, False, between=lambda stage: stages.get(stage, lambda: None)())


def _ffn_specs(rows, layer):
    row_spec = lambda w: pl.BlockSpec((rows, w), lambda i: (i, 0))
    lspec = lambda *shape: _layer_spec(shape, layer)
    return [row_spec(D_MODEL), row_spec(D_A), row_spec(D_B), lspec(1, D_B),
            _const_spec((D_A + D_B, D_MODEL)), lspec(1, D_MODEL),
            _const_spec((D_MODEL, D_FF)), _const_spec((D_FF, D_MODEL))], row_spec(D_MODEL)


def _out_ffn(x, ya, yb, pw, layer, weights, *, rows, norm_b):
    n = x.shape[0]
    in_specs, out_spec = _ffn_specs(rows, layer)
    return pl.pallas_call(
        functools.partial(_out_ffn_body, norm_b=norm_b),
        grid=(n // rows,),
        in_specs=in_specs,
        out_specs=out_spec,
        out_shape=jax.ShapeDtypeStruct((n, D_MODEL), F32),
        compiler_params=_params(1),
        name="out_ffn",
    )(x, ya, yb, pw["g_ob"], weights[0], pw["g_ffn"], weights[1], weights[2])


def _ffn_window(x, ya, yb, pw, layer, weights, cache_kt, cache_vt, k_new, v_new, q, mult, prev_out,
                *, rows):
    n = x.shape[0]
    steps = n // rows
    depth, db = cache_kt.shape[:2]
    per_seq = steps // db
    hps = N_HEADS // per_seq
    assert per_seq * db == steps and hps * per_seq == N_HEADS
    assert q.shape == (db, per_seq, Q_ROWS * hps, HEAD_DIM * hps)
    in_specs, out_spec = _ffn_specs(rows, layer)
    cache_spec = pl.BlockSpec((None, None, hps, HEAD_DIM, MAX_WINDOW),
                              lambda i: (layer, i // per_seq, i % per_seq, 0, 0))
    new_spec = pl.BlockSpec((hps, HEAD_DIM, LANES), lambda i: (i % per_seq, 0, 0))
    q_spec = pl.BlockSpec((None, None) + q.shape[2:], lambda i: (i // per_seq, i % per_seq, 0, 0))
    hbm_spec = pl.BlockSpec(memory_space=pl.ANY)
    in_specs += [hbm_spec, hbm_spec, new_spec, new_spec, q_spec,
                 _const_spec((Q_ROWS * hps, MAX_WINDOW + LANES))]
    args = [x, ya, yb, pw["g_ob"], weights[0], pw["g_ffn"], weights[1], weights[2],
            cache_kt, cache_vt, k_new, v_new, q, mult]
    aliases = {}
    if prev_out is not None:
        in_specs += [pl.BlockSpec(memory_space=pl.ANY)] * 2
        aliases = {len(args): 1, len(args) + 1: 2}
        args += list(prev_out)
    win_shape = jax.ShapeDtypeStruct(cache_kt.shape, F32)
    return pl.pallas_call(
        functools.partial(_ffn_window_body, steps_per_seq=per_seq, layer=layer, n_steps=steps),
        grid=(steps,),
        in_specs=in_specs,
        out_specs=[out_spec, cache_spec, cache_spec,
                   pl.BlockSpec((None, hps, Q_ROWS, HEAD_DIM), lambda i: (i // per_seq, i % per_seq, 0, 0))],
        out_shape=[jax.ShapeDtypeStruct((n, D_MODEL), F32), win_shape, win_shape,
                   jax.ShapeDtypeStruct((db, N_HEADS, Q_ROWS, HEAD_DIM), F32)],
        input_output_aliases=aliases,
        scratch_shapes=[pltpu.VMEM((WINDOW_SLOTS, hps, HEAD_DIM, MAX_WINDOW), F32)] * 2
        + [pltpu.SemaphoreType.DMA((2, WINDOW_SLOTS))],
        compiler_params=_params(1),
        name="ffn_window",
    )(*args)


def _rope_tables(base_pos, offsets):
    half = ROT_DIM // 2
    inv = np.power(ROPE_THETA, -np.arange(half) * (2.0 / ROT_DIM))
    lane = np.arange(LANES) % HEAD_DIM
    freq = np.where(lane < ROT_DIM, inv[lane % half], 0.0)
    ang = lambda pos: np.asarray(pos, np.float64)[:, None] * freq[None, :]
    ab, ao = ang(base_pos), ang(offsets)
    tbase = np.stack([np.cos(ab), np.sin(ab)], axis=1)
    toff = np.stack([np.cos(ao), np.sin(ao)], axis=0)
    tsign = np.stack([-1.0 * (lane < half), 1.0 * ((lane >= half) & (lane < ROT_DIM))])
    return tuple(jnp.asarray(t, F32) for t in (tbase, toff, tsign))


def kernel(x_prompt, x_sample, cache_win_k, cache_win_v, attn_norm_g, w_in, v_norm_g, w_spatial, b_spatial, q_norm_g, k_norm_g, out_norm_a_g, out_norm_b_g, w_out, ffn_norm_g, w_up, w_down):
    depth = w_in.shape[0]
    batch, seq, _ = x_prompt.shape
    db, t_new, _ = x_sample.shape
    n_s = db * t_new
    assert n_s == CHUNK and t_new == T_NEW and cache_win_k.shape[2] == MAX_WINDOW
    assert seq % TILE == 0

    tab_p = _rope_tables(np.arange(seq // ROWS_IN_PROJ) * ROWS_IN_PROJ, np.arange(ROWS_IN_PROJ))
    tab_s = _rope_tables(np.full((1,), PAST_LEN), np.arange(n_s) % t_new)

    tril = jnp.tril(jnp.ones((CHUNK, CHUNK), bool))
    hsum = jnp.kron(jnp.eye(N_HEADS // 2, dtype=F32), jnp.full((HEAD_DIM, HEAD_DIM), 1.0 / HEAD_DIM, F32)).astype(BF16)
    row = lambda g: g[:, None, :]
    per_head = lambda g: jnp.tile(g, (1, N_HEADS))
    pw = dict(g_attn=row(attn_norm_g), w_in=w_in.astype(BF16), g_v=row(v_norm_g),
              g_q=row(per_head(q_norm_g)), g_k=row(per_head(k_norm_g)), hsum=hsum,
              g_oa=row(out_norm_a_g), g_ob=row(out_norm_b_g), g_ffn=row(ffn_norm_g))
    pair = lambda w: w.reshape(depth, N_SLABS, 2, CHUNK, CHUNK).transpose(0, 1, 3, 2, 4) \
        .reshape(depth, N_SLABS, CHUNK, 2 * CHUNK).astype(BF16)
    gate_w_p = pair(jnp.where(tril, w_spatial, 0))
    gate_b_p = jnp.repeat(b_spatial.transpose(0, 2, 1), HEAD_DIM, axis=2)
    t_of = np.arange(CHUNK) % t_new
    same_seq = (np.arange(CHUNK)[:, None] // t_new) == (np.arange(CHUNK)[None, :] // t_new)
    causal_s = jnp.asarray(same_seq & (t_of[:, None] >= t_of[None, :]))
    onehot = jnp.asarray(t_of[:, None] == np.arange(t_new)[None, :], F32)
    w_rep = jnp.einsum("ia,lhab,jb->lhij", onehot, w_spatial[:, :, :t_new, :t_new], onehot,
                       precision=lax.Precision.HIGHEST)
    gate_w_s = pair(jnp.where(causal_s, w_rep, 0))
    b_rep = jnp.einsum("ia,lha->lih", onehot, b_spatial[:, :, :t_new], precision=lax.Precision.HIGHEST)
    gate_b_s = jnp.repeat(b_rep, HEAD_DIM, axis=2)

    bias = jnp.asarray(_band_bias())
    per_seq = batch * seq // ROWS_FFN // db
    hps = N_HEADS // per_seq
    mult = jnp.asarray(_sample_multiplicity(Q_ROWS * hps))
    eye_h = jnp.eye(hps, dtype=F32)[None, None, :, None, :, None]

    cache_kt = cache_win_k.transpose(0, 1, 3, 4, 2)
    cache_vt = cache_win_v.transpose(0, 1, 3, 4, 2)

    xp = x_prompt.reshape(batch * seq, D_MODEL)
    xs = x_sample.reshape(n_s, D_MODEL)
    cv_list = []
    win_p = win_s = None
    for l in range(depth):
        ya_p, q16, *kv, kwin, vwin = _in_proj(xp, pw, l, tab_p, gate_w_p, gate_b_p, rows=ROWS_IN_PROJ,
                                              batch=batch, seq=seq, prev_win=win_p)
        win_p = (kwin, vwin)
        yb_p, w_l = _attn_prompt(q16, (kv[0:3], kv[3:6]), bias, pw["g_ob"], l, batch, seq,
                                 (w_out, w_up, w_down))
        ya_s, q, kt, vt, va = _in_proj(xs, pw, l, tab_s, gate_w_s, gate_b_s, rows=CHUNK)
        cv_list.append(va.reshape(db, t_new, D_A))
        q8 = jnp.tile(q.reshape(db, t_new, per_seq, hps, HEAD_DIM).transpose(0, 2, 3, 1, 4), (1, 1, 1, 2, 1))
        q_blk = (q8[:, :, :, :, None, :] * eye_h.astype(BF16)).reshape(db, per_seq, Q_ROWS * hps, hps * HEAD_DIM)
        heads = lambda t: t.reshape(N_HEADS, HEAD_DIM, n_s)

        xp, okt, ovt, o = _ffn_window(xp, ya_p, yb_p, pw, l, w_l, cache_kt, cache_vt, heads(kt),
                                      heads(vt), q_blk, mult, win_s, rows=ROWS_FFN)
        win_s = (okt, ovt)
        yb_s = o[:, :, :t_new].transpose(0, 2, 1, 3).reshape(n_s, D_B)
        xs = _out_ffn(xs, ya_s, yb_s, pw, l, w_l, rows=CHUNK, norm_b=True)

    win_p = [w.reshape(depth, batch, N_HEADS, HEAD_DIM, MAX_WINDOW) for w in win_p]
    unt = lambda w: w.transpose(0, 1, 4, 2, 3)
    return (xp.reshape(batch, seq, D_MODEL), xs.reshape(db, t_new, D_MODEL),
            unt(win_p[0]), unt(win_p[1]), unt(win_s[0]), unt(win_s[1]),
            jnp.stack(cv_list))
```
